```python
import math
import jax, jax.numpy as jnp
from jax import lax
import numpy as np

D_MODEL = 1024
BATCH = 1
SEQ = 16384
DEPTH = 1
DEC_BATCH = 128
DEC_SEQ = 4
PAST_LEN = 8192
PAGE_SIZE = 128

ATT_HEADS = 12
HEAD_DIM = 64
ATT_W = ATT_HEADS * HEAD_DIM
DILATED_BRANCHES = ((128, 1), (512, 4), (2048, 16))
WIN_MAX = 2048
QBLK = 128
N_BUCKETS = 32
MAX_EXACT = N_BUCKETS // 2
MAX_DIST = WIN_MAX
SSD_HEADS = 20
SSD_HEAD_DIM = 64
SSD_W = SSD_HEADS * SSD_HEAD_DIM
N_GROUPS = 4
D_STATE = 128
CONV_W = 4
CONV_DIM = SSD_W + 2 * N_GROUPS * D_STATE
SSD_CHUNK = 128
MIX_W = ATT_W + SSD_W
SPLITS = (ATT_W, 2 * ATT_W, 3 * ATT_W, 3 * ATT_W + SSD_W, 3 * ATT_W + SSD_W + CONV_DIM)
D_IN_PROJ = 3 * ATT_W + SSD_W + CONV_DIM + SSD_HEADS
D_FF = ((8 * D_MODEL // 3 + 255) // 256) * 256
EPS = 1e-6

kernel_name = 'dilated_ssd_hybrid_step'


def rmsnorm(x, g):
    xf = x.astype(jnp.float32)
    y = xf * lax.rsqrt(jnp.mean(xf * xf, axis=-1, keepdims=True) + EPS)
    return (y * g.astype(jnp.float32)).astype(x.dtype)


def rel_bucket(dist):
    n = jnp.maximum(dist, 0)
    nf = jnp.maximum(n, 1).astype(jnp.float32)
    large = MAX_EXACT + (jnp.log(nf / MAX_EXACT) / math.log(MAX_DIST / MAX_EXACT)
                         * (N_BUCKETS - MAX_EXACT)).astype(jnp.int32)
    return jnp.where(n < MAX_EXACT, n, jnp.minimum(large, N_BUCKETS - 1))


def merge_branches(parts):
    mx = jnp.stack([p[0] for p in parts])
    den = jnp.stack([p[1] for p in parts])
    num = jnp.stack([p[2] for p in parts])
    wts = jnp.exp(mx - jnp.max(mx, axis=0))
    return jnp.sum(wts[..., None] * num, axis=0) / jnp.sum(wts * den, axis=0)[..., None]


def dilated_branch_prompt(q, k, v, rel_bias, window, dilation):
    b, s, h, hd = q.shape
    w_sub = window // dilation
    m = s // dilation
    nb = -(-m // QBLK)
    m_pad = nb * QBLK

    def to_sub(t):
        t = t.reshape(b, m, dilation, h, hd).transpose(0, 2, 1, 3, 4)
        return jnp.pad(t, ((0, 0), (0, 0), (0, m_pad - m), (0, 0), (0, 0)))

    def band(t):
        tp = jnp.pad(t, ((0, 0), (0, 0), (QBLK, 0), (0, 0), (0, 0)))
        prev = tp[:, :, :m_pad].reshape(b, dilation, nb, QBLK, h, hd)
        cur = t.reshape(b, dilation, nb, QBLK, h, hd)
        return jnp.concatenate([prev, cur], axis=3)

    qb = to_sub(q).reshape(b, dilation, nb, QBLK, h, hd)
    kb = band(to_sub(k))
    vb = band(to_sub(v))
    scores = jnp.einsum('brnqhd,brnkhd->brnhqk', qb, kb,
                        preferred_element_type=jnp.float32) * (hd ** -0.5)
    qi = jnp.arange(QBLK)[:, None] + QBLK
    kk = jnp.arange(2 * QBLK)[None, :]
    dist = qi - kk
    bias = rel_bias[rel_bucket(dist * dilation)].transpose(2, 0, 1).astype(jnp.float32)
    key_idx = (jnp.arange(nb) * QBLK - QBLK)[:, None] + kk
    valid = ((dist >= 0) & (dist <= w_sub))[None] & (key_idx[:, None, :] >= 0)
    scores = jnp.where(valid[None, None, :, None], scores + bias, -jnp.inf)
    mx = jnp.max(scores, axis=-1)
    p = jnp.exp(scores - mx[..., None])
    den = jnp.sum(p, axis=-1)
    num = jnp.einsum('brnhqk,brnkhd->brnqhd', p.astype(v.dtype), vb,
                     preferred_element_type=jnp.float32)

    def from_sub(t):
        t = t.reshape(b, dilation, m_pad, *t.shape[4:])[:, :, :m]
        t = jnp.moveaxis(t, 1, 2)
        return t.reshape(b, s, *t.shape[3:])

    return (from_sub(jnp.moveaxis(mx, 3, 4)), from_sub(jnp.moveaxis(den, 3, 4)), from_sub(num))


def dilated_attention_prompt(q, k, v, rel_bias):
    return merge_branches([dilated_branch_prompt(q, k, v, rel_bias, w, d) for (w, d) in DILATED_BRANCHES])


def dilated_branch_sample(q, kcat, vcat, rel_bias, window, dilation):
    t = q.shape[1]
    wb = kcat.shape[1] - t
    n_keys = window // dilation + 1
    steps = jnp.arange(n_keys) * dilation
    idx = wb + jnp.arange(t)[:, None] - steps[None, :]
    valid = idx >= 0
    idx_c = jnp.maximum(idx, 0)
    kg = kcat[:, idx_c]
    vg = vcat[:, idx_c]
    scores = jnp.einsum('bthd,btkhd->bhtk', q, kg,
                        preferred_element_type=jnp.float32) * (q.shape[-1] ** -0.5)
    bias = rel_bias[rel_bucket(steps)].T.astype(jnp.float32)
    scores = jnp.where(valid[None, None], scores + bias[None, :, None, :], -jnp.inf)
    mx = jnp.max(scores, axis=-1)
    p = jnp.exp(scores - mx[..., None])
    den = jnp.sum(p, axis=-1)
    num = jnp.einsum('bhtk,btkhd->bthd', p.astype(vg.dtype), vg,
                     preferred_element_type=jnp.float32)
    return (mx.transpose(0, 2, 1), den.transpose(0, 2, 1), num)


def dilated_attention_sample(q, k, v, cache_k, cache_v, rel_bias):
    kcat = jnp.concatenate([cache_k.astype(k.dtype), k], axis=1)
    vcat = jnp.concatenate([cache_v.astype(v.dtype), v], axis=1)
    return merge_branches([dilated_branch_sample(q, kcat, vcat, rel_bias, w, d) for (w, d) in DILATED_BRANCHES])


def causal_conv(xbc, conv_state, conv_w, conv_b):
    L = xbc.shape[1]
    xp = jnp.concatenate([conv_state.astype(xbc.dtype), xbc], axis=1)
    y = sum(xp[:, j:j + L] * conv_w[j] for j in range(CONV_W)) + conv_b
    return jax.nn.silu(y), xp[:, -(CONV_W - 1):]


def ssd_scan(x, dt, a, bm, cm, h0):
    b, L, H, P = x.shape
    G, N = bm.shape[2], bm.shape[3]
    hpg = H // G
    cl = math.gcd(L, SSD_CHUNK)
    nc = L // cl
    xf = x.astype(jnp.float32).reshape(b, nc, cl, G, hpg, P)
    dtc = dt.reshape(b, nc, cl, G, hpg)
    bc = bm.astype(jnp.float32).reshape(b, nc, cl, G, N)
    cc = cm.astype(jnp.float32).reshape(b, nc, cl, G, N)
    acs = jnp.cumsum(dtc * a.reshape(G, hpg), axis=2)
    causal = jnp.tril(jnp.ones((cl, cl), dtype=bool))
    seg = acs[:, :, :, None] - acs[:, :, None, :]
    decay = jnp.exp(jnp.where(causal[None, None, :, :, None, None], seg, -jnp.inf))
    cb = jnp.einsum('bclgn,bcsgn->bclsg', cc, bc)
    w_ls = cb[..., None] * decay * dtc[:, :, None]
    y_diag = jnp.einsum('bclsgh,bcsghp->bclghp', w_ls, xf)
    decay_end = jnp.exp(acs[:, :, -1:] - acs) * dtc
    chunk_states = jnp.einsum('bclgn,bclghp->bcghpn', bc, decay_end[..., None] * xf)
    chunk_decay = jnp.exp(acs[:, :, -1])

    def step(h, inp):
        st, dec = inp
        return h * dec[..., None, None] + st, h

    h0g = h0.astype(jnp.float32).reshape(b, G, hpg, P, N)
    h_last, h_in = lax.scan(step, h0g, (jnp.moveaxis(chunk_states, 1, 0), jnp.moveaxis(chunk_decay, 1, 0)))
    h_in = jnp.moveaxis(h_in, 0, 1)
    y_off = jnp.einsum('bclgn,bcghpn->bclghp', cc, h_in) * jnp.exp(acs)[..., None]
    return (y_diag + y_off).reshape(b, L, H, P), h_last.reshape(b, H, P, N)


def ssd_mixer(z, xbc, dt_raw, conv_state, ssm_state, conv_w, conv_b, dt_bias, a_log, d_skip, g_ssd_out):
    b, L, _ = z.shape
    xbc, new_conv = causal_conv(xbc, conv_state, conv_w, conv_b)
    xs, bm, cm = jnp.split(xbc, (SSD_W, SSD_W + N_GROUPS * D_STATE), axis=-1)
    xs = xs.reshape(b, L, SSD_HEADS, SSD_HEAD_DIM)
    bm = bm.reshape(b, L, N_GROUPS, D_STATE)
    cm = cm.reshape(b, L, N_GROUPS, D_STATE)
    dt = jax.nn.softplus(dt_raw.astype(jnp.float32) + dt_bias.astype(jnp.float32))
    a = -jnp.exp(a_log.astype(jnp.float32))
    y, h_new = ssd_scan(xs, dt, a, bm, cm, ssm_state)
    y = y + d_skip.astype(jnp.float32)[:, None] * xs.astype(jnp.float32)
    y = y.reshape(b, L, SSD_W) * jax.nn.silu(z.astype(jnp.float32))
    yg = y.reshape(b, L, N_GROUPS, SSD_W // N_GROUPS)
    yg = yg * lax.rsqrt(jnp.mean(yg * yg, axis=-1, keepdims=True) + EPS)
    y = yg.reshape(b, L, SSD_W) * g_ssd_out.astype(jnp.float32)
    return y.astype(z.dtype), new_conv, h_new.astype(ssm_state.dtype)


def hybrid_layer(x, attend, conv_state, ssm_state, g_mix, w_in, conv_w, conv_b, dt_bias, a_log, d_skip,
                 g_attn_out, g_ssd_out, w_out, g_ffn, w_gate, w_up, w_down):
    b, L, _ = x.shape
    h = rmsnorm(x, g_mix)
    q, k, v, z, xbc, dt_raw = jnp.split(h @ w_in, SPLITS, axis=-1)
    q = q.reshape(b, L, ATT_HEADS, HEAD_DIM)
    k = k.reshape(b, L, ATT_HEADS, HEAD_DIM)
    v = v.reshape(b, L, ATT_HEADS, HEAD_DIM)
    att = rmsnorm(attend(q, k, v).reshape(b, L, ATT_W).astype(x.dtype), g_attn_out)
    ssd, new_conv, new_ssm = ssd_mixer(z, xbc, dt_raw, conv_state, ssm_state, conv_w, conv_b,
                                       dt_bias, a_log, d_skip, g_ssd_out)
    x = x + jnp.concatenate([att, ssd], axis=-1) @ w_out
    h2 = rmsnorm(x, g_ffn)
    x = x + (jax.nn.silu(h2 @ w_gate) * (h2 @ w_up)) @ w_down
    return x, k, v, new_conv, new_ssm


def setup_inputs(seed: int = 0) -> dict:
    key = jax.random.key(seed)
    ks = jax.random.split(key, 22)
    f32 = jnp.float32

    def nrm(k, shape, scale):
        return jax.random.normal(k, shape, f32) * scale

    def gain(k, shape):
        return 1.0 + 0.05 * jax.random.normal(k, shape, f32)

    wb = min(WIN_MAX, PAST_LEN)
    dt0 = jnp.exp(jax.random.uniform(ks[11], (DEPTH, SSD_HEADS), f32)
                  * (math.log(0.1) - math.log(0.001)) + math.log(0.001))
    return {
        'x_prompt': nrm(ks[0], (BATCH, SEQ, D_MODEL), 1.0),
        'x_sample': nrm(ks[1], (DEC_BATCH, DEC_SEQ, D_MODEL), 1.0),
        'cache_k': nrm(ks[2], (DEPTH, DEC_BATCH, wb, ATT_HEADS, HEAD_DIM), 1.0),
        'cache_v': nrm(ks[3], (DEPTH, DEC_BATCH, wb, ATT_HEADS, HEAD_DIM), 1.0),
        'state_conv': nrm(ks[4], (DEPTH, DEC_BATCH, CONV_W - 1, CONV_DIM), 1.0),
        'state_ssm': nrm(ks[5], (DEPTH, DEC_BATCH, SSD_HEADS, SSD_HEAD_DIM, D_STATE), 0.3),
        'rel_bias': nrm(ks[6], (N_BUCKETS, ATT_HEADS), 0.5),
        'g_mix': gain(ks[7], (DEPTH, D_MODEL)),
        'w_in': nrm(ks[8], (DEPTH, D_MODEL, D_IN_PROJ), D_MODEL ** -0.5),
        'conv_w': nrm(ks[9], (DEPTH, CONV_W, CONV_DIM), 0.5 * CONV_W ** -0.5),
        'conv_b': nrm(ks[10], (DEPTH, CONV_DIM), 0.01),
        'dt_bias': dt0 + jnp.log(-jnp.expm1(-dt0)),
        'a_log': jnp.log(jax.random.uniform(ks[12], (DEPTH, SSD_HEADS), f32, minval=1.0, maxval=16.0)),
        'd_skip': 1.0 + 0.1 * jax.random.normal(ks[13], (DEPTH, SSD_HEADS), f32),
        'g_attn_out': gain(ks[14], (DEPTH, ATT_W)),
        'g_ssd_out': gain(ks[15], (DEPTH, SSD_W)),
        'w_out': nrm(ks[16], (DEPTH, MIX_W, D_MODEL), MIX_W ** -0.5),
        'g_ffn': gain(ks[17], (DEPTH, D_MODEL)),
        'w_gate': nrm(ks[18], (DEPTH, D_MODEL, D_FF), D_MODEL ** -0.5),
        'w_up': nrm(ks[19], (DEPTH, D_MODEL, D_FF), D_MODEL ** -0.5),
        'w_down': nrm(ks[20], (DEPTH, D_FF, D_MODEL), D_FF ** -0.5),
        'g_final': gain(ks[21], (D_MODEL,)),
    }


def reference(x_prompt, x_sample, cache_k, cache_v, state_conv, state_ssm, rel_bias, g_mix, w_in,
              conv_w, conv_b, dt_bias, a_log, d_skip, g_attn_out, g_ssd_out, w_out, g_ffn,
              w_gate, w_up, w_down, g_final):
    xp, xs = x_prompt, x_sample
    kp_l, vp_l, cp_l, sp_l, ks_l, vs_l, cs_l, ss_l = [], [], [], [], [], [], [], []
    for l in range(DEPTH):
        lp = (g_mix[l], w_in[l], conv_w[l], conv_b[l], dt_bias[l], a_log[l], d_skip[l],
              g_attn_out[l], g_ssd_out[l], w_out[l], g_ffn[l], w_gate[l], w_up[l], w_down[l])
        conv0 = jnp.zeros((xp.shape[0], CONV_W - 1, CONV_DIM), xp.dtype)
        ssm0 = jnp.zeros((xp.shape[0], SSD_HEADS, SSD_HEAD_DIM, D_STATE), state_ssm.dtype)
        xp, kp, vp, cp, sp = hybrid_layer(
            xp, lambda q, k, v: dilated_attention_prompt(q, k, v, rel_bias), conv0, ssm0, *lp)
        ck, cv = cache_k[l], cache_v[l]
        xs, kn, vn, cs, ss = hybrid_layer(
            xs, lambda q, k, v, ck=ck, cv=cv: dilated_attention_sample(q, k, v, ck, cv, rel_bias),
            state_conv[l], state_ssm[l], *lp)
        wbp = min(WIN_MAX, kp.shape[1])
        kp_l.append(kp[:, -wbp:])
        vp_l.append(vp[:, -wbp:])
        cp_l.append(cp)
        sp_l.append(sp)
        ks_l.append(kn)
        vs_l.append(vn)
        cs_l.append(cs)
        ss_l.append(ss)
    y_prompt = rmsnorm(xp, g_final)
    y_sample = rmsnorm(xs, g_final)
    win_k_prompt = jnp.stack(kp_l)
    win_v_prompt = jnp.stack(vp_l)
    conv_prompt = jnp.stack(cp_l)
    ssm_prompt = jnp.stack(sp_l)
    win_k_sample = jnp.stack(ks_l)
    win_v_sample = jnp.stack(vs_l)
    conv_sample = jnp.stack(cs_l)
    ssm_sample = jnp.stack(ss_l)
    return (y_prompt, y_sample, win_k_prompt, win_v_prompt, conv_prompt, ssm_prompt,
            win_k_sample, win_v_sample, conv_sample, ssm_sample)
```

```python
import functools
import math

import jax
import jax.numpy as jnp
from jax import lax
from jax.experimental import pallas as pl
from jax.experimental.pallas import tpu as pltpu

F32 = jnp.float32
BF16 = jnp.bfloat16

D_MODEL = 1024
ATT_HEADS = 12
HEAD_DIM = 64
ATT_W = ATT_HEADS * HEAD_DIM
DILATIONS = (1, 4, 16)
QBLK = 128
WIN_MAX = 2048
N_BUCKETS = 32
MAX_EXACT = N_BUCKETS // 2
SSD_HEADS = 20
SSD_HEAD_DIM = 64
SSD_W = SSD_HEADS * SSD_HEAD_DIM
N_GROUPS = 4
D_STATE = 128
CONV_W = 4
CONV_DIM = SSD_W + 2 * N_GROUPS * D_STATE
SSD_CHUNK = 128
EPS = 1e-6

LANES = 128
VMEM_LIMIT = 48 * 1024 * 1024


def _rms(x, g):
    return x * lax.rsqrt(jnp.mean(x * x, axis=-1, keepdims=True) + EPS) * g


def _const_spec(shape):
    nd = len(shape)
    return pl.BlockSpec(shape, lambda *_: (0,) * nd, pipeline_mode=pl.Buffered(1))


def _inproj_kernel(x_ref, g_ref, wqkv_ref, wz_ref, wxbc_ref, wdt_ref,
                   qkv_ref, z_ref, xbc_ref, dt_ref):
    h = _rms(x_ref[...], g_ref[...]).astype(BF16)
    qkv_ref[...] = jnp.dot(h, wqkv_ref[...], preferred_element_type=F32)
    z_ref[...] = jnp.dot(h, wz_ref[...], preferred_element_type=F32)
    xbc_ref[...] = jnp.dot(h, wxbc_ref[...], preferred_element_type=F32)
    dt_ref[...] = jnp.dot(h, wdt_ref[...], preferred_element_type=F32)


def _inproj(x, g_mix, wqkv, wz, wxbc, wdt, tm=256):
    t = x.shape[0]
    assert t % tm == 0
    row = lambda n: pl.BlockSpec((tm, n), lambda i: (i, 0))
    return pl.pallas_call(
        _inproj_kernel,
        grid=(t // tm,),
        in_specs=[row(D_MODEL), _const_spec((1, D_MODEL)), _const_spec(wqkv.shape),
                  _const_spec(wz.shape), _const_spec(wxbc.shape), _const_spec(wdt.shape)],
        out_specs=[row(3 * ATT_W), row(SSD_W), row(CONV_DIM), row(LANES)],
        out_shape=[jax.ShapeDtypeStruct((t, 3 * ATT_W), F32), jax.ShapeDtypeStruct((t, SSD_W), F32),
                   jax.ShapeDtypeStruct((t, CONV_DIM), F32), jax.ShapeDtypeStruct((t, LANES), F32)],
        compiler_params=pltpu.CompilerParams(dimension_semantics=("parallel",),
                                             vmem_limit_bytes=VMEM_LIMIT),
        name="inproj",
    )(x, g_mix, wqkv, wz, wxbc, wdt)


def _tail_kernel(x_ref, att_ref, ssd_ref, gatt_ref, woa_ref, wos_ref, gffn_ref,
                 wg_ref, wu_ref, wd_ref, gfin_ref, y_ref, *, final_norm):
    an = _rms(att_ref[...], gatt_ref[...]).astype(BF16)
    mix = jnp.dot(an, woa_ref[...], preferred_element_type=F32)
    mix = mix + jnp.dot(ssd_ref[...].astype(BF16), wos_ref[...], preferred_element_type=F32)
    x1 = x_ref[...] + mix
    h2 = _rms(x1, gffn_ref[...]).astype(BF16)
    gate = jnp.dot(h2, wg_ref[...], preferred_element_type=F32)
    up = jnp.dot(h2, wu_ref[...], preferred_element_type=F32)
    act = (gate * jax.nn.sigmoid(gate) * up).astype(BF16)
    x2 = x1 + jnp.dot(act, wd_ref[...], preferred_element_type=F32)
    y_ref[...] = _rms(x2, gfin_ref[...]) if final_norm else x2


def _tail(x, att, ssd, g_att, woa, wos, g_ffn, wg, wu, wd, g_fin, final_norm, tm=256):
    t = x.shape[0]
    assert t % tm == 0
    row = lambda n: pl.BlockSpec((tm, n), lambda i: (i, 0))
    consts = [g_att, woa, wos, g_ffn, wg, wu, wd, g_fin]
    return pl.pallas_call(
        functools.partial(_tail_kernel, final_norm=final_norm),
        grid=(t // tm,),
        in_specs=[row(D_MODEL), row(ATT_W), row(SSD_W)] + [_const_spec(c.shape) for c in consts],
        out_specs=row(D_MODEL),
        out_shape=jax.ShapeDtypeStruct((t, D_MODEL), F32),
        compiler_params=pltpu.CompilerParams(dimension_semantics=("parallel",),
                                             vmem_limit_bytes=VMEM_LIMIT),
        name="tail",
    )(x, att, ssd, *consts)


TQ = QBLK * max(DILATIONS)
NEG_INF = float("-inf")


def _rel_bucket(dist):
    n = jnp.maximum(dist, 0)
    nf = jnp.maximum(n, 1).astype(F32)
    large = MAX_EXACT + (jnp.log(nf / MAX_EXACT) / math.log(WIN_MAX / MAX_EXACT)
                         * (N_BUCKETS - MAX_EXACT)).astype(jnp.int32)
    return jnp.where(n < MAX_EXACT, n, jnp.minimum(large, N_BUCKETS - 1))


def _band_buckets():
    qi = jnp.arange(QBLK)[:, None] + QBLK
    kk = jnp.arange(2 * QBLK)[None, :]
    dist = qi - kk
    valid = (dist >= 0) & (dist <= QBLK)
    return jnp.stack([jnp.where(valid, _rel_bucket(dist * d), -1) for d in DILATIONS]).astype(jnp.int32)


def _attn_prompt_kernel(tbl_ref, q_ref, kp_ref, kc_ref, vp_ref, vc_ref, bkt_ref, o_ref,
                        kbuf, vbuf, bias, macc, nacc, dacc):
    hp = pl.program_id(0)
    t = pl.program_id(1)

    @pl.when(t == 0)
    def _build_bias():
        for b in range(len(DILATIONS)):
            bk = bkt_ref[b]
            for hh in range(2):
                acc = jnp.full((QBLK, 2 * QBLK), NEG_INF, F32)
                for u in range(N_BUCKETS):
                    acc = jnp.where(bk == u, tbl_ref[u, 2 * hp + hh], acc)
                bias[b, hh] = acc

    kbuf[0:TQ] = kp_ref[...]
    kbuf[TQ:2 * TQ] = kc_ref[...]
    vbuf[0:TQ] = vp_ref[...]
    vbuf[TQ:2 * TQ] = vc_ref[...]

    left = lax.broadcasted_iota(jnp.int32, (1, LANES), 1) < HEAD_DIM
    prev_half = lax.broadcasted_iota(jnp.int32, (1, 2 * QBLK), 1) < QBLK
    scale = HEAD_DIM ** -0.5

    for b, d in enumerate(DILATIONS):
        nbr = max(DILATIONS) // d

        def rows(start, size, d=d):
            return pl.ds(start, size) if d == 1 else pl.ds(start, size, stride=d)

        def block(j, carry, b=b, d=d, nbr=nbr, rows=rows):
            r = j // nbr
            m = j % nbr
            qs = r + d * QBLK * m
            ks = TQ + qs - d * QBLK
            q = q_ref[rows(qs, QBLK), :] * scale
            kc = kbuf[rows(ks, 2 * QBLK), :].astype(BF16)
            vc = vbuf[rows(ks, 2 * QBLK), :]
            pen = jnp.where(prev_half & (m == 0) & (t == 0), NEG_INF, 0.0)
            res, mx = [], []
            for hh in range(2):
                sel = left if hh == 0 else jnp.logical_not(left)
                qh = jnp.where(sel, q, 0.0).astype(BF16)
                s = lax.dot_general(qh, kc, (((1,), (1,)), ((), ())), preferred_element_type=F32)
                s = s + bias[b, hh] + pen
                mh = jnp.max(s, axis=-1, keepdims=True)
                p = jnp.exp(s - mh).astype(BF16)
                vh = jnp.where(sel, vc, 1.0).astype(BF16)
                res.append(jnp.dot(p, vh, preferred_element_type=F32))
                mx.append(mh)
            nacc[b, rows(qs, QBLK), :] = jnp.where(left, res[0], res[1])
            dacc[b, rows(qs, QBLK), :] = pltpu.roll(jnp.where(left, res[1], res[0]), HEAD_DIM, axis=1)
            macc[b, rows(qs, QBLK), :] = jnp.where(left, mx[0], mx[1])
            return carry

        lax.fori_loop(0, max(DILATIONS), block, 0)

    def merge(c, carry):
        sl = pl.ds(pl.multiple_of(c * QBLK, QBLK), QBLK)
        ms = [macc[b, sl, :] for b in range(len(DILATIONS))]
        top = jnp.maximum(jnp.maximum(ms[0], ms[1]), ms[2])
        ws = [jnp.exp(mb - top) for mb in ms]
        num = sum(w * nacc[b, sl, :] for b, w in enumerate(ws))
        den = sum(w * dacc[b, sl, :] for b, w in enumerate(ws))
        o_ref[sl, :] = num / den
        return carry

    lax.fori_loop(0, TQ // QBLK, merge, 0)


def _attn_prompt(qkv, rel_bias):
    s = qkv.shape[0]
    assert s % TQ == 0
    nb = len(DILATIONS)
    hp_n = ATT_W // LANES
    tile = lambda f: pl.BlockSpec((TQ, LANES), f)
    return pl.pallas_call(
        _attn_prompt_kernel,
        grid=(hp_n, s // TQ),
        in_specs=[pl.BlockSpec(memory_space=pltpu.SMEM),
                  tile(lambda h, t: (t, h)),
                  tile(lambda h, t: (jnp.maximum(t - 1, 0), hp_n + h)),
                  tile(lambda h, t: (t, hp_n + h)),
                  tile(lambda h, t: (jnp.maximum(t - 1, 0), 2 * hp_n + h)),
                  tile(lambda h, t: (t, 2 * hp_n + h)),
                  _const_spec((nb, QBLK, 2 * QBLK))],
        out_specs=tile(lambda h, t: (t, h)),
        out_shape=jax.ShapeDtypeStruct((s, ATT_W), F32),
        scratch_shapes=[pltpu.VMEM((2 * TQ, LANES), F32), pltpu.VMEM((2 * TQ, LANES), F32),
                        pltpu.VMEM((nb, 2, QBLK, 2 * QBLK), F32),
                        pltpu.VMEM((nb, TQ, LANES), F32), pltpu.VMEM((nb, TQ, LANES), F32),
                        pltpu.VMEM((nb, TQ, LANES), F32)],
        compiler_params=pltpu.CompilerParams(dimension_semantics=("parallel", "arbitrary"),
                                             vmem_limit_bytes=VMEM_LIMIT),
        name="attn_prompt",
    )(rel_bias, qkv, qkv, qkv, qkv, qkv, _band_buckets())


HPG = SSD_HEADS // N_GROUPS
GROUP_W = SSD_W // N_GROUPS
GROUP_WIN = 3 * LANES
GROUP_LO = tuple((g * GROUP_W) // LANES * LANES for g in range(N_GROUPS))
CARRY_ROW = 8


def _split(x, terms):
    parts = []
    for _ in range(terms):
        p = x.astype(BF16)
        parts.append(p)
        x = x - p.astype(F32)
    return parts


def _dot_exact_rhs(x, m, terms=3):
    return sum(jnp.dot(p, m, preferred_element_type=F32) for p in _split(x, terms))


def _dot_exact_lhs(m, x, terms=3):
    return sum(jnp.dot(m, p, preferred_element_type=F32) for p in _split(x, terms))


def _softplus(x):
    return jnp.maximum(x, 0.0) + jnp.log1p(jnp.exp(-jnp.abs(x)))


def _silu(x):
    return x * jax.nn.sigmoid(x)


def _ssd_consts():
    c = jnp.arange(SSD_W)
    head_of = (c // SSD_HEAD_DIM)[None, :] == jnp.arange(LANES)[:, None]
    group_of = (c // GROUP_W)[None, :] == jnp.arange(LANES)[:, None]
    return head_of.astype(BF16), group_of.astype(BF16), group_of.T.astype(BF16)


def _conv_silu(cbuf, xbc_ref, w_ref, b_ref, rows):
    cbuf[CARRY_ROW:CARRY_ROW + rows, :] = xbc_ref[...]
    acc = b_ref[...]
    for j in range(CONV_W):
        acc = acc + cbuf[pl.ds(CARRY_ROW - (CONV_W - 1) + j, rows), :] * w_ref[j:j + 1, :]
    return _silu(acc)


def _ssd_ydiag(xs, bm, cm, acs, dt, mask):
    cbs = [lax.dot_general(cm[:, g * D_STATE:(g + 1) * D_STATE], bm[:, g * D_STATE:(g + 1) * D_STATE],
                           (((1,), (1,)), ((), ())), preferred_element_type=F32) for g in range(N_GROUPS)]
    acs_t = acs.T
    dt_t = dt.T
    left = lax.broadcasted_iota(jnp.int32, (1, LANES), 1) < SSD_HEAD_DIM
    pairs = []
    for pair in range(SSD_HEADS // 2):
        xpair = xs[:, pair * LANES:(pair + 1) * LANES]
        acc = None
        for hh in range(2):
            h = 2 * pair + hh
            seg = acs[:, h:h + 1] - acs_t[h:h + 1, :]
            dec = jnp.exp(jnp.where(mask, seg, NEG_INF))
            w = (cbs[h // HPG] * dec * dt_t[h:h + 1, :]).astype(BF16)
            xh = jnp.where(left if hh == 0 else jnp.logical_not(left), xpair, 0.0).astype(BF16)
            part = jnp.dot(w, xh, preferred_element_type=F32)
            acc = part if acc is None else acc + part
        pairs.append(acc)
    return jnp.concatenate(pairs, axis=1)


def _ssd_gate_norm(y, z, gred, gexp, gout):
    y = y * _silu(z)
    ms = _dot_exact_rhs(y * y, gred, terms=2) * (1.0 / GROUP_W)
    return y * _dot_exact_rhs(lax.rsqrt(ms + EPS), gexp) * gout


def _ssd_prompt_kernel(z_ref, xbc_ref, dt_ref, cw_ref, cb_ref, dtb_ref, alog_ref, dsk_ref, gout_ref,
                       hexp_ref, gexp_ref, gred_ref, y_ref, st_ref, cbuf, state, yoff):
    c = pl.program_id(0)
    cl = SSD_CHUNK

    @pl.when(c == 0)
    def _init():
        state[...] = jnp.zeros_like(state)
        cbuf[0:CARRY_ROW, :] = jnp.zeros((CARRY_ROW, CONV_DIM), F32)

    xc = _conv_silu(cbuf, xbc_ref, cw_ref, cb_ref, cl)
    cbuf[CARRY_ROW - (CONV_W - 1):CARRY_ROW, :] = xbc_ref[cl - (CONV_W - 1):cl, :]
    xs = xc[:, :SSD_W]
    bm = xc[:, SSD_W:SSD_W + N_GROUPS * D_STATE].astype(BF16)
    cm = xc[:, SSD_W + N_GROUPS * D_STATE:].astype(BF16)

    li = lax.broadcasted_iota(jnp.int32, (cl, cl), 0)
    si = lax.broadcasted_iota(jnp.int32, (cl, cl), 1)
    causal = li >= si

    dt = _softplus(dt_ref[...] + dtb_ref[...])
    acs = _dot_exact_lhs(causal.astype(BF16), dt * (-jnp.exp(alog_ref[...])))
    hexp = hexp_ref[...]
    e_in = _dot_exact_rhs(jnp.exp(acs), hexp)
    e_end = _dot_exact_rhs(jnp.exp(acs[cl - 1:cl, :] - acs) * dt, hexp)
    xw = (xs * e_end).astype(BF16)

    gmask = lax.broadcasted_iota(jnp.int32, (1, SSD_W), 1) // GROUP_W
    for g in range(N_GROUPS):
        cg = cm[:, g * D_STATE:(g + 1) * D_STATE]
        lo = GROUP_LO[g]
        win = slice(lo, lo + GROUP_WIN)
        inside = gmask[:, win] == g
        yo = jnp.dot(cg, state[:, win].astype(BF16), preferred_element_type=F32)
        yoff[:, win] = jnp.where(inside, yo, 0.0 if g == 0 else yoff[:, win])
    for g in range(N_GROUPS):
        bg = bm[:, g * D_STATE:(g + 1) * D_STATE]
        lo = GROUP_LO[g]
        win = slice(lo, lo + GROUP_WIN)
        inside = gmask[:, win] == g
        upd = lax.dot_general(bg, xw[:, win], (((0,), (0,)), ((), ())), preferred_element_type=F32)
        old = state[:, win]
        state[:, win] = jnp.where(inside, old * e_in[cl - 1:cl, win] + upd, old)

    ydiag = _ssd_ydiag(xs, bm, cm, acs, dt, causal)
    y_ref[...] = _ssd_gate_norm(ydiag + yoff[...] * e_in + xs * dsk_ref[...], z_ref[...],
                                gred_ref[...], gexp_ref[...], gout_ref[...])

    @pl.when(c == pl.num_programs(0) - 1)
    def _emit_state():
        for i in range(SSD_W // LANES):
            st_ref[i * LANES:(i + 1) * LANES, :] = state[:, i * LANES:(i + 1) * LANES].T


def _ssd_prompt(z, xbc, dt, conv_w, conv_b, dt_bias, a_log, dskip_x, g_out):
    s = z.shape[0]
    assert s % SSD_CHUNK == 0
    hexp, gexp, gred = _ssd_consts()
    row = lambda n: pl.BlockSpec((SSD_CHUNK, n), lambda i: (i, 0))
    consts = [conv_w, conv_b, dt_bias, a_log, dskip_x, g_out, hexp, gexp, gred]
    return pl.pallas_call(
        _ssd_prompt_kernel,
        grid=(s // SSD_CHUNK,),
        in_specs=[row(SSD_W), row(CONV_DIM), row(LANES)] + [_const_spec(a.shape) for a in consts],
        out_specs=[row(SSD_W), pl.BlockSpec((SSD_W, D_STATE), lambda i: (0, 0))],
        out_shape=[jax.ShapeDtypeStruct((s, SSD_W), F32), jax.ShapeDtypeStruct((SSD_W, D_STATE), F32)],
        scratch_shapes=[pltpu.VMEM((CARRY_ROW + SSD_CHUNK, CONV_DIM), F32),
                        pltpu.VMEM((D_STATE, SSD_W), F32), pltpu.VMEM((SSD_CHUNK, SSD_W), F32)],
        compiler_params=pltpu.CompilerParams(dimension_semantics=("arbitrary",),
                                             vmem_limit_bytes=VMEM_LIMIT),
        name="ssd_prompt",
    )(z, xbc, dt, *consts)


DEC_SEQ = 4
NEW_ROWS = 8
FAR_GROUPS = WIN_MAX // 16
MID_GROUPS = 512 // 4
NEAR_GROUPS = QBLK // 4
CAND = FAR_GROUPS + MID_GROUPS + 4 * NEAR_GROUPS + NEW_ROWS


def _sample_distances():
    far = WIN_MAX - 16 * jnp.arange(FAR_GROUPS)
    mid = 512 - 4 * jnp.arange(MID_GROUPS)
    dist, mult = [], []
    for t in range(DEC_SEQ):
        near = [QBLK + t - 4 * jnp.arange(NEAR_GROUPS) - u for u in range(4)]
        near = [jnp.where(n <= QBLK, n, -1) for n in near]
        u = jnp.arange(NEW_ROWS)
        new = jnp.where(u <= t, t - u, -1)
        dist.append(jnp.concatenate([far, mid] + near + [new]))
        mult.append(jnp.concatenate([jnp.ones((CAND - NEW_ROWS,), F32),
                                     jnp.where(u == t, float(len(DILATIONS)), 1.0)]))
    return jnp.stack(dist), jnp.stack(mult)


def _attn_sample_kernel(q_ref, k16_ref, k4_ref, v16_ref, v4_ref, bkt_ref, ladd_ref, tbl_ref,
                        eh_ref, eht_ref, o_ref, bias):
    @pl.when(pl.program_id(0) == 0)
    def _build_bias():
        for t in range(DEC_SEQ):
            bk = bkt_ref[t]
            acc = jnp.full((CAND, LANES), NEG_INF, F32)
            for u in range(N_BUCKETS):
                acc = jnp.where(bk == u, tbl_ref[u:u + 1, :], acc)
            bias[t] = acc + ladd_ref[t]

    scale = HEAD_DIM ** -0.5
    near = slice(MID_GROUPS - NEAR_GROUPS, MID_GROUPS)
    knew = q_ref[0, :, ATT_W:2 * ATT_W]
    vnew = q_ref[0, :, 2 * ATT_W:3 * ATT_W]
    outs = []
    for t in range(DEC_SEQ):
        col = slice(t * ATT_W, (t + 1) * ATT_W)
        cols = [slice(u * ATT_W, (u + 1) * ATT_W) for u in range(4)]
        q = q_ref[0, t:t + 1, 0:ATT_W] * scale
        ks = [k16_ref[0, :, col], k4_ref[0, :, col]] + [k4_ref[0, near, c] for c in cols] + [knew]
        vs = [v16_ref[0, :, col], v4_ref[0, :, col]] + [v4_ref[0, near, c] for c in cols] + [vnew]
        prod = jnp.concatenate([(k * q).astype(BF16) for k in ks], axis=0)
        s = jnp.dot(prod, eh_ref[...], preferred_element_type=F32) + bias[t]
        p = jnp.exp(s - jnp.max(s, axis=0, keepdims=True))
        den = jnp.sum(p, axis=0, keepdims=True)
        pe = jnp.dot(p.astype(BF16), eht_ref[...], preferred_element_type=F32)
        num = jnp.sum(pe * jnp.concatenate(vs, axis=0), axis=0, keepdims=True)
        outs.append(num / _dot_exact_rhs(den, eht_ref[...]))
    o_ref[0] = jnp.concatenate(outs, axis=0)


def _attn_sample(qkv, cache_k, cache_v, rel_bias):
    db, wb, _ = cache_k.shape
    assert wb == WIN_MAX and qkv.shape[0] == db * DEC_SEQ
    q8 = jnp.pad(qkv.reshape(db, DEC_SEQ, 3 * ATT_W), ((0, 0), (0, NEW_ROWS - DEC_SEQ), (0, 0)))
    dist, mult = _sample_distances()
    bkt = jnp.where(dist >= 0, _rel_bucket(dist), -1).astype(jnp.int32)
    bkt = jnp.broadcast_to(bkt[:, :, None], (DEC_SEQ, CAND, LANES))
    ladd = jnp.broadcast_to(jnp.log(mult)[:, :, None], (DEC_SEQ, CAND, LANES))
    tbl = jnp.pad(rel_bias, ((0, 0), (0, LANES - ATT_HEADS)))
    eh = (jnp.arange(ATT_W)[:, None] // HEAD_DIM == jnp.arange(LANES)[None, :]).astype(BF16)
    wide = 4 * ATT_W
    by16 = lambda a: a.reshape(db, WIN_MAX // 16, 16 * ATT_W)
    by4 = lambda a: a.reshape(db, WIN_MAX // 4, 4 * ATT_W)
    far_spec = pl.BlockSpec((1, FAR_GROUPS, wide), lambda b: (b, 0, 0))
    mid_spec = pl.BlockSpec((1, MID_GROUPS, wide), lambda b: (b, WIN_MAX // 4 // MID_GROUPS - 1, 0))
    consts = [bkt, ladd, tbl, eh, eh.T]
    out = pl.pallas_call(
        _attn_sample_kernel,
        grid=(db,),
        in_specs=[pl.BlockSpec((1, NEW_ROWS, 3 * ATT_W), lambda b: (b, 0, 0)),
                  far_spec, mid_spec, far_spec, mid_spec] + [_const_spec(c.shape) for c in consts],
        out_specs=pl.BlockSpec((1, DEC_SEQ, ATT_W), lambda b: (b, 0, 0)),
        out_shape=jax.ShapeDtypeStruct((db, DEC_SEQ, ATT_W), F32),
        scratch_shapes=[pltpu.VMEM((DEC_SEQ, CAND, LANES), F32)],
        compiler_params=pltpu.CompilerParams(dimension_semantics=("arbitrary",),
                                             vmem_limit_bytes=VMEM_LIMIT),
        name="attn_sample",
    )(q8, by16(cache_k), by4(cache_k), by16(cache_v), by4(cache_v), *consts)
    return out.reshape(db * DEC_SEQ, ATT_W)


SEQ_ROWS = 8
SEQ_PER_STEP = 8


def _ssd_sample_kernel(z_ref, xp_ref, dt_ref, st_ref, cw_ref, cb_ref, dtb_ref, alog_ref, dsk_ref, gout_ref,
                       hexp_ref, gexp_ref, gred_ref, y_ref, sto_ref, cbuf, yoff):
    rows = SEQ_PER_STEP * SEQ_ROWS
    cbuf[0:CARRY_ROW, :] = jnp.zeros((CARRY_ROW, CONV_DIM), F32)
    xc = _conv_silu(cbuf, xp_ref, cw_ref, cb_ref, rows)
    xs = xc[:, :SSD_W]
    bmf = xc[:, SSD_W:SSD_W + N_GROUPS * D_STATE]
    bm = bmf.astype(BF16)
    cm = xc[:, SSD_W + N_GROUPS * D_STATE:].astype(BF16)

    li = lax.broadcasted_iota(jnp.int32, (rows, rows), 0)
    si = lax.broadcasted_iota(jnp.int32, (rows, rows), 1)
    same = (li // SEQ_ROWS) == (si // SEQ_ROWS)
    mask = same & (li >= si)
    rowi = lax.broadcasted_iota(jnp.int32, (rows, 1), 0)
    is_token = (rowi % SEQ_ROWS) >= SEQ_ROWS - DEC_SEQ

    dt = jnp.where(is_token, _softplus(dt_ref[...] + dtb_ref[...]), 0.0)
    acs = _dot_exact_lhs(mask.astype(BF16), dt * (-jnp.exp(alog_ref[...])))
    acs_end = _dot_exact_lhs(same.astype(BF16), dt * (-jnp.exp(alog_ref[...])))
    hexp = hexp_ref[...]
    e_in = _dot_exact_rhs(jnp.exp(acs), hexp)
    e_end = _dot_exact_rhs(jnp.exp(acs_end - acs) * dt, hexp)
    e_all = _dot_exact_rhs(jnp.exp(acs_end), hexp)
    xw = xs * e_end

    sub = rowi % SEQ_ROWS
    pieces = [p.astype(F32) for p in _split(e_all, 3)]
    dstack = jnp.where(sub == 0, pieces[0], jnp.where(sub == 1, pieces[1], jnp.where(sub == 2, pieces[2], 0.0)))
    ones = jnp.ones((SEQ_ROWS, D_STATE), BF16)
    gmask = lax.broadcasted_iota(jnp.int32, (1, SSD_W), 1) // GROUP_W
    tn = (((0,), (0,)), ((), ()))
    nt = (((1,), (1,)), ((), ()))
    for i in range(SEQ_PER_STEP):
        rs = slice(i * SEQ_ROWS, (i + 1) * SEQ_ROWS)
        for g in range(N_GROUPS):
            lo = GROUP_LO[g]
            win = slice(lo, lo + GROUP_WIN)
            inside = gmask[:, win] == g
            yo = lax.dot_general(cm[rs, g * D_STATE:(g + 1) * D_STATE], st_ref[i, win, :].astype(BF16), nt,
                                 preferred_element_type=F32)
            yoff[rs, win] = jnp.where(inside, yo, 0.0 if g == 0 else yoff[rs, win])
        xstack = jnp.concatenate([jnp.where(gmask == g, xw[rs, :], 0.0) for g in range(N_GROUPS)], axis=0)
        bstack = jnp.concatenate([bmf[rs, g * D_STATE:(g + 1) * D_STATE] for g in range(N_GROUPS)], axis=0)
        upd = lax.dot_general(xstack.astype(BF16), bstack.astype(BF16), tn, preferred_element_type=F32)
        dcol = lax.dot_general(dstack[rs, :].astype(BF16), ones, tn, preferred_element_type=F32)
        sto_ref[i] = st_ref[i] * dcol + upd

    ydiag = _ssd_ydiag(xs, bm, cm, acs, dt, mask)
    y_ref[...] = _ssd_gate_norm(ydiag + yoff[...] * e_in + xs * dsk_ref[...], z_ref[...],
                                gred_ref[...], gexp_ref[...], gout_ref[...])


def _ssd_sample(z, xbc, dt, conv_state, ssm_state, conv_w, conv_b, dt_bias, a_log, dskip_x, g_out):
    db = conv_state.shape[0]
    assert db % SEQ_PER_STEP == 0 and z.shape[0] == db * DEC_SEQ
    lead = SEQ_ROWS - DEC_SEQ

    def padded(a):
        a = a.reshape(db, DEC_SEQ, a.shape[-1])
        return jnp.pad(a, ((0, 0), (lead, 0), (0, 0))).reshape(db * SEQ_ROWS, a.shape[-1])

    xp = jnp.concatenate([jnp.zeros((db, lead - (CONV_W - 1), CONV_DIM), F32), conv_state,
                          xbc.reshape(db, DEC_SEQ, CONV_DIM)], axis=1).reshape(db * SEQ_ROWS, CONV_DIM)
    hexp, gexp, gred = _ssd_consts()
    rows = SEQ_PER_STEP * SEQ_ROWS
    row = lambda n: pl.BlockSpec((rows, n), lambda i: (i, 0))
    st_spec = pl.BlockSpec((SEQ_PER_STEP, SSD_W, D_STATE), lambda i: (i, 0, 0))
    consts = [conv_w, conv_b, dt_bias, a_log, dskip_x, g_out, hexp, gexp, gred]
    y, st = pl.pallas_call(
        _ssd_sample_kernel,
        grid=(db // SEQ_PER_STEP,),
        in_specs=[row(SSD_W), row(CONV_DIM), row(LANES), st_spec] + [_const_spec(a.shape) for a in consts],
        out_specs=[row(SSD_W), st_spec],
        out_shape=[jax.ShapeDtypeStruct((db * SEQ_ROWS, SSD_W), F32),
                   jax.ShapeDtypeStruct((db, SSD_W, D_STATE), F32)],
        scratch_shapes=[pltpu.VMEM((CARRY_ROW + rows, CONV_DIM), F32), pltpu.VMEM((rows, SSD_W), F32)],
        compiler_params=pltpu.CompilerParams(dimension_semantics=("parallel",),
                                             vmem_limit_bytes=VMEM_LIMIT),
        name="ssd_sample",
    )(padded(z), xp, padded(dt), ssm_state, *consts)
    y = y.reshape(db, SEQ_ROWS, SSD_W)[:, lead:].reshape(db * DEC_SEQ, SSD_W)
    return y, st


def kernel(x_prompt, x_sample, cache_k, cache_v, state_conv, state_ssm, rel_bias, g_mix, w_in, conv_w, conv_b,
           dt_bias, a_log, d_skip, g_attn_out, g_ssd_out, w_out, g_ffn, w_gate, w_up, w_down, g_final):
    depth = w_in.shape[0]
    bp, sp, _ = x_prompt.shape
    db, ds, _ = x_sample.shape
    assert bp == 1 and ds == DEC_SEQ and sp >= WIN_MAX
    xp = x_prompt.reshape(bp * sp, D_MODEL)
    xs = x_sample.reshape(db * ds, D_MODEL)
    pad_heads = lambda v: jnp.pad(v, (0, LANES - SSD_HEADS))[None]
    o_z, o_xbc, o_dt = 3 * ATT_W, 3 * ATT_W + SSD_W, 3 * ATT_W + SSD_W + CONV_DIM
    outs = [[] for _ in range(8)]
    for l in range(depth):
        wi = w_in[l]
        wqkv, wz, wxbc = (wi[:, :o_z].astype(BF16), wi[:, o_z:o_xbc].astype(BF16), wi[:, o_xbc:o_dt].astype(BF16))
        wdt = jnp.pad(wi[:, o_dt:], ((0, 0), (0, LANES - SSD_HEADS))).astype(BF16)
        ssd_prm = (conv_w[l], conv_b[l][None], pad_heads(dt_bias[l]), pad_heads(a_log[l]),
                   jnp.repeat(d_skip[l], SSD_HEAD_DIM)[None], g_ssd_out[l][None])
        tail_prm = (g_attn_out[l][None], w_out[l][:ATT_W].astype(BF16), w_out[l][ATT_W:].astype(BF16),
                    g_ffn[l][None], w_gate[l].astype(BF16), w_up[l].astype(BF16), w_down[l].astype(BF16),
                    g_final[None])
        last = l == depth - 1

        qkv_p, z_p, xbc_p, dt_p = _inproj(xp, g_mix[l][None], wqkv, wz, wxbc, wdt)
        att_p = _attn_prompt(qkv_p, rel_bias)
        ssd_p, st_p = _ssd_prompt(z_p, xbc_p, dt_p, *ssd_prm)
        xp = _tail(xp, att_p, ssd_p, *tail_prm, final_norm=last)

        qkv_s, z_s, xbc_s, dt_s = _inproj(xs, g_mix[l][None], wqkv, wz, wxbc, wdt)
        att_s = _attn_sample(qkv_s, cache_k[l].reshape(db, -1, ATT_W), cache_v[l].reshape(db, -1, ATT_W), rel_bias)
        ssd_s, st_s = _ssd_sample(z_s, xbc_s, dt_s, state_conv[l], state_ssm[l].reshape(db, SSD_W, D_STATE),
                                  *ssd_prm)
        xs = _tail(xs, att_s, ssd_s, *tail_prm, final_norm=last)

        heads = lambda a: a.reshape(a.shape[0], a.shape[1], ATT_HEADS, HEAD_DIM)
        kv_p = qkv_p.reshape(bp, sp, 3 * ATT_W)[:, sp - WIN_MAX:]
        kv_s = qkv_s.reshape(db, ds, 3 * ATT_W)
        outs[0].append(heads(kv_p[..., ATT_W:2 * ATT_W]))
        outs[1].append(heads(kv_p[..., 2 * ATT_W:]))
        outs[2].append(xbc_p.reshape(bp, sp, CONV_DIM)[:, sp - (CONV_W - 1):])
        outs[3].append(st_p.reshape(bp, SSD_HEADS, SSD_HEAD_DIM, D_STATE))
        outs[4].append(heads(kv_s[..., ATT_W:2 * ATT_W]))
        outs[5].append(heads(kv_s[..., 2 * ATT_W:]))
        outs[6].append(xbc_s.reshape(db, ds, CONV_DIM)[:, ds - (CONV_W - 1):])
        outs[7].append(st_s.reshape(db, SSD_HEADS, SSD_HEAD_DIM, D_STATE))
    return (xp.reshape(bp, sp, D_MODEL), xs.reshape(db, ds, D_MODEL)) + tuple(jnp.stack(o) for o in outs)
```

```python
import functools
import math

import jax
import jax.numpy as jnp
from jax import lax
from jax.experimental import pallas as pl
from jax.experimental.pallas import tpu as pltpu

F32 = jnp.float32
BF16 = jnp.bfloat16

D_MODEL = 1024
ATT_HEADS = 12
HEAD_DIM = 64
ATT_W = ATT_HEADS * HEAD_DIM
DILATIONS = (1, 4, 16)
QBLK = 128
WIN_MAX = 2048
N_BUCKETS = 32
MAX_EXACT = N_BUCKETS // 2
SSD_HEADS = 20
SSD_HEAD_DIM = 64
SSD_W = SSD_HEADS * SSD_HEAD_DIM
N_GROUPS = 4
D_STATE = 128
CONV_W = 4
CONV_DIM = SSD_W + 2 * N_GROUPS * D_STATE
SSD_CHUNK = 128
EPS = 1e-6

LANES = 128
VMEM_LIMIT = 48 * 1024 * 1024


def _rms(x, g):
    return x * lax.rsqrt(jnp.mean(x * x, axis=-1, keepdims=True) + EPS) * g


def _const_spec(shape):
    nd = len(shape)
    return pl.BlockSpec(shape, lambda *_: (0,) * nd, pipeline_mode=pl.Buffered(1))


def _inproj_kernel(x_ref, g_ref, wqkv_ref, wz_ref, wxbc_ref, wdt_ref,
                   qkv_ref, z_ref, xbc_ref, dt_ref):
    h = _rms(x_ref[...], g_ref[...]).astype(BF16)
    qkv_ref[...] = jnp.dot(h, wqkv_ref[...], preferred_element_type=F32)
    z_ref[...] = jnp.dot(h, wz_ref[...], preferred_element_type=F32)
    xbc_ref[...] = jnp.dot(h, wxbc_ref[...], preferred_element_type=F32)
    dt_ref[...] = jnp.dot(h, wdt_ref[...], preferred_element_type=F32)


def _inproj(x, g_mix, wqkv, wz, wxbc, wdt, tm=256):
    t = x.shape[0]
    assert t % tm == 0
    row = lambda n: pl.BlockSpec((tm, n), lambda i: (i, 0))
    return pl.pallas_call(
        _inproj_kernel,
        grid=(t // tm,),
        in_specs=[row(D_MODEL), _const_spec((1, D_MODEL)), _const_spec(wqkv.shape),
                  _const_spec(wz.shape), _const_spec(wxbc.shape), _const_spec(wdt.shape)],
        out_specs=[row(3 * ATT_W), row(SSD_W), row(CONV_DIM), row(LANES)],
        out_shape=[jax.ShapeDtypeStruct((t, 3 * ATT_W), F32), jax.ShapeDtypeStruct((t, SSD_W), F32),
                   jax.ShapeDtypeStruct((t, CONV_DIM), F32), jax.ShapeDtypeStruct((t, LANES), F32)],
        compiler_params=pltpu.CompilerParams(dimension_semantics=("parallel",),
                                             vmem_limit_bytes=VMEM_LIMIT),
        name="inproj",
    )(x, g_mix, wqkv, wz, wxbc, wdt)


def _tail_kernel(x_ref, att_ref, ssd_ref, gatt_ref, woa_ref, wos_ref, gffn_ref,
                 wg_ref, wu_ref, wd_ref, gfin_ref, y_ref, *, final_norm):
    an = _rms(att_ref[...], gatt_ref[...]).astype(BF16)
    mix = jnp.dot(an, woa_ref[...], preferred_element_type=F32)
    mix = mix + jnp.dot(ssd_ref[...].astype(BF16), wos_ref[...], preferred_element_type=F32)
    x1 = x_ref[...] + mix
    h2 = _rms(x1, gffn_ref[...]).astype(BF16)
    gate = jnp.dot(h2, wg_ref[...], preferred_element_type=F32)
    up = jnp.dot(h2, wu_ref[...], preferred_element_type=F32)
    act = (gate * jax.nn.sigmoid(gate) * up).astype(BF16)
    x2 = x1 + jnp.dot(act, wd_ref[...], preferred_element_type=F32)
    y_ref[...] = _rms(x2, gfin_ref[...]) if final_norm else x2


def _tail(x, att, ssd, g_att, woa, wos, g_ffn, wg, wu, wd, g_fin, final_norm, tm=256):
    t = x.shape[0]
    assert t % tm == 0
    row = lambda n: pl.BlockSpec((tm, n), lambda i: (i, 0))
    consts = [g_att, woa, wos, g_ffn, wg, wu, wd, g_fin]
    return pl.pallas_call(
        functools.partial(_tail_kernel, final_norm=final_norm),
        grid=(t // tm,),
        in_specs=[row(D_MODEL), row(ATT_W), row(SSD_W)] + [_const_spec(c.shape) for c in consts],
        out_specs=row(D_MODEL),
        out_shape=jax.ShapeDtypeStruct((t, D_MODEL), F32),
        compiler_params=pltpu.CompilerParams(dimension_semantics=("parallel",),
                                             vmem_limit_bytes=VMEM_LIMIT),
        name="tail",
    )(x, att, ssd, *consts)


TQ = QBLK * max(DILATIONS)
NEG_INF = float("-inf")
BLOCK_UNROLL = 8


def _rel_bucket(dist):
    n = jnp.maximum(dist, 0)
    nf = jnp.maximum(n, 1).astype(F32)
    large = MAX_EXACT + (jnp.log(nf / MAX_EXACT) / math.log(WIN_MAX / MAX_EXACT)
                         * (N_BUCKETS - MAX_EXACT)).astype(jnp.int32)
    return jnp.where(n < MAX_EXACT, n, jnp.minimum(large, N_BUCKETS - 1))


def _band_buckets():
    qi = jnp.arange(QBLK)[:, None] + QBLK
    kk = jnp.arange(2 * QBLK)[None, :]
    dist = qi - kk
    valid = (dist >= 0) & (dist <= QBLK)
    return jnp.stack([jnp.where(valid, _rel_bucket(dist * d), -1) for d in DILATIONS]).astype(jnp.int32)


def _attn_prompt_kernel(tbl_ref, q_ref, kp_ref, kc_ref, vp_ref, vc_ref, bkt_ref, o_ref,
                        kbuf, vbuf, bias, macc, nacc, dacc):
    hp = pl.program_id(0)
    t = pl.program_id(1)

    @pl.when(t == 0)
    def _build_bias():
        for b in range(len(DILATIONS)):
            bk = bkt_ref[b]
            for hh in range(2):
                acc = jnp.full((QBLK, 2 * QBLK), NEG_INF, F32)
                for u in range(N_BUCKETS):
                    acc = jnp.where(bk == u, tbl_ref[u, 2 * hp + hh], acc)
                bias[b, hh] = acc

    kbuf[0:TQ] = kp_ref[...]
    kbuf[TQ:2 * TQ] = kc_ref[...]
    vbuf[0:TQ] = vp_ref[...]
    vbuf[TQ:2 * TQ] = vc_ref[...]

    left = lax.broadcasted_iota(jnp.int32, (1, LANES), 1) < HEAD_DIM
    prev_half = lax.broadcasted_iota(jnp.int32, (1, 2 * QBLK), 1) < QBLK
    scale = HEAD_DIM ** -0.5

    for b, d in enumerate(DILATIONS):
        nbr = max(DILATIONS) // d

        def rows(start, size, d=d):
            return pl.ds(start, size) if d == 1 else pl.ds(start, size, stride=d)

        def block(j, carry, b=b, d=d, nbr=nbr, rows=rows):
            r = j // nbr
            m = j % nbr
            qs = r + d * QBLK * m
            ks = TQ + qs - d * QBLK
            q = q_ref[rows(qs, QBLK), :] * scale
            kc = kbuf[rows(ks, 2 * QBLK), :].astype(BF16)
            vc = vbuf[rows(ks, 2 * QBLK), :]
            pen = jnp.where(prev_half & (m == 0) & (t == 0), NEG_INF, 0.0)
            res, mx = [], []
            for hh in range(2):
                sel = left if hh == 0 else jnp.logical_not(left)
                qh = jnp.where(sel, q, 0.0).astype(BF16)
                s = lax.dot_general(qh, kc, (((1,), (1,)), ((), ())), preferred_element_type=F32)
                s = s + bias[b, hh] + pen
                mh = jnp.max(s, axis=-1, keepdims=True)
                p = jnp.exp(s - mh).astype(BF16)
                vh = jnp.where(sel, vc, 1.0).astype(BF16)
                res.append(jnp.dot(p, vh, preferred_element_type=F32))
                mx.append(mh)
            nacc[b, rows(qs, QBLK), :] = jnp.where(left, res[0], res[1])
            dacc[b, rows(qs, QBLK), :] = pltpu.roll(jnp.where(left, res[1], res[0]), HEAD_DIM, axis=1)
            macc[b, rows(qs, QBLK), :] = jnp.where(left, mx[0], mx[1])
            return carry

        lax.fori_loop(0, max(DILATIONS), block, 0, unroll=BLOCK_UNROLL)

    def merge(c, carry):
        sl = pl.ds(pl.multiple_of(c * QBLK, QBLK), QBLK)
        ms = [macc[b, sl, :] for b in range(len(DILATIONS))]
        top = jnp.maximum(jnp.maximum(ms[0], ms[1]), ms[2])
        ws = [jnp.exp(mb - top) for mb in ms]
        num = sum(w * nacc[b, sl, :] for b, w in enumerate(ws))
        den = sum(w * dacc[b, sl, :] for b, w in enumerate(ws))
        o_ref[sl, :] = num / den
        return carry

    lax.fori_loop(0, TQ // QBLK, merge, 0)


def _attn_prompt(qkv, rel_bias):
    s = qkv.shape[0]
    assert s % TQ == 0
    nb = len(DILATIONS)
    hp_n = ATT_W // LANES
    tile = lambda f: pl.BlockSpec((TQ, LANES), f)
    return pl.pallas_call(
        _attn_prompt_kernel,
        grid=(hp_n, s // TQ),
        in_specs=[pl.BlockSpec(memory_space=pltpu.SMEM),
                  tile(lambda h, t: (t, h)),
                  tile(lambda h, t: (jnp.maximum(t - 1, 0), hp_n + h)),
                  tile(lambda h, t: (t, hp_n + h)),
                  tile(lambda h, t: (jnp.maximum(t - 1, 0), 2 * hp_n + h)),
                  tile(lambda h, t: (t, 2 * hp_n + h)),
                  _const_spec((nb, QBLK, 2 * QBLK))],
        out_specs=tile(lambda h, t: (t, h)),
        out_shape=jax.ShapeDtypeStruct((s, ATT_W), F32),
        scratch_shapes=[pltpu.VMEM((2 * TQ, LANES), F32), pltpu.VMEM((2 * TQ, LANES), F32),
                        pltpu.VMEM((nb, 2, QBLK, 2 * QBLK), F32),
                        pltpu.VMEM((nb, TQ, LANES), F32), pltpu.VMEM((nb, TQ, LANES), F32),
                        pltpu.VMEM((nb, TQ, LANES), F32)],
        compiler_params=pltpu.CompilerParams(dimension_semantics=("parallel", "arbitrary"),
                                             vmem_limit_bytes=VMEM_LIMIT),
        name="attn_prompt",
    )(rel_bias, qkv, qkv, qkv, qkv, qkv, _band_buckets())


HPG = SSD_HEADS // N_GROUPS
GROUP_W = SSD_W // N_GROUPS
GROUP_WIN = 3 * LANES
GROUP_LO = tuple((g * GROUP_W) // LANES * LANES for g in range(N_GROUPS))
CARRY_ROW = 8


def _split(x, terms):
    parts = []
    for _ in range(terms):
        p = x.astype(BF16)
        parts.append(p)
        x = x - p.astype(F32)
    return parts


def _dot_exact_rhs(x, m, terms=3):
    return sum(jnp.dot(p, m, preferred_element_type=F32) for p in _split(x, terms))


def _dot_exact_lhs(m, x, terms=3):
    return sum(jnp.dot(m, p, preferred_element_type=F32) for p in _split(x, terms))


def _softplus(x):
    return jnp.maximum(x, 0.0) + jnp.log1p(jnp.exp(-jnp.abs(x)))


def _silu(x):
    return x * jax.nn.sigmoid(x)


def _ssd_consts():
    c = jnp.arange(SSD_W)
    head_of = (c // SSD_HEAD_DIM)[None, :] == jnp.arange(LANES)[:, None]
    group_of = (c // GROUP_W)[None, :] == jnp.arange(LANES)[:, None]
    return head_of.astype(BF16), group_of.astype(BF16), group_of.T.astype(BF16)


def _conv_silu(cbuf, xbc_ref, w_ref, b_ref, rows):
    cbuf[CARRY_ROW:CARRY_ROW + rows, :] = xbc_ref[...]
    acc = b_ref[...]
    for j in range(CONV_W):
        acc = acc + cbuf[pl.ds(CARRY_ROW - (CONV_W - 1) + j, rows), :] * w_ref[j:j + 1, :]
    return _silu(acc)


def _ssd_ydiag(xs, bm, cm, acs, dt, mask):
    cbs = [lax.dot_general(cm[:, g * D_STATE:(g + 1) * D_STATE], bm[:, g * D_STATE:(g + 1) * D_STATE],
                           (((1,), (1,)), ((), ())), preferred_element_type=F32) for g in range(N_GROUPS)]
    acs_t = acs.T
    dt_t = dt.T
    left = lax.broadcasted_iota(jnp.int32, (1, LANES), 1) < SSD_HEAD_DIM
    pairs = []
    for pair in range(SSD_HEADS // 2):
        xpair = xs[:, pair * LANES:(pair + 1) * LANES]
        acc = None
        for hh in range(2):
            h = 2 * pair + hh
            seg = acs[:, h:h + 1] - acs_t[h:h + 1, :]
            dec = jnp.exp(jnp.where(mask, seg, NEG_INF))
            w = (cbs[h // HPG] * dec * dt_t[h:h + 1, :]).astype(BF16)
            xh = jnp.where(left if hh == 0 else jnp.logical_not(left), xpair, 0.0).astype(BF16)
            part = jnp.dot(w, xh, preferred_element_type=F32)
            acc = part if acc is None else acc + part
        pairs.append(acc)
    return jnp.concatenate(pairs, axis=1)


def _ssd_gate_norm(y, z, gred, gexp, gout):
    y = y * _silu(z)
    ms = _dot_exact_rhs(y * y, gred, terms=2) * (1.0 / GROUP_W)
    return y * _dot_exact_rhs(lax.rsqrt(ms + EPS), gexp) * gout


def _ssd_prompt_kernel(z_ref, xbc_ref, dt_ref, cw_ref, cb_ref, dtb_ref, alog_ref, dsk_ref, gout_ref,
                       hexp_ref, gexp_ref, gred_ref, y_ref, st_ref, cbuf, state, yoff):
    c = pl.program_id(0)
    cl = SSD_CHUNK

    @pl.when(c == 0)
    def _init():
        state[...] = jnp.zeros_like(state)
        cbuf[0:CARRY_ROW, :] = jnp.zeros((CARRY_ROW, CONV_DIM), F32)

    xc = _conv_silu(cbuf, xbc_ref, cw_ref, cb_ref, cl)
    cbuf[CARRY_ROW - (CONV_W - 1):CARRY_ROW, :] = xbc_ref[cl - (CONV_W - 1):cl, :]
    xs = xc[:, :SSD_W]
    bm = xc[:, SSD_W:SSD_W + N_GROUPS * D_STATE].astype(BF16)
    cm = xc[:, SSD_W + N_GROUPS * D_STATE:].astype(BF16)

    li = lax.broadcasted_iota(jnp.int32, (cl, cl), 0)
    si = lax.broadcasted_iota(jnp.int32, (cl, cl), 1)
    causal = li >= si

    dt = _softplus(dt_ref[...] + dtb_ref[...])
    acs = _dot_exact_lhs(causal.astype(BF16), dt * (-jnp.exp(alog_ref[...])))
    hexp = hexp_ref[...]
    e_in = _dot_exact_rhs(jnp.exp(acs), hexp)
    e_end = _dot_exact_rhs(jnp.exp(acs[cl - 1:cl, :] - acs) * dt, hexp)
    xw = (xs * e_end).astype(BF16)

    gmask = lax.broadcasted_iota(jnp.int32, (1, SSD_W), 1) // GROUP_W
    for g in range(N_GROUPS):
        cg = cm[:, g * D_STATE:(g + 1) * D_STATE]
        lo = GROUP_LO[g]
        win = slice(lo, lo + GROUP_WIN)
        inside = gmask[:, win] == g
        yo = jnp.dot(cg, state[:, win].astype(BF16), preferred_element_type=F32)
        yoff[:, win] = jnp.where(inside, yo, 0.0 if g == 0 else yoff[:, win])
    for g in range(N_GROUPS):
        bg = bm[:, g * D_STATE:(g + 1) * D_STATE]
        lo = GROUP_LO[g]
        win = slice(lo, lo + GROUP_WIN)
        inside = gmask[:, win] == g
        upd = lax.dot_general(bg, xw[:, win], (((0,), (0,)), ((), ())), preferred_element_type=F32)
        old = state[:, win]
        state[:, win] = jnp.where(inside, old * e_in[cl - 1:cl, win] + upd, old)

    ydiag = _ssd_ydiag(xs, bm, cm, acs, dt, causal)
    y_ref[...] = _ssd_gate_norm(ydiag + yoff[...] * e_in + xs * dsk_ref[...], z_ref[...],
                                gred_ref[...], gexp_ref[...], gout_ref[...])

    @pl.when(c == pl.num_programs(0) - 1)
    def _emit_state():
        for i in range(SSD_W // LANES):
            st_ref[i * LANES:(i + 1) * LANES, :] = state[:, i * LANES:(i + 1) * LANES].T


def _ssd_prompt(z, xbc, dt, conv_w, conv_b, dt_bias, a_log, dskip_x, g_out):
    s = z.shape[0]
    assert s % SSD_CHUNK == 0
    hexp, gexp, gred = _ssd_consts()
    row = lambda n: pl.BlockSpec((SSD_CHUNK, n), lambda i: (i, 0))
    consts = [conv_w, conv_b, dt_bias, a_log, dskip_x, g_out, hexp, gexp, gred]
    return pl.pallas_call(
        _ssd_prompt_kernel,
        grid=(s // SSD_CHUNK,),
        in_specs=[row(SSD_W), row(CONV_DIM), row(LANES)] + [_const_spec(a.shape) for a in consts],
        out_specs=[row(SSD_W), pl.BlockSpec((SSD_W, D_STATE), lambda i: (0, 0))],
        out_shape=[jax.ShapeDtypeStruct((s, SSD_W), F32), jax.ShapeDtypeStruct((SSD_W, D_STATE), F32)],
        scratch_shapes=[pltpu.VMEM((CARRY_ROW + SSD_CHUNK, CONV_DIM), F32),
                        pltpu.VMEM((D_STATE, SSD_W), F32), pltpu.VMEM((SSD_CHUNK, SSD_W), F32)],
        compiler_params=pltpu.CompilerParams(dimension_semantics=("arbitrary",),
                                             vmem_limit_bytes=VMEM_LIMIT),
        name="ssd_prompt",
    )(z, xbc, dt, *consts)


DEC_SEQ = 4
NEW_ROWS = 8


def _branch_count(dist):
    return sum(((dist % d == 0) & (dist <= QBLK * d)).astype(jnp.int32) for d in DILATIONS)


def _sample_bias_tables(wb):
    t = jnp.arange(NEW_ROWS)[:, None]
    dist = wb + t - jnp.arange(wb)[None, :]
    cnt = jnp.where(t < DEC_SEQ, _branch_count(dist), 0)
    bkt = jnp.where(cnt > 0, _rel_bucket(dist), -1).astype(jnp.int32)
    ladd = jnp.log(jnp.maximum(cnt, 1).astype(F32))
    u = jnp.arange(LANES)[None, :]
    dnew = t - u
    cnew = jnp.where((t < DEC_SEQ) & (dnew > 0), _branch_count(dnew),
                     jnp.where((t < DEC_SEQ) & (dnew == 0), len(DILATIONS), 0))
    bkt_new = jnp.where(cnew > 0, _rel_bucket(dnew), -1).astype(jnp.int32)
    ladd_new = jnp.log(jnp.maximum(cnew, 1).astype(F32))
    return bkt, ladd, bkt_new, ladd_new


def _attn_sample_kernel(tbl_ref, q_ref, kt_ref, vt_ref, bkt_ref, ladd_ref, bktn_ref, laddn_ref,
                        o_ref, bias, bias_new):
    @pl.when(pl.program_id(0) == 0)
    def _build_bias():
        for h in range(ATT_HEADS):
            for src, add, dst in ((bkt_ref, ladd_ref, bias), (bktn_ref, laddn_ref, bias_new)):
                bk = src[...]
                acc = jnp.full(bk.shape, NEG_INF, F32)
                for u in range(N_BUCKETS):
                    acc = jnp.where(bk == u, tbl_ref[u, h], acc)
                dst[h // 2, (h % 2) * NEW_ROWS:(h % 2 + 1) * NEW_ROWS, :] = acc + add[...]

    left = lax.broadcasted_iota(jnp.int32, (1, LANES), 1) < HEAD_DIM
    nt = (((1,), (1,)), ((), ()))
    scale = HEAD_DIM ** -0.5
    outs = []
    for hp in range(ATT_W // LANES):
        cols = slice(hp * LANES, (hp + 1) * LANES)
        q = q_ref[0, :, cols] * scale
        fill = jnp.zeros((LANES - NEW_ROWS, LANES), F32)
        knew = jnp.concatenate([q_ref[0, :, ATT_W + hp * LANES:ATT_W + (hp + 1) * LANES], fill], axis=0)
        vnew = jnp.concatenate([q_ref[0, :, 2 * ATT_W + hp * LANES:2 * ATT_W + (hp + 1) * LANES], fill], axis=0)
        qq = jnp.concatenate([jnp.where(left, q, 0.0), jnp.where(left, 0.0, q)], axis=0).astype(BF16)
        s = jnp.dot(qq, kt_ref[0, cols, :].astype(BF16), preferred_element_type=F32) + bias[hp]
        sn = lax.dot_general(qq, knew.astype(BF16), nt, preferred_element_type=F32) + bias_new[hp]
        m = jnp.maximum(jnp.max(s, axis=-1, keepdims=True), jnp.max(sn, axis=-1, keepdims=True))
        m = jnp.where(m == NEG_INF, 0.0, m)
        p = jnp.exp(s - m)
        pn = jnp.exp(sn - m)
        den = jnp.sum(p, axis=-1, keepdims=True) + jnp.sum(pn, axis=-1, keepdims=True)
        r = lax.dot_general(p.astype(BF16), vt_ref[0, cols, :].astype(BF16), nt, preferred_element_type=F32)
        r = r + jnp.dot(pn.astype(BF16), vnew.astype(BF16), preferred_element_type=F32)
        o = r / jnp.where(den == 0.0, 1.0, den)
        outs.append(jnp.where(left, o[:NEW_ROWS], o[NEW_ROWS:]))
    o_ref[0] = jnp.concatenate(outs, axis=1)[:DEC_SEQ]


def _attn_sample(qkv, cache_kt, cache_vt, rel_bias):
    db, _, wb = cache_kt.shape
    assert wb >= WIN_MAX and qkv.shape[0] == db * DEC_SEQ
    q8 = jnp.pad(qkv.reshape(db, DEC_SEQ, 3 * ATT_W), ((0, 0), (0, NEW_ROWS - DEC_SEQ), (0, 0)))
    consts = _sample_bias_tables(wb)
    hp_n = ATT_W // LANES
    seq = lambda n, m: pl.BlockSpec((1, n, m), lambda b: (b, 0, 0))
    out = pl.pallas_call(
        _attn_sample_kernel,
        grid=(db,),
        in_specs=[pl.BlockSpec(memory_space=pltpu.SMEM), seq(NEW_ROWS, 3 * ATT_W), seq(ATT_W, wb),
                  seq(ATT_W, wb)] + [_const_spec(c.shape) for c in consts],
        out_specs=seq(DEC_SEQ, ATT_W),
        out_shape=jax.ShapeDtypeStruct((db, DEC_SEQ, ATT_W), F32),
        scratch_shapes=[pltpu.VMEM((hp_n, 2 * NEW_ROWS, wb), F32),
                        pltpu.VMEM((hp_n, 2 * NEW_ROWS, LANES), F32)],
        compiler_params=pltpu.CompilerParams(dimension_semantics=("arbitrary",),
                                             vmem_limit_bytes=VMEM_LIMIT),
        name="attn_sample",
    )(rel_bias, q8, cache_kt, cache_vt, *consts)
    return out.reshape(db * DEC_SEQ, ATT_W)


SEQ_ROWS = 8
SEQ_PER_STEP = 8


def _ssd_sample_kernel(z_ref, xp_ref, dt_ref, st_ref, cw_ref, cb_ref, dtb_ref, alog_ref, dsk_ref, gout_ref,
                       hexp_ref, gexp_ref, gred_ref, y_ref, sto_ref, cbuf, yoff):
    rows = SEQ_PER_STEP * SEQ_ROWS
    cbuf[0:CARRY_ROW, :] = jnp.zeros((CARRY_ROW, CONV_DIM), F32)
    xc = _conv_silu(cbuf, xp_ref, cw_ref, cb_ref, rows)
    xs = xc[:, :SSD_W]
    bmf = xc[:, SSD_W:SSD_W + N_GROUPS * D_STATE]
    bm = bmf.astype(BF16)
    cm = xc[:, SSD_W + N_GROUPS * D_STATE:].astype(BF16)

    li = lax.broadcasted_iota(jnp.int32, (rows, rows), 0)
    si = lax.broadcasted_iota(jnp.int32, (rows, rows), 1)
    same = (li // SEQ_ROWS) == (si // SEQ_ROWS)
    mask = same & (li >= si)
    rowi = lax.broadcasted_iota(jnp.int32, (rows, 1), 0)
    is_token = (rowi % SEQ_ROWS) >= SEQ_ROWS - DEC_SEQ

    dt = jnp.where(is_token, _softplus(dt_ref[...] + dtb_ref[...]), 0.0)
    acs = _dot_exact_lhs(mask.astype(BF16), dt * (-jnp.exp(alog_ref[...])))
    acs_end = _dot_exact_lhs(same.astype(BF16), dt * (-jnp.exp(alog_ref[...])))
    hexp = hexp_ref[...]
    e_in = _dot_exact_rhs(jnp.exp(acs), hexp)
    e_end = _dot_exact_rhs(jnp.exp(acs_end - acs) * dt, hexp)
    e_all = _dot_exact_rhs(jnp.exp(acs_end), hexp)
    xw = xs * e_end

    sub = rowi % SEQ_ROWS
    pieces = [p.astype(F32) for p in _split(e_all, 3)]
    dstack = jnp.where(sub == 0, pieces[0], jnp.where(sub == 1, pieces[1], jnp.where(sub == 2, pieces[2], 0.0)))
    ones = jnp.ones((SEQ_ROWS, D_STATE), BF16)
    gmask = lax.broadcasted_iota(jnp.int32, (1, SSD_W), 1) // GROUP_W
    tn = (((0,), (0,)), ((), ()))
    nt = (((1,), (1,)), ((), ()))
    for i in range(SEQ_PER_STEP):
        rs = slice(i * SEQ_ROWS, (i + 1) * SEQ_ROWS)
        for g in range(N_GROUPS):
            lo = GROUP_LO[g]
            win = slice(lo, lo + GROUP_WIN)
            inside = gmask[:, win] == g
            yo = lax.dot_general(cm[rs, g * D_STATE:(g + 1) * D_STATE], st_ref[i, win, :].astype(BF16), nt,
                                 preferred_element_type=F32)
            yoff[rs, win] = jnp.where(inside, yo, 0.0 if g == 0 else yoff[rs, win])
        xstack = jnp.concatenate([jnp.where(gmask == g, xw[rs, :], 0.0) for g in range(N_GROUPS)], axis=0)
        bstack = jnp.concatenate([bmf[rs, g * D_STATE:(g + 1) * D_STATE] for g in range(N_GROUPS)], axis=0)
        upd = lax.dot_general(xstack.astype(BF16), bstack.astype(BF16), tn, preferred_element_type=F32)
        dcol = lax.dot_general(dstack[rs, :].astype(BF16), ones, tn, preferred_element_type=F32)
        sto_ref[i] = st_ref[i] * dcol + upd

    ydiag = _ssd_ydiag(xs, bm, cm, acs, dt, mask)
    y_ref[...] = _ssd_gate_norm(ydiag + yoff[...] * e_in + xs * dsk_ref[...], z_ref[...],
                                gred_ref[...], gexp_ref[...], gout_ref[...])


def _ssd_sample(z, xbc, dt, conv_state, ssm_state, conv_w, conv_b, dt_bias, a_log, dskip_x, g_out):
    db = conv_state.shape[0]
    assert db % SEQ_PER_STEP == 0 and z.shape[0] == db * DEC_SEQ
    lead = SEQ_ROWS - DEC_SEQ

    def padded(a):
        a = a.reshape(db, DEC_SEQ, a.shape[-1])
        return jnp.pad(a, ((0, 0), (lead, 0), (0, 0))).reshape(db * SEQ_ROWS, a.shape[-1])

    xp = jnp.concatenate([jnp.zeros((db, lead - (CONV_W - 1), CONV_DIM), F32), conv_state,
                          xbc.reshape(db, DEC_SEQ, CONV_DIM)], axis=1).reshape(db * SEQ_ROWS, CONV_DIM)
    hexp, gexp, gred = _ssd_consts()
    rows = SEQ_PER_STEP * SEQ_ROWS
    row = lambda n: pl.BlockSpec((rows, n), lambda i: (i, 0))
    st_spec = pl.BlockSpec((SEQ_PER_STEP, SSD_W, D_STATE), lambda i: (i, 0, 0))
    consts = [conv_w, conv_b, dt_bias, a_log, dskip_x, g_out, hexp, gexp, gred]
    y, st = pl.pallas_call(
        _ssd_sample_kernel,
        grid=(db // SEQ_PER_STEP,),
        in_specs=[row(SSD_W), row(CONV_DIM), row(LANES), st_spec] + [_const_spec(a.shape) for a in consts],
        out_specs=[row(SSD_W), st_spec],
        out_shape=[jax.ShapeDtypeStruct((db * SEQ_ROWS, SSD_W), F32),
                   jax.ShapeDtypeStruct((db, SSD_W, D_STATE), F32)],
        scratch_shapes=[pltpu.VMEM((CARRY_ROW + rows, CONV_DIM), F32), pltpu.VMEM((rows, SSD_W), F32)],
        compiler_params=pltpu.CompilerParams(dimension_semantics=("parallel",),
                                             vmem_limit_bytes=VMEM_LIMIT),
        name="ssd_sample",
    )(padded(z), xp, padded(dt), ssm_state, *consts)
    y = y.reshape(db, SEQ_ROWS, SSD_W)[:, lead:].reshape(db * DEC_SEQ, SSD_W)
    return y, st


def kernel(x_prompt, x_sample, cache_k, cache_v, state_conv, state_ssm, rel_bias, g_mix, w_in, conv_w, conv_b,
           dt_bias, a_log, d_skip, g_attn_out, g_ssd_out, w_out, g_ffn, w_gate, w_up, w_down, g_final):
    depth = w_in.shape[0]
    bp, sp, _ = x_prompt.shape
    db, ds, _ = x_sample.shape
    assert bp == 1 and ds == DEC_SEQ and sp >= WIN_MAX
    xp = x_prompt.reshape(bp * sp, D_MODEL)
    xs = x_sample.reshape(db * ds, D_MODEL)
    pad_heads = lambda v: jnp.pad(v, (0, LANES - SSD_HEADS))[None]
    o_z, o_xbc, o_dt = 3 * ATT_W, 3 * ATT_W + SSD_W, 3 * ATT_W + SSD_W + CONV_DIM
    outs = [[] for _ in range(8)]
    for l in range(depth):
        wi = w_in[l]
        wqkv, wz, wxbc = (wi[:, :o_z].astype(BF16), wi[:, o_z:o_xbc].astype(BF16), wi[:, o_xbc:o_dt].astype(BF16))
        wdt = jnp.pad(wi[:, o_dt:], ((0, 0), (0, LANES - SSD_HEADS))).astype(BF16)
        ssd_prm = (conv_w[l], conv_b[l][None], pad_heads(dt_bias[l]), pad_heads(a_log[l]),
                   jnp.repeat(d_skip[l], SSD_HEAD_DIM)[None], g_ssd_out[l][None])
        tail_prm = (g_attn_out[l][None], w_out[l][:ATT_W].astype(BF16), w_out[l][ATT_W:].astype(BF16),
                    g_ffn[l][None], w_gate[l].astype(BF16), w_up[l].astype(BF16), w_down[l].astype(BF16),
                    g_final[None])
        last = l == depth - 1

        qkv_p, z_p, xbc_p, dt_p = _inproj(xp, g_mix[l][None], wqkv, wz, wxbc, wdt)
        att_p = _attn_prompt(qkv_p, rel_bias)
        ssd_p, st_p = _ssd_prompt(z_p, xbc_p, dt_p, *ssd_prm)
        xp = _tail(xp, att_p, ssd_p, *tail_prm, final_norm=last)

        qkv_s, z_s, xbc_s, dt_s = _inproj(xs, g_mix[l][None], wqkv, wz, wxbc, wdt)
        feature_major = lambda c: jnp.transpose(c, (0, 2, 3, 1)).reshape(db, ATT_W, c.shape[1])
        att_s = _attn_sample(qkv_s, feature_major(cache_k[l]), feature_major(cache_v[l]), rel_bias)
        ssd_s, st_s = _ssd_sample(z_s, xbc_s, dt_s, state_conv[l], state_ssm[l].reshape(db, SSD_W, D_STATE),
                                  *ssd_prm)
        xs = _tail(xs, att_s, ssd_s, *tail_prm, final_norm=last)

        heads = lambda a: a.reshape(a.shape[0], a.shape[1], ATT_HEADS, HEAD_DIM)
        kv_p = qkv_p.reshape(bp, sp, 3 * ATT_W)[:, sp - WIN_MAX:]
        kv_s = qkv_s.reshape(db, ds, 3 * ATT_W)
        outs[0].append(heads(kv_p[..., ATT_W:2 * ATT_W]))
        outs[1].append(heads(kv_p[..., 2 * ATT_W:]))
        outs[2].append(xbc_p.reshape(bp, sp, CONV_DIM)[:, sp - (CONV_W - 1):])
        outs[3].append(st_p.reshape(bp, SSD_HEADS, SSD_HEAD_DIM, D_STATE))
        outs[4].append(heads(kv_s[..., ATT_W:2 * ATT_W]))
        outs[5].append(heads(kv_s[..., 2 * ATT_W:]))
        outs[6].append(xbc_s.reshape(db, ds, CONV_DIM)[:, ds - (CONV_W - 1):])
        outs[7].append(st_s.reshape(db, SSD_HEADS, SSD_HEAD_DIM, D_STATE))
    return (xp.reshape(bp, sp, D_MODEL), xs.reshape(db, ds, D_MODEL)) + tuple(jnp.stack(o) for o in outs)
```

```python
import functools
import math

import jax
import jax.numpy as jnp
from jax import lax
from jax.experimental import pallas as pl
from jax.experimental.pallas import tpu as pltpu

F32 = jnp.float32
BF16 = jnp.bfloat16

D_MODEL = 1024
ATT_HEADS = 12
HEAD_DIM = 64
ATT_W = ATT_HEADS * HEAD_DIM
DILATIONS = (1, 4, 16)
QBLK = 128
WIN_MAX = 2048
N_BUCKETS = 32
MAX_EXACT = N_BUCKETS // 2
SSD_HEADS = 20
SSD_HEAD_DIM = 64
SSD_W = SSD_HEADS * SSD_HEAD_DIM
N_GROUPS = 4
D_STATE = 128
CONV_W = 4
CONV_DIM = SSD_W + 2 * N_GROUPS * D_STATE
SSD_CHUNK = 128
EPS = 1e-6

LANES = 128
VMEM_LIMIT = 48 * 1024 * 1024


def _rms(x, g):
    return x * lax.rsqrt(jnp.mean(x * x, axis=-1, keepdims=True) + EPS) * g


def _const_spec(shape):
    nd = len(shape)
    return pl.BlockSpec(shape, lambda *_: (0,) * nd, pipeline_mode=pl.Buffered(1))


def _inproj_kernel(x_ref, g_ref, wqkv_ref, wz_ref, wxbc_ref, wdt_ref,
                   qkv_ref, z_ref, xbc_ref, dt_ref):
    h = _rms(x_ref[...], g_ref[...]).astype(BF16)
    qkv_ref[...] = jnp.dot(h, wqkv_ref[...], preferred_element_type=F32)
    z_ref[...] = jnp.dot(h, wz_ref[...], preferred_element_type=F32)
    xbc_ref[...] = jnp.dot(h, wxbc_ref[...], preferred_element_type=F32)
    dt_ref[...] = jnp.dot(h, wdt_ref[...], preferred_element_type=F32)


def _inproj(x, g_mix, wqkv, wz, wxbc, wdt, tm=256):
    t = x.shape[0]
    assert t % tm == 0
    row = lambda n: pl.BlockSpec((tm, n), lambda i: (i, 0))
    return pl.pallas_call(
        _inproj_kernel,
        grid=(t // tm,),
        in_specs=[row(D_MODEL), _const_spec((1, D_MODEL)), _const_spec(wqkv.shape),
                  _const_spec(wz.shape), _const_spec(wxbc.shape), _const_spec(wdt.shape)],
        out_specs=[row(3 * ATT_W), row(SSD_W), row(CONV_DIM), row(LANES)],
        out_shape=[jax.ShapeDtypeStruct((t, 3 * ATT_W), F32), jax.ShapeDtypeStruct((t, SSD_W), F32),
                   jax.ShapeDtypeStruct((t, CONV_DIM), F32), jax.ShapeDtypeStruct((t, LANES), F32)],
        compiler_params=pltpu.CompilerParams(dimension_semantics=("parallel",),
                                             vmem_limit_bytes=VMEM_LIMIT),
        name="inproj",
    )(x, g_mix, wqkv, wz, wxbc, wdt)


def _tail_kernel(x_ref, att_ref, ssd_ref, gatt_ref, woa_ref, wos_ref, gffn_ref,
                 wg_ref, wu_ref, wd_ref, gfin_ref, y_ref, *, final_norm):
    an = _rms(att_ref[...], gatt_ref[...]).astype(BF16)
    mix = jnp.dot(an, woa_ref[...], preferred_element_type=F32)
    mix = mix + jnp.dot(ssd_ref[...].astype(BF16), wos_ref[...], preferred_element_type=F32)
    x1 = x_ref[...] + mix
    h2 = _rms(x1, gffn_ref[...]).astype(BF16)
    gate = jnp.dot(h2, wg_ref[...], preferred_element_type=F32)
    up = jnp.dot(h2, wu_ref[...], preferred_element_type=F32)
    act = (gate * jax.nn.sigmoid(gate) * up).astype(BF16)
    x2 = x1 + jnp.dot(act, wd_ref[...], preferred_element_type=F32)
    y_ref[...] = _rms(x2, gfin_ref[...]) if final_norm else x2


def _tail(x, att, ssd, g_att, woa, wos, g_ffn, wg, wu, wd, g_fin, final_norm, tm=256):
    t = x.shape[0]
    assert t % tm == 0
    row = lambda n: pl.BlockSpec((tm, n), lambda i: (i, 0))
    consts = [g_att, woa, wos, g_ffn, wg, wu, wd, g_fin]
    return pl.pallas_call(
        functools.partial(_tail_kernel, final_norm=final_norm),
        grid=(t // tm,),
        in_specs=[row(D_MODEL), row(ATT_W), row(SSD_W)] + [_const_spec(c.shape) for c in consts],
        out_specs=row(D_MODEL),
        out_shape=jax.ShapeDtypeStruct((t, D_MODEL), F32),
        compiler_params=pltpu.CompilerParams(dimension_semantics=("parallel",),
                                             vmem_limit_bytes=VMEM_LIMIT),
        name="tail",
    )(x, att, ssd, *consts)


TQ = QBLK * max(DILATIONS)
NEG_INF = float("-inf")
BLOCK_UNROLL = 8


def _rel_bucket(dist):
    n = jnp.maximum(dist, 0)
    nf = jnp.maximum(n, 1).astype(F32)
    large = MAX_EXACT + (jnp.log(nf / MAX_EXACT) / math.log(WIN_MAX / MAX_EXACT)
                         * (N_BUCKETS - MAX_EXACT)).astype(jnp.int32)
    return jnp.where(n < MAX_EXACT, n, jnp.minimum(large, N_BUCKETS - 1))


def _band_buckets():
    qi = jnp.arange(QBLK)[:, None] + QBLK
    kk = jnp.arange(2 * QBLK)[None, :]
    dist = qi - kk
    valid = (dist >= 0) & (dist <= QBLK)
    return jnp.stack([jnp.where(valid, _rel_bucket(dist * d), -1) for d in DILATIONS]).astype(jnp.int32)


def _attn_prompt_kernel(tbl_ref, q_ref, kp_ref, kc_ref, vp_ref, vc_ref, bkt_ref, o_ref,
                        kbuf, vbuf, bias, macc, nacc, dacc):
    hp = pl.program_id(0)
    t = pl.program_id(1)

    @pl.when(t == 0)
    def _build_bias():
        prev_half = lax.broadcasted_iota(jnp.int32, (1, 2 * QBLK), 1) < QBLK
        for b in range(len(DILATIONS)):
            bk = bkt_ref[b]
            for hh in range(2):
                acc = jnp.full((QBLK, 2 * QBLK), NEG_INF, F32)
                for u in range(N_BUCKETS):
                    acc = jnp.where(bk == u, tbl_ref[u, 2 * hp + hh], acc)
                bias[b, 0, hh * QBLK:(hh + 1) * QBLK] = acc
                bias[b, 1, hh * QBLK:(hh + 1) * QBLK] = jnp.where(prev_half, NEG_INF, acc)

    kbuf[0:TQ] = kp_ref[...]
    kbuf[TQ:2 * TQ] = kc_ref[...]
    vbuf[0:TQ] = vp_ref[...]
    vbuf[TQ:2 * TQ] = vc_ref[...]

    left = lax.broadcasted_iota(jnp.int32, (1, LANES), 1) < HEAD_DIM
    scale = jnp.asarray(HEAD_DIM ** -0.5, BF16)
    nt = (((1,), (1,)), ((), ()))

    for b, d in enumerate(DILATIONS):
        nbr = max(DILATIONS) // d

        def rows(start, size, d=d):
            return pl.ds(start, size) if d == 1 else pl.ds(start, size, stride=d)

        def block(j, carry, b=b, d=d, nbr=nbr, rows=rows):
            r = j // nbr
            m = j % nbr
            qs = r + d * QBLK * m
            ks = TQ + qs - d * QBLK
            q = q_ref[rows(qs, QBLK), :].astype(BF16) * scale
            kc = kbuf[rows(ks, 2 * QBLK), :].astype(BF16)
            vc = vbuf[rows(ks, 2 * QBLK), :].astype(BF16)
            first = jnp.logical_and(m == 0, t == 0).astype(jnp.int32)
            zero = jnp.zeros_like(q)
            qq = jnp.concatenate([jnp.where(left, q, zero), jnp.where(left, zero, q)], axis=0)
            s = lax.dot_general(qq, kc, nt, preferred_element_type=F32) + bias[b, first]
            mx = jnp.max(s, axis=-1, keepdims=True)
            e = jnp.exp(s - mx)
            den = jnp.sum(e, axis=-1, keepdims=True)
            res = jnp.dot(e.astype(BF16), vc, preferred_element_type=F32)
            nacc[b, rows(qs, QBLK), :] = jnp.where(left, res[:QBLK], res[QBLK:])
            dacc[b, rows(qs, QBLK), :] = jnp.where(left, den[:QBLK], den[QBLK:])
            macc[b, rows(qs, QBLK), :] = jnp.where(left, mx[:QBLK], mx[QBLK:])
            return carry

        lax.fori_loop(0, max(DILATIONS), block, 0, unroll=BLOCK_UNROLL)

    def merge(c, carry):
        sl = pl.ds(pl.multiple_of(c * QBLK, QBLK), QBLK)
        ms = [macc[b, sl, :] for b in range(len(DILATIONS))]
        top = jnp.maximum(jnp.maximum(ms[0], ms[1]), ms[2])
        ws = [jnp.exp(mb - top) for mb in ms]
        num = sum(w * nacc[b, sl, :] for b, w in enumerate(ws))
        den = sum(w * dacc[b, sl, :] for b, w in enumerate(ws))
        o_ref[sl, :] = num / den
        return carry

    lax.fori_loop(0, TQ // QBLK, merge, 0)


def _attn_prompt(qkv, rel_bias):
    s = qkv.shape[0]
    assert s % TQ == 0
    nb = len(DILATIONS)
    hp_n = ATT_W // LANES
    tile = lambda f: pl.BlockSpec((TQ, LANES), f)
    return pl.pallas_call(
        _attn_prompt_kernel,
        grid=(hp_n, s // TQ),
        in_specs=[pl.BlockSpec(memory_space=pltpu.SMEM),
                  tile(lambda h, t: (t, h)),
                  tile(lambda h, t: (jnp.maximum(t - 1, 0), hp_n + h)),
                  tile(lambda h, t: (t, hp_n + h)),
                  tile(lambda h, t: (jnp.maximum(t - 1, 0), 2 * hp_n + h)),
                  tile(lambda h, t: (t, 2 * hp_n + h)),
                  _const_spec((nb, QBLK, 2 * QBLK))],
        out_specs=tile(lambda h, t: (t, h)),
        out_shape=jax.ShapeDtypeStruct((s, ATT_W), F32),
        scratch_shapes=[pltpu.VMEM((2 * TQ, LANES), F32), pltpu.VMEM((2 * TQ, LANES), F32),
                        pltpu.VMEM((nb, 2, 2 * QBLK, 2 * QBLK), F32),
                        pltpu.VMEM((nb, TQ, LANES), F32), pltpu.VMEM((nb, TQ, LANES), F32),
                        pltpu.VMEM((nb, TQ, LANES), F32)],
        compiler_params=pltpu.CompilerParams(dimension_semantics=("parallel", "arbitrary"),
                                             vmem_limit_bytes=VMEM_LIMIT),
        name="attn_prompt",
    )(rel_bias, qkv, qkv, qkv, qkv, qkv, _band_buckets())


HPG = SSD_HEADS // N_GROUPS
GROUP_W = SSD_W // N_GROUPS
GROUP_WIN = 3 * LANES
GROUP_LO = tuple((g * GROUP_W) // LANES * LANES for g in range(N_GROUPS))
CARRY_ROW = 8


def _split(x, terms):
    parts = []
    for _ in range(terms):
        p = x.astype(BF16)
        parts.append(p)
        x = x - p.astype(F32)
    return parts


def _dot_exact_rhs(xs_terms, m):
    pieces = [p for x, terms in xs_terms for p in _split(x, terms)]
    out = jnp.dot(jnp.concatenate(pieces, axis=0) if len(pieces) > 1 else pieces[0], m,
                  preferred_element_type=F32)
    res, r = [], 0
    for x, terms in xs_terms:
        n = x.shape[0]
        res.append(sum(out[r + i * n:r + (i + 1) * n] for i in range(terms)))
        r += terms * n
    return res


def _dot_exact_lhs(m, x, terms=3):
    n = x.shape[1]
    out = jnp.dot(m, jnp.concatenate(_split(x, terms), axis=1), preferred_element_type=F32)
    return sum(out[:, i * n:(i + 1) * n] for i in range(terms))


def _softplus(x):
    return jnp.maximum(x, 0.0) + jnp.log1p(jnp.exp(-jnp.abs(x)))


def _silu(x):
    return x * jax.nn.sigmoid(x)


def _ssd_consts():
    c = jnp.arange(SSD_W)
    head_of = (c // SSD_HEAD_DIM)[None, :] == jnp.arange(LANES)[:, None]
    group_of = (c // GROUP_W)[None, :] == jnp.arange(LANES)[:, None]
    return head_of.astype(BF16), group_of.astype(BF16), group_of.T.astype(BF16)


def _conv_silu(cbuf, xbc_ref, w_ref, b_ref, rows):
    out = []
    for c in range(CONV_DIM // LANES):
        cols = slice(c * LANES, (c + 1) * LANES)
        cbuf[c, CARRY_ROW:CARRY_ROW + rows, :] = xbc_ref[:, cols]
        acc = b_ref[:, cols]
        for j in range(CONV_W):
            acc = acc + cbuf[c, pl.ds(CARRY_ROW - (CONV_W - 1) + j, rows), :] * w_ref[j:j + 1, cols]
        out.append(_silu(acc))
    return out


def _ssd_ydiag(xs, bm, cm, acs, dt, mask):
    cbs = [lax.dot_general(cm[g], bm[g], (((1,), (1,)), ((), ())), preferred_element_type=F32)
           for g in range(N_GROUPS)]
    acs_t = acs.T
    dt_t = dt.T
    left = lax.broadcasted_iota(jnp.int32, (1, LANES), 1) < SSD_HEAD_DIM
    pairs = []
    for pair in range(SSD_HEADS // 2):
        xpair = xs[:, pair * LANES:(pair + 1) * LANES]
        acc = None
        for hh in range(2):
            h = 2 * pair + hh
            seg = acs[:, h:h + 1] - acs_t[h:h + 1, :]
            dec = jnp.exp(jnp.where(mask, seg, NEG_INF))
            w = (cbs[h // HPG] * dec * dt_t[h:h + 1, :]).astype(BF16)
            xh = jnp.where(left if hh == 0 else jnp.logical_not(left), xpair, 0.0).astype(BF16)
            part = jnp.dot(w, xh, preferred_element_type=F32)
            acc = part if acc is None else acc + part
        pairs.append(acc)
    return jnp.concatenate(pairs, axis=1)


def _ssd_gate_norm(y, z, gred, gexp, gout):
    y = y * _silu(z)
    ms = _dot_exact_rhs([(y * y, 2)], gred)[0] * (1.0 / GROUP_W)
    return y * _dot_exact_rhs([(lax.rsqrt(ms + EPS), 2)], gexp)[0] * gout


def _ssd_prompt_kernel(z_ref, xbc_ref, dt_ref, cw_ref, cb_ref, dtb_ref, alog_ref, dsk_ref, gout_ref,
                       hexp_ref, gexp_ref, gred_ref, y_ref, st_ref, cbuf, state, yoff):
    c = pl.program_id(0)
    cl = SSD_CHUNK

    @pl.when(c == 0)
    def _init():
        state[...] = jnp.zeros_like(state)
        cbuf[:, 0:CARRY_ROW, :] = jnp.zeros((CONV_DIM // LANES, CARRY_ROW, LANES), F32)

    xc = _conv_silu(cbuf, xbc_ref, cw_ref, cb_ref, cl)
    cbuf[:, CARRY_ROW - (CONV_W - 1):CARRY_ROW, :] = cbuf[:, CARRY_ROW + cl - (CONV_W - 1):CARRY_ROW + cl, :]
    nx = SSD_W // LANES
    xs = jnp.concatenate(xc[:nx], axis=1)
    bm = [b.astype(BF16) for b in xc[nx:nx + N_GROUPS]]
    cm = [b.astype(BF16) for b in xc[nx + N_GROUPS:]]

    li = lax.broadcasted_iota(jnp.int32, (cl, cl), 0)
    si = lax.broadcasted_iota(jnp.int32, (cl, cl), 1)
    causal = li >= si

    dt = _softplus(dt_ref[...] + dtb_ref[...])
    acs = _dot_exact_lhs(causal.astype(BF16), dt * (-jnp.exp(alog_ref[...])))
    hexp = hexp_ref[...]
    e_in, e_end = _dot_exact_rhs([(jnp.exp(acs), 2), (jnp.exp(acs[cl - 1:cl, :] - acs) * dt, 1)], hexp)
    xw = (xs * e_end).astype(BF16)
    sub = lax.broadcasted_iota(jnp.int32, (CARRY_ROW, 1), 0)
    last = [p.astype(F32) for p in _split(jnp.exp(acs[cl - 1:cl, :]), 3)]
    stack = jnp.where(sub == 0, last[0], jnp.where(sub == 1, last[1], jnp.where(sub == 2, last[2], 0.0)))
    chunk_decay = jnp.sum(jnp.dot(stack.astype(BF16), hexp, preferred_element_type=F32), axis=0, keepdims=True)

    gmask = lax.broadcasted_iota(jnp.int32, (1, SSD_W), 1) // GROUP_W
    for g in range(N_GROUPS):
        lo = GROUP_LO[g]
        win = slice(lo, lo + GROUP_WIN)
        inside = gmask[:, win] == g
        yo = jnp.dot(cm[g], state[:, win].astype(BF16), preferred_element_type=F32)
        yoff[:, win] = jnp.where(inside, yo, 0.0 if g == 0 else yoff[:, win])
    for g in range(N_GROUPS):
        lo = GROUP_LO[g]
        win = slice(lo, lo + GROUP_WIN)
        inside = gmask[:, win] == g
        upd = lax.dot_general(bm[g], xw[:, win], (((0,), (0,)), ((), ())), preferred_element_type=F32)
        old = state[:, win]
        state[:, win] = jnp.where(inside, old * chunk_decay[:, win] + upd, old)

    ydiag = _ssd_ydiag(xs, bm, cm, acs, dt, causal)
    y_ref[...] = _ssd_gate_norm(ydiag + yoff[...] * e_in + xs * dsk_ref[...], z_ref[...],
                                gred_ref[...], gexp_ref[...], gout_ref[...])

    @pl.when(c == pl.num_programs(0) - 1)
    def _emit_state():
        for i in range(SSD_W // LANES):
            st_ref[i * LANES:(i + 1) * LANES, :] = state[:, i * LANES:(i + 1) * LANES].T


def _ssd_prompt(z, xbc, dt, conv_w, conv_b, dt_bias, a_log, dskip_x, g_out):
    s = z.shape[0]
    assert s % SSD_CHUNK == 0
    hexp, gexp, gred = _ssd_consts()
    row = lambda n: pl.BlockSpec((SSD_CHUNK, n), lambda i: (i, 0))
    consts = [conv_w, conv_b, dt_bias, a_log, dskip_x, g_out, hexp, gexp, gred]
    return pl.pallas_call(
        _ssd_prompt_kernel,
        grid=(s // SSD_CHUNK,),
        in_specs=[row(SSD_W), row(CONV_DIM), row(LANES)] + [_const_spec(a.shape) for a in consts],
        out_specs=[row(SSD_W), pl.BlockSpec((SSD_W, D_STATE), lambda i: (0, 0))],
        out_shape=[jax.ShapeDtypeStruct((s, SSD_W), F32), jax.ShapeDtypeStruct((SSD_W, D_STATE), F32)],
        scratch_shapes=[pltpu.VMEM((CONV_DIM // LANES, CARRY_ROW + SSD_CHUNK, LANES), F32),
                        pltpu.VMEM((D_STATE, SSD_W), F32), pltpu.VMEM((SSD_CHUNK, SSD_W), F32)],
        compiler_params=pltpu.CompilerParams(dimension_semantics=("arbitrary",),
                                             vmem_limit_bytes=VMEM_LIMIT),
        name="ssd_prompt",
    )(z, xbc, dt, *consts)


DEC_SEQ = 4
NEW_ROWS = 8


def _branch_count(dist):
    return sum(((dist % d == 0) & (dist <= QBLK * d)).astype(jnp.int32) for d in DILATIONS)


def _sample_bias_tables(wb):
    t = jnp.arange(NEW_ROWS)[:, None]
    dist = wb + t - jnp.arange(wb)[None, :]
    cnt = jnp.where(t < DEC_SEQ, _branch_count(dist), 0)
    bkt = jnp.where(cnt > 0, _rel_bucket(dist), -1).astype(jnp.int32)
    ladd = jnp.log(jnp.maximum(cnt, 1).astype(F32))
    u = jnp.arange(LANES)[None, :]
    dnew = t - u
    cnew = jnp.where((t < DEC_SEQ) & (dnew > 0), _branch_count(dnew),
                     jnp.where((t < DEC_SEQ) & (dnew == 0), len(DILATIONS), 0))
    bkt_new = jnp.where(cnew > 0, _rel_bucket(dnew), -1).astype(jnp.int32)
    ladd_new = jnp.log(jnp.maximum(cnew, 1).astype(F32))
    return bkt, ladd, bkt_new, ladd_new


def _attn_sample_kernel(tbl_ref, q_ref, kt_ref, vt_ref, bkt_ref, ladd_ref, bktn_ref, laddn_ref,
                        o_ref, bias, bias_new):
    @pl.when(pl.program_id(0) == 0)
    def _build_bias():
        for h in range(ATT_HEADS):
            for src, add, dst in ((bkt_ref, ladd_ref, bias), (bktn_ref, laddn_ref, bias_new)):
                bk = src[...]
                acc = jnp.full(bk.shape, NEG_INF, F32)
                for u in range(N_BUCKETS):
                    acc = jnp.where(bk == u, tbl_ref[u, h], acc)
                dst[h // 2, (h % 2) * NEW_ROWS:(h % 2 + 1) * NEW_ROWS, :] = acc + add[...]

    left = lax.broadcasted_iota(jnp.int32, (1, LANES), 1) < HEAD_DIM
    nt = (((1,), (1,)), ((), ()))
    scale = HEAD_DIM ** -0.5
    outs = []
    for hp in range(ATT_W // LANES):
        cols = slice(hp * LANES, (hp + 1) * LANES)
        q = q_ref[0, :, cols] * scale
        fill = jnp.zeros((LANES - NEW_ROWS, LANES), F32)
        knew = jnp.concatenate([q_ref[0, :, ATT_W + hp * LANES:ATT_W + (hp + 1) * LANES], fill], axis=0)
        vnew = jnp.concatenate([q_ref[0, :, 2 * ATT_W + hp * LANES:2 * ATT_W + (hp + 1) * LANES], fill], axis=0)
        qq = jnp.concatenate([jnp.where(left, q, 0.0), jnp.where(left, 0.0, q)], axis=0).astype(BF16)
        s = jnp.dot(qq, kt_ref[0, cols, :].astype(BF16), preferred_element_type=F32) + bias[hp]
        sn = lax.dot_general(qq, knew.astype(BF16), nt, preferred_element_type=F32) + bias_new[hp]
        m = jnp.maximum(jnp.max(s, axis=-1, keepdims=True), jnp.max(sn, axis=-1, keepdims=True))
        m = jnp.where(m == NEG_INF, 0.0, m)
        p = jnp.exp(s - m)
        pn = jnp.exp(sn - m)
        den = jnp.sum(p, axis=-1, keepdims=True) + jnp.sum(pn, axis=-1, keepdims=True)
        r = lax.dot_general(p.astype(BF16), vt_ref[0, cols, :].astype(BF16), nt, preferred_element_type=F32)
        r = r + jnp.dot(pn.astype(BF16), vnew.astype(BF16), preferred_element_type=F32)
        o = r / jnp.where(den == 0.0, 1.0, den)
        outs.append(jnp.where(left, o[:NEW_ROWS], o[NEW_ROWS:]))
    o_ref[0] = jnp.concatenate(outs, axis=1)[:DEC_SEQ]


def _attn_sample(qkv, cache_kt, cache_vt, rel_bias):
    db, _, wb = cache_kt.shape
    assert wb >= WIN_MAX and qkv.shape[0] == db * DEC_SEQ
    q8 = jnp.pad(qkv.reshape(db, DEC_SEQ, 3 * ATT_W), ((0, 0), (0, NEW_ROWS - DEC_SEQ), (0, 0)))
    consts = _sample_bias_tables(wb)
    hp_n = ATT_W // LANES
    seq = lambda n, m: pl.BlockSpec((1, n, m), lambda b: (b, 0, 0))
    out = pl.pallas_call(
        _attn_sample_kernel,
        grid=(db,),
        in_specs=[pl.BlockSpec(memory_space=pltpu.SMEM), seq(NEW_ROWS, 3 * ATT_W), seq(ATT_W, wb),
                  seq(ATT_W, wb)] + [_const_spec(c.shape) for c in consts],
        out_specs=seq(DEC_SEQ, ATT_W),
        out_shape=jax.ShapeDtypeStruct((db, DEC_SEQ, ATT_W), F32),
        scratch_shapes=[pltpu.VMEM((hp_n, 2 * NEW_ROWS, wb), F32),
                        pltpu.VMEM((hp_n, 2 * NEW_ROWS, LANES), F32)],
        compiler_params=pltpu.CompilerParams(dimension_semantics=("arbitrary",),
                                             vmem_limit_bytes=VMEM_LIMIT),
        name="attn_sample",
    )(rel_bias, q8, cache_kt, cache_vt, *consts)
    return out.reshape(db * DEC_SEQ, ATT_W)


SEQ_ROWS = 8
SEQ_PER_STEP = 8


def _ssd_sample_kernel(z_ref, xp_ref, dt_ref, st_ref, cw_ref, cb_ref, dtb_ref, alog_ref, dsk_ref, gout_ref,
                       hexp_ref, gexp_ref, gred_ref, y_ref, sto_ref, cbuf, yoff):
    rows = SEQ_PER_STEP * SEQ_ROWS
    cbuf[:, 0:CARRY_ROW, :] = jnp.zeros((CONV_DIM // LANES, CARRY_ROW, LANES), F32)
    xc = _conv_silu(cbuf, xp_ref, cw_ref, cb_ref, rows)
    nx = SSD_W // LANES
    xs = jnp.concatenate(xc[:nx], axis=1)
    bmf = xc[nx:nx + N_GROUPS]
    bm = [b.astype(BF16) for b in bmf]
    cm = [b.astype(BF16) for b in xc[nx + N_GROUPS:]]

    li = lax.broadcasted_iota(jnp.int32, (rows, rows), 0)
    si = lax.broadcasted_iota(jnp.int32, (rows, rows), 1)
    same = (li // SEQ_ROWS) == (si // SEQ_ROWS)
    mask = same & (li >= si)
    rowi = lax.broadcasted_iota(jnp.int32, (rows, 1), 0)
    is_token = (rowi % SEQ_ROWS) >= SEQ_ROWS - DEC_SEQ

    dt = jnp.where(is_token, _softplus(dt_ref[...] + dtb_ref[...]), 0.0)
    acs = _dot_exact_lhs(mask.astype(BF16), dt * (-jnp.exp(alog_ref[...])))
    acs_end = _dot_exact_lhs(same.astype(BF16), dt * (-jnp.exp(alog_ref[...])))
    hexp = hexp_ref[...]
    e_in, e_end, e_all = _dot_exact_rhs(
        [(jnp.exp(acs), 2), (jnp.exp(acs_end - acs) * dt, 1), (jnp.exp(acs_end), 3)], hexp)
    xw = xs * e_end

    sub = rowi % SEQ_ROWS
    pieces = [p.astype(F32) for p in _split(e_all, 3)]
    dstack = jnp.where(sub == 0, pieces[0], jnp.where(sub == 1, pieces[1], jnp.where(sub == 2, pieces[2], 0.0)))
    ones = jnp.ones((SEQ_ROWS, D_STATE), BF16)
    gmask = lax.broadcasted_iota(jnp.int32, (1, SSD_W), 1) // GROUP_W
    tn = (((0,), (0,)), ((), ()))
    nt = (((1,), (1,)), ((), ()))
    for i in range(SEQ_PER_STEP):
        rs = slice(i * SEQ_ROWS, (i + 1) * SEQ_ROWS)
        for g in range(N_GROUPS):
            lo = GROUP_LO[g]
            win = slice(lo, lo + GROUP_WIN)
            inside = gmask[:, win] == g
            yo = lax.dot_general(cm[g][rs], st_ref[i, win, :].astype(BF16), nt, preferred_element_type=F32)
            yoff[rs, win] = jnp.where(inside, yo, 0.0 if g == 0 else yoff[rs, win])
        xstack = jnp.concatenate([jnp.where(gmask == g, xw[rs, :], 0.0) for g in range(N_GROUPS)], axis=0)
        bstack = jnp.concatenate([bmf[g][rs] for g in range(N_GROUPS)], axis=0)
        upd = lax.dot_general(xstack.astype(BF16), bstack.astype(BF16), tn, preferred_element_type=F32)
        dcol = lax.dot_general(dstack[rs, :].astype(BF16), ones, tn, preferred_element_type=F32)
        sto_ref[i] = st_ref[i] * dcol + upd

    ydiag = _ssd_ydiag(xs, bm, cm, acs, dt, mask)
    y_ref[...] = _ssd_gate_norm(ydiag + yoff[...] * e_in + xs * dsk_ref[...], z_ref[...],
                                gred_ref[...], gexp_ref[...], gout_ref[...])


def _ssd_sample(z, xbc, dt, conv_state, ssm_state, conv_w, conv_b, dt_bias, a_log, dskip_x, g_out):
    db = conv_state.shape[0]
    assert db % SEQ_PER_STEP == 0 and z.shape[0] == db * DEC_SEQ
    lead = SEQ_ROWS - DEC_SEQ

    def padded(a):
        a = a.reshape(db, DEC_SEQ, a.shape[-1])
        return jnp.pad(a, ((0, 0), (lead, 0), (0, 0))).reshape(db * SEQ_ROWS, a.shape[-1])

    xp = jnp.concatenate([jnp.zeros((db, lead - (CONV_W - 1), CONV_DIM), F32), conv_state,
                          xbc.reshape(db, DEC_SEQ, CONV_DIM)], axis=1).reshape(db * SEQ_ROWS, CONV_DIM)
    hexp, gexp, gred = _ssd_consts()
    rows = SEQ_PER_STEP * SEQ_ROWS
    row = lambda n: pl.BlockSpec((rows, n), lambda i: (i, 0))
    st_spec = pl.BlockSpec((SEQ_PER_STEP, SSD_W, D_STATE), lambda i: (i, 0, 0))
    consts = [conv_w, conv_b, dt_bias, a_log, dskip_x, g_out, hexp, gexp, gred]
    y, st = pl.pallas_call(
        _ssd_sample_kernel,
        grid=(db // SEQ_PER_STEP,),
        in_specs=[row(SSD_W), row(CONV_DIM), row(LANES), st_spec] + [_const_spec(a.shape) for a in consts],
        out_specs=[row(SSD_W), st_spec],
        out_shape=[jax.ShapeDtypeStruct((db * SEQ_ROWS, SSD_W), F32),
                   jax.ShapeDtypeStruct((db, SSD_W, D_STATE), F32)],
        scratch_shapes=[pltpu.VMEM((CONV_DIM // LANES, CARRY_ROW + rows, LANES), F32),
                        pltpu.VMEM((rows, SSD_W), F32)],
        compiler_params=pltpu.CompilerParams(dimension_semantics=("parallel",),
                                             vmem_limit_bytes=VMEM_LIMIT),
        name="ssd_sample",
    )(padded(z), xp, padded(dt), ssm_state, *consts)
    y = y.reshape(db, SEQ_ROWS, SSD_W)[:, lead:].reshape(db * DEC_SEQ, SSD_W)
    return y, st


def kernel(x_prompt, x_sample, cache_k, cache_v, state_conv, state_ssm, rel_bias, g_mix, w_in, conv_w, conv_b,
           dt_bias, a_log, d_skip, g_attn_out, g_ssd_out, w_out, g_ffn, w_gate, w_up, w_down, g_final):
    depth = w_in.shape[0]
    bp, sp, _ = x_prompt.shape
    db, ds, _ = x_sample.shape
    assert bp == 1 and ds == DEC_SEQ and sp >= WIN_MAX
    xp = x_prompt.reshape(bp * sp, D_MODEL)
    xs = x_sample.reshape(db * ds, D_MODEL)
    pad_heads = lambda v: jnp.pad(v, (0, LANES - SSD_HEADS))[None]
    o_z, o_xbc, o_dt = 3 * ATT_W, 3 * ATT_W + SSD_W, 3 * ATT_W + SSD_W + CONV_DIM
    outs = [[] for _ in range(8)]
    for l in range(depth):
        wi = w_in[l]
        wqkv, wz, wxbc = (wi[:, :o_z].astype(BF16), wi[:, o_z:o_xbc].astype(BF16), wi[:, o_xbc:o_dt].astype(BF16))
        wdt = jnp.pad(wi[:, o_dt:], ((0, 0), (0, LANES - SSD_HEADS))).astype(BF16)
        ssd_prm = (conv_w[l], conv_b[l][None], pad_heads(dt_bias[l]), pad_heads(a_log[l]),
                   jnp.repeat(d_skip[l], SSD_HEAD_DIM)[None], g_ssd_out[l][None])
        tail_prm = (g_attn_out[l][None], w_out[l][:ATT_W].astype(BF16), w_out[l][ATT_W:].astype(BF16),
                    g_ffn[l][None], w_gate[l].astype(BF16), w_up[l].astype(BF16), w_down[l].astype(BF16),
                    g_final[None])
        last = l == depth - 1

        qkv_p, z_p, xbc_p, dt_p = _inproj(xp, g_mix[l][None], wqkv, wz, wxbc, wdt)
        att_p = _attn_prompt(qkv_p, rel_bias)
        ssd_p, st_p = _ssd_prompt(z_p, xbc_p, dt_p, *ssd_prm)
        xp = _tail(xp, att_p, ssd_p, *tail_prm, final_norm=last)

        qkv_s, z_s, xbc_s, dt_s = _inproj(xs, g_mix[l][None], wqkv, wz, wxbc, wdt)
        feature_major = lambda c: jnp.transpose(c, (0, 2, 3, 1)).reshape(db, ATT_W, c.shape[1])
        att_s = _attn_sample(qkv_s, feature_major(cache_k[l]), feature_major(cache_v[l]), rel_bias)
        ssd_s, st_s = _ssd_sample(z_s, xbc_s, dt_s, state_conv[l], state_ssm[l].reshape(db, SSD_W, D_STATE),
                                  *ssd_prm)
        xs = _tail(xs, att_s, ssd_s, *tail_prm, final_norm=last)

        heads = lambda a: a.reshape(a.shape[0], a.shape[1], ATT_HEADS, HEAD_DIM)
        kv_p = qkv_p.reshape(bp, sp, 3 * ATT_W)[:, sp - WIN_MAX:]
        kv_s = qkv_s.reshape(db, ds, 3 * ATT_W)
        outs[0].append(heads(kv_p[..., ATT_W:2 * ATT_W]))
        outs[1].append(heads(kv_p[..., 2 * ATT_W:]))
        outs[2].append(xbc_p.reshape(bp, sp, CONV_DIM)[:, sp - (CONV_W - 1):])
        outs[3].append(st_p.reshape(bp, SSD_HEADS, SSD_HEAD_DIM, D_STATE))
        outs[4].append(heads(kv_s[..., ATT_W:2 * ATT_W]))
        outs[5].append(heads(kv_s[..., 2 * ATT_W:]))
        outs[6].append(xbc_s.reshape(db, ds, CONV_DIM)[:, ds - (CONV_W - 1):])
        outs[7].append(st_s.reshape(db, SSD_HEADS, SSD_HEAD_DIM, D_STATE))
    return (xp.reshape(bp, sp, D_MODEL), xs.reshape(db, ds, D_MODEL)) + tuple(jnp.stack(o) for o in outs)
```

```python
import functools
import math

import jax
import jax.numpy as jnp
import numpy as np
from jax import lax
from jax.experimental import pallas as pl
from jax.experimental.pallas import tpu as pltpu

F32 = jnp.float32
BF16 = jnp.bfloat16

D_MODEL = 1024
ATT_HEADS = 12
HEAD_DIM = 64
ATT_W = ATT_HEADS * HEAD_DIM
DILATIONS = (1, 4, 16)
QBLK = 128
WIN_MAX = 2048
N_BUCKETS = 32
MAX_EXACT = N_BUCKETS // 2
SSD_HEADS = 20
SSD_HEAD_DIM = 64
SSD_W = SSD_HEADS * SSD_HEAD_DIM
N_GROUPS = 4
D_STATE = 128
CONV_W = 4
CONV_DIM = SSD_W + 2 * N_GROUPS * D_STATE
SSD_CHUNK = 128
EPS = 1e-6

LANES = 128
VMEM_LIMIT = 48 * 1024 * 1024


def _rms(x, g):
    return x * lax.rsqrt(jnp.mean(x * x, axis=-1, keepdims=True) + EPS) * g


def _const_spec(shape):
    nd = len(shape)
    return pl.BlockSpec(shape, lambda *_: (0,) * nd, pipeline_mode=pl.Buffered(1))


def _inproj_kernel(x_ref, g_ref, wqkv_ref, wz_ref, wxbc_ref, wdt_ref,
                   qkv_ref, z_ref, xbc_ref, dt_ref):
    h = _rms(x_ref[...], g_ref[...]).astype(BF16)
    qkv_ref[...] = jnp.dot(h, wqkv_ref[...], preferred_element_type=F32)
    z_ref[...] = jnp.dot(h, wz_ref[...], preferred_element_type=F32)
    xbc_ref[...] = jnp.dot(h, wxbc_ref[...], preferred_element_type=F32)
    dt_ref[...] = jnp.dot(h, wdt_ref[...], preferred_element_type=F32)


def _inproj(x, g_mix, wqkv, wz, wxbc, wdt, tm=256):
    t = x.shape[0]
    assert t % tm == 0
    row = lambda n: pl.BlockSpec((tm, n), lambda i: (i, 0))
    return pl.pallas_call(
        _inproj_kernel,
        grid=(t // tm,),
        in_specs=[row(D_MODEL), _const_spec((1, D_MODEL)), _const_spec(wqkv.shape),
                  _const_spec(wz.shape), _const_spec(wxbc.shape), _const_spec(wdt.shape)],
        out_specs=[row(3 * ATT_W), row(SSD_W), row(CONV_DIM), row(LANES)],
        out_shape=[jax.ShapeDtypeStruct((t, 3 * ATT_W), F32), jax.ShapeDtypeStruct((t, SSD_W), F32),
                   jax.ShapeDtypeStruct((t, CONV_DIM), F32), jax.ShapeDtypeStruct((t, LANES), F32)],
        compiler_params=pltpu.CompilerParams(dimension_semantics=("parallel",),
                                             vmem_limit_bytes=VMEM_LIMIT),
        name="inproj",
    )(x, g_mix, wqkv, wz, wxbc, wdt)


def _tail_kernel(x_ref, att_ref, ssd_ref, gatt_ref, woa_ref, wos_ref, gffn_ref,
                 wg_ref, wu_ref, wd_ref, gfin_ref, y_ref, *, final_norm):
    an = _rms(att_ref[...], gatt_ref[...]).astype(BF16)
    mix = jnp.dot(an, woa_ref[...], preferred_element_type=F32)
    mix = mix + jnp.dot(ssd_ref[...].astype(BF16), wos_ref[...], preferred_element_type=F32)
    x1 = x_ref[...] + mix
    h2 = _rms(x1, gffn_ref[...]).astype(BF16)
    gate = jnp.dot(h2, wg_ref[...], preferred_element_type=F32)
    up = jnp.dot(h2, wu_ref[...], preferred_element_type=F32)
    act = (gate * jax.nn.sigmoid(gate) * up).astype(BF16)
    x2 = x1 + jnp.dot(act, wd_ref[...], preferred_element_type=F32)
    y_ref[...] = _rms(x2, gfin_ref[...]) if final_norm else x2


def _tail(x, att, ssd, g_att, woa, wos, g_ffn, wg, wu, wd, g_fin, final_norm, tm=256):
    t = x.shape[0]
    assert t % tm == 0
    row = lambda n: pl.BlockSpec((tm, n), lambda i: (i, 0))
    consts = [g_att, woa, wos, g_ffn, wg, wu, wd, g_fin]
    return pl.pallas_call(
        functools.partial(_tail_kernel, final_norm=final_norm),
        grid=(t // tm,),
        in_specs=[row(D_MODEL), row(ATT_W), row(SSD_W)] + [_const_spec(c.shape) for c in consts],
        out_specs=row(D_MODEL),
        out_shape=jax.ShapeDtypeStruct((t, D_MODEL), F32),
        compiler_params=pltpu.CompilerParams(dimension_semantics=("parallel",),
                                             vmem_limit_bytes=VMEM_LIMIT),
        name="tail",
    )(x, att, ssd, *consts)


TQ = QBLK * max(DILATIONS)
NEG_INF = float("-inf")
BLOCK_UNROLL = 8


def _rel_bucket(dist):
    n = np.maximum(dist, 0)
    nf = np.maximum(n, 1).astype(np.float32)
    large = MAX_EXACT + (np.log(nf / MAX_EXACT) / math.log(WIN_MAX / MAX_EXACT)
                         * (N_BUCKETS - MAX_EXACT)).astype(np.int32)
    return np.where(n < MAX_EXACT, n, np.minimum(large, N_BUCKETS - 1))


def _band_buckets():
    qi = np.arange(QBLK)[:, None] + QBLK
    kk = np.arange(2 * QBLK)[None, :]
    dist = qi - kk
    valid = (dist >= 0) & (dist <= QBLK)
    return jnp.asarray(np.stack([np.where(valid, _rel_bucket(dist * d), -1) for d in DILATIONS]).astype(np.int32))


def _attn_prompt_kernel(tbl_ref, q_ref, kp_ref, kc_ref, vp_ref, vc_ref, bkt_ref, o_ref,
                        kbuf, vbuf, bias, macc, nacc, dacc):
    hp = pl.program_id(0)
    t = pl.program_id(1)

    @pl.when(t == 0)
    def _build_bias():
        prev_half = lax.broadcasted_iota(jnp.int32, (1, 2 * QBLK), 1) < QBLK
        for b in range(len(DILATIONS)):
            bk = bkt_ref[b]
            for hh in range(2):
                acc = jnp.full((QBLK, 2 * QBLK), NEG_INF, F32)
                for u in range(N_BUCKETS):
                    acc = jnp.where(bk == u, tbl_ref[u, 2 * hp + hh], acc)
                bias[b, 0, hh * QBLK:(hh + 1) * QBLK] = acc
                bias[b, 1, hh * QBLK:(hh + 1) * QBLK] = jnp.where(prev_half, NEG_INF, acc)

    kbuf[0:TQ] = kp_ref[...]
    kbuf[TQ:2 * TQ] = kc_ref[...]
    vbuf[0:TQ] = vp_ref[...]
    vbuf[TQ:2 * TQ] = vc_ref[...]

    left = lax.broadcasted_iota(jnp.int32, (1, LANES), 1) < HEAD_DIM
    scale = jnp.asarray(HEAD_DIM ** -0.5, BF16)
    nt = (((1,), (1,)), ((), ()))

    for b, d in enumerate(DILATIONS):
        nbr = max(DILATIONS) // d

        def rows(start, size, d=d):
            return pl.ds(start, size) if d == 1 else pl.ds(start, size, stride=d)

        def block(j, carry, b=b, d=d, nbr=nbr, rows=rows):
            r = j // nbr
            m = j % nbr
            qs = r + d * QBLK * m
            ks = TQ + qs - d * QBLK
            q = q_ref[rows(qs, QBLK), :].astype(BF16) * scale
            kc = kbuf[rows(ks, 2 * QBLK), :].astype(BF16)
            vc = vbuf[rows(ks, 2 * QBLK), :].astype(BF16)
            first = jnp.logical_and(m == 0, t == 0).astype(jnp.int32)
            zero = jnp.zeros_like(q)
            qq = jnp.concatenate([jnp.where(left, q, zero), jnp.where(left, zero, q)], axis=0)
            s = lax.dot_general(qq, kc, nt, preferred_element_type=F32) + bias[b, first]
            mx = jnp.max(s, axis=-1, keepdims=True)
            e = jnp.exp(s - mx)
            den = jnp.sum(e, axis=-1, keepdims=True)
            res = jnp.dot(e.astype(BF16), vc, preferred_element_type=F32)
            nacc[b, rows(qs, QBLK), :] = jnp.where(left, res[:QBLK], res[QBLK:])
            dacc[b, rows(qs, QBLK), :] = jnp.where(left, den[:QBLK], den[QBLK:])
            macc[b, rows(qs, QBLK), :] = jnp.where(left, mx[:QBLK], mx[QBLK:])
            return carry

        lax.fori_loop(0, max(DILATIONS), block, 0, unroll=BLOCK_UNROLL)

    def merge(c, carry):
        sl = pl.ds(pl.multiple_of(c * QBLK, QBLK), QBLK)
        ms = [macc[b, sl, :] for b in range(len(DILATIONS))]
        top = jnp.maximum(jnp.maximum(ms[0], ms[1]), ms[2])
        ws = [jnp.exp(mb - top) for mb in ms]
        num = sum(w * nacc[b, sl, :] for b, w in enumerate(ws))
        den = sum(w * dacc[b, sl, :] for b, w in enumerate(ws))
        o_ref[sl, :] = num / den
        return carry

    lax.fori_loop(0, TQ // QBLK, merge, 0)


def _attn_prompt(qkv, rel_bias):
    s = qkv.shape[0]
    assert s % TQ == 0
    nb = len(DILATIONS)
    hp_n = ATT_W // LANES
    tile = lambda f: pl.BlockSpec((TQ, LANES), f)
    return pl.pallas_call(
        _attn_prompt_kernel,
        grid=(hp_n, s // TQ),
        in_specs=[pl.BlockSpec(memory_space=pltpu.SMEM),
                  tile(lambda h, t: (t, h)),
                  tile(lambda h, t: (jnp.maximum(t - 1, 0), hp_n + h)),
                  tile(lambda h, t: (t, hp_n + h)),
                  tile(lambda h, t: (jnp.maximum(t - 1, 0), 2 * hp_n + h)),
                  tile(lambda h, t: (t, 2 * hp_n + h)),
                  _const_spec((nb, QBLK, 2 * QBLK))],
        out_specs=tile(lambda h, t: (t, h)),
        out_shape=jax.ShapeDtypeStruct((s, ATT_W), F32),
        scratch_shapes=[pltpu.VMEM((2 * TQ, LANES), F32), pltpu.VMEM((2 * TQ, LANES), F32),
                        pltpu.VMEM((nb, 2, 2 * QBLK, 2 * QBLK), F32),
                        pltpu.VMEM((nb, TQ, LANES), F32), pltpu.VMEM((nb, TQ, LANES), F32),
                        pltpu.VMEM((nb, TQ, LANES), F32)],
        compiler_params=pltpu.CompilerParams(dimension_semantics=("parallel", "arbitrary"),
                                             vmem_limit_bytes=VMEM_LIMIT),
        name="attn_prompt",
    )(rel_bias, qkv, qkv, qkv, qkv, qkv, _band_buckets())


HPG = SSD_HEADS // N_GROUPS
GROUP_W = SSD_W // N_GROUPS
GROUP_WIN = 3 * LANES
GROUP_LO = tuple((g * GROUP_W) // LANES * LANES for g in range(N_GROUPS))
CARRY_ROW = 8


def _split(x, terms):
    parts = []
    for _ in range(terms):
        p = x.astype(BF16)
        parts.append(p)
        x = x - p.astype(F32)
    return parts


def _dot_exact_rhs(xs_terms, m):
    pieces = [p for x, terms in xs_terms for p in _split(x, terms)]
    out = jnp.dot(jnp.concatenate(pieces, axis=0) if len(pieces) > 1 else pieces[0], m,
                  preferred_element_type=F32)
    res, r = [], 0
    for x, terms in xs_terms:
        n = x.shape[0]
        res.append(sum(out[r + i * n:r + (i + 1) * n] for i in range(terms)))
        r += terms * n
    return res


def _dot_exact_lhs(m, x, terms=3):
    n = x.shape[1]
    out = jnp.dot(m, jnp.concatenate(_split(x, terms), axis=1), preferred_element_type=F32)
    return sum(out[:, i * n:(i + 1) * n] for i in range(terms))


def _softplus(x):
    return jnp.maximum(x, 0.0) + jnp.log1p(jnp.exp(-jnp.abs(x)))


def _silu(x):
    return x * jax.nn.sigmoid(x)


def _ssd_consts():
    c = jnp.arange(SSD_W)
    head_of = (c // SSD_HEAD_DIM)[None, :] == jnp.arange(LANES)[:, None]
    group_of = (c // GROUP_W)[None, :] == jnp.arange(LANES)[:, None]
    return head_of.astype(BF16), group_of.astype(BF16), group_of.T.astype(BF16)


def _conv_silu(cbuf, xbc_ref, w_ref, b_ref, rows):
    out = []
    for c in range(CONV_DIM // LANES):
        cols = slice(c * LANES, (c + 1) * LANES)
        cbuf[c, CARRY_ROW:CARRY_ROW + rows, :] = xbc_ref[:, cols]
        acc = b_ref[:, cols]
        for j in range(CONV_W):
            acc = acc + cbuf[c, pl.ds(CARRY_ROW - (CONV_W - 1) + j, rows), :] * w_ref[j:j + 1, cols]
        out.append(_silu(acc))
    return out


def _ssd_ydiag(xs, bm, cm, acs, dt, mask):
    cbs = [lax.dot_general(cm[g], bm[g], (((1,), (1,)), ((), ())), preferred_element_type=F32)
           for g in range(N_GROUPS)]
    acs_t = acs.T
    dt_t = dt.T
    left = lax.broadcasted_iota(jnp.int32, (1, LANES), 1) < SSD_HEAD_DIM
    pairs = []
    for pair in range(SSD_HEADS // 2):
        xpair = xs[:, pair * LANES:(pair + 1) * LANES]
        acc = None
        for hh in range(2):
            h = 2 * pair + hh
            seg = acs[:, h:h + 1] - acs_t[h:h + 1, :]
            dec = jnp.exp(jnp.where(mask, seg, NEG_INF))
            w = (cbs[h // HPG] * dec * dt_t[h:h + 1, :]).astype(BF16)
            xh = jnp.where(left if hh == 0 else jnp.logical_not(left), xpair, 0.0).astype(BF16)
            part = jnp.dot(w, xh, preferred_element_type=F32)
            acc = part if acc is None else acc + part
        pairs.append(acc)
    return jnp.concatenate(pairs, axis=1)


def _ssd_gate_norm(y, z, gred, gexp, gout):
    y = y * _silu(z)
    ms = _dot_exact_rhs([(y * y, 2)], gred)[0] * (1.0 / GROUP_W)
    return y * _dot_exact_rhs([(lax.rsqrt(ms + EPS), 2)], gexp)[0] * gout


def _ssd_prompt_init(cbuf, state):
    @pl.when(pl.program_id(0) == 0)
    def _init():
        state[...] = jnp.zeros_like(state)
        cbuf[:, 0:CARRY_ROW, :] = jnp.zeros((CONV_DIM // LANES, CARRY_ROW, LANES), F32)


def _ssd_prompt_emit(st_ref, state):
    @pl.when(pl.program_id(0) == pl.num_programs(0) - 1)
    def _emit_state():
        for i in range(SSD_W // LANES):
            st_ref[i * LANES:(i + 1) * LANES, :] = state[:, i * LANES:(i + 1) * LANES].T


def _ssd_prompt_chunk(z_ref, xbc_ref, dt_ref, cw_ref, cb_ref, dtb_ref, alog_ref, dsk_ref, gout_ref,
                      hexp_ref, gexp_ref, gred_ref, y_ref, cbuf, state, yoff):
    cl = SSD_CHUNK
    xc = _conv_silu(cbuf, xbc_ref, cw_ref, cb_ref, cl)
    cbuf[:, CARRY_ROW - (CONV_W - 1):CARRY_ROW, :] = cbuf[:, CARRY_ROW + cl - (CONV_W - 1):CARRY_ROW + cl, :]
    nx = SSD_W // LANES
    xs = jnp.concatenate(xc[:nx], axis=1)
    bm = [b.astype(BF16) for b in xc[nx:nx + N_GROUPS]]
    cm = [b.astype(BF16) for b in xc[nx + N_GROUPS:]]

    li = lax.broadcasted_iota(jnp.int32, (cl, cl), 0)
    si = lax.broadcasted_iota(jnp.int32, (cl, cl), 1)
    causal = li >= si

    dt = _softplus(dt_ref[...] + dtb_ref[...])
    acs = _dot_exact_lhs(causal.astype(BF16), dt * (-jnp.exp(alog_ref[...])))
    hexp = hexp_ref[...]
    e_in, e_end = _dot_exact_rhs([(jnp.exp(acs), 2), (jnp.exp(acs[cl - 1:cl, :] - acs) * dt, 1)], hexp)
    xw = (xs * e_end).astype(BF16)
    sub = lax.broadcasted_iota(jnp.int32, (CARRY_ROW, 1), 0)
    last = [p.astype(F32) for p in _split(jnp.exp(acs[cl - 1:cl, :]), 3)]
    stack = jnp.where(sub == 0, last[0], jnp.where(sub == 1, last[1], jnp.where(sub == 2, last[2], 0.0)))
    chunk_decay = jnp.sum(jnp.dot(stack.astype(BF16), hexp, preferred_element_type=F32), axis=0, keepdims=True)

    gmask = lax.broadcasted_iota(jnp.int32, (1, SSD_W), 1) // GROUP_W
    for g in range(N_GROUPS):
        lo = GROUP_LO[g]
        win = slice(lo, lo + GROUP_WIN)
        inside = gmask[:, win] == g
        yo = jnp.dot(cm[g], state[:, win].astype(BF16), preferred_element_type=F32)
        yoff[:, win] = jnp.where(inside, yo, 0.0 if g == 0 else yoff[:, win])
    for g in range(N_GROUPS):
        lo = GROUP_LO[g]
        win = slice(lo, lo + GROUP_WIN)
        inside = gmask[:, win] == g
        upd = lax.dot_general(bm[g], xw[:, win], (((0,), (0,)), ((), ())), preferred_element_type=F32)
        old = state[:, win]
        state[:, win] = jnp.where(inside, old * chunk_decay[:, win] + upd, old)

    ydiag = _ssd_ydiag(xs, bm, cm, acs, dt, causal)
    y_ref[...] = _ssd_gate_norm(ydiag + yoff[...] * e_in + xs * dsk_ref[...], z_ref[...],
                                gred_ref[...], gexp_ref[...], gout_ref[...])


N_SSD_IN = 12


def _ssd_prompt_kernel(*refs):
    ins, (y_ref, st_ref, cbuf, state, yoff) = refs[:N_SSD_IN], refs[N_SSD_IN:]
    _ssd_prompt_init(cbuf, state)
    _ssd_prompt_chunk(*ins, y_ref, cbuf, state, yoff)
    _ssd_prompt_emit(st_ref, state)


def _ssd_prompt_call(z, xbc, dt, conv_w, conv_b, dt_bias, a_log, dskip_x, g_out):
    s = z.shape[0]
    assert s % SSD_CHUNK == 0
    hexp, gexp, gred = _ssd_consts()
    row = lambda n: pl.BlockSpec((SSD_CHUNK, n), lambda i: (i, 0))
    consts = [conv_w, conv_b, dt_bias, a_log, dskip_x, g_out, hexp, gexp, gred]
    return dict(
        steps=s // SSD_CHUNK,
        operands=[z, xbc, dt] + consts,
        in_specs=[row(SSD_W), row(CONV_DIM), row(LANES)] + [_const_spec(a.shape) for a in consts],
        out_specs=[row(SSD_W), pl.BlockSpec((SSD_W, D_STATE), lambda i: (0, 0))],
        out_shape=[jax.ShapeDtypeStruct((s, SSD_W), F32), jax.ShapeDtypeStruct((SSD_W, D_STATE), F32)],
        scratch=[pltpu.VMEM((CONV_DIM // LANES, CARRY_ROW + SSD_CHUNK, LANES), F32),
                 pltpu.VMEM((D_STATE, SSD_W), F32), pltpu.VMEM((SSD_CHUNK, SSD_W), F32)])


def _run(kernel_fn, name, *calls):
    steps = calls[0]["steps"]
    assert all(c["steps"] == steps for c in calls)
    cat = lambda key: [x for c in calls for x in c[key]]
    return pl.pallas_call(
        kernel_fn, grid=(steps,), in_specs=cat("in_specs"), out_specs=cat("out_specs"),
        out_shape=cat("out_shape"), scratch_shapes=cat("scratch"),
        compiler_params=pltpu.CompilerParams(dimension_semantics=("arbitrary",), vmem_limit_bytes=VMEM_LIMIT),
        name=name)(*cat("operands"))


def _ssd_prompt(*args):
    return _run(_ssd_prompt_kernel, "ssd_prompt", _ssd_prompt_call(*args))


DEC_SEQ = 4
NEW_ROWS = 8


def _branch_count(dist):
    return sum(((dist % d == 0) & (dist <= QBLK * d)).astype(np.int32) for d in DILATIONS)


def _sample_bias_tables(wb):
    t = np.arange(NEW_ROWS)[:, None]
    dist = wb + t - np.arange(wb)[None, :]
    cnt = np.where(t < DEC_SEQ, _branch_count(dist), 0)
    bkt = np.where(cnt > 0, _rel_bucket(dist), -1).astype(np.int32)
    ladd = np.log(np.maximum(cnt, 1).astype(np.float32))
    u = np.arange(LANES)[None, :]
    dnew = t - u
    cnew = np.where((t < DEC_SEQ) & (dnew > 0), _branch_count(dnew),
                    np.where((t < DEC_SEQ) & (dnew == 0), len(DILATIONS), 0))
    bkt_new = np.where(cnew > 0, _rel_bucket(dnew), -1).astype(np.int32)
    ladd_new = np.log(np.maximum(cnew, 1).astype(np.float32))
    return tuple(jnp.asarray(a) for a in (bkt, ladd, bkt_new, ladd_new))


def _attn_sample_init(tbl_ref, bkt_ref, ladd_ref, bktn_ref, laddn_ref, bias, bias_new):
    @pl.when(pl.program_id(0) == 0)
    def _build_bias():
        for h in range(ATT_HEADS):
            for src, add, dst in ((bkt_ref, ladd_ref, bias), (bktn_ref, laddn_ref, bias_new)):
                bk = src[...]
                acc = jnp.full(bk.shape, NEG_INF, F32)
                for u in range(N_BUCKETS):
                    acc = jnp.where(bk == u, tbl_ref[u, h], acc)
                dst[h // 2, (h % 2) * NEW_ROWS:(h % 2 + 1) * NEW_ROWS, :] = acc + add[...]


N_ATTN_S_IN = 8


def _attn_sample_kernel(tbl_ref, q_ref, kt_ref, vt_ref, bkt_ref, ladd_ref, bktn_ref, laddn_ref,
                        o_ref, bias, bias_new):
    _attn_sample_init(tbl_ref, bkt_ref, ladd_ref, bktn_ref, laddn_ref, bias, bias_new)
    _attn_sample_seq(q_ref, kt_ref, vt_ref, o_ref, bias, bias_new)


def _attn_sample_seq(q_ref, kt_ref, vt_ref, o_ref, bias, bias_new):
    left = lax.broadcasted_iota(jnp.int32, (1, LANES), 1) < HEAD_DIM
    nt = (((1,), (1,)), ((), ()))
    scale = HEAD_DIM ** -0.5
    outs = []
    for hp in range(ATT_W // LANES):
        cols = slice(hp * LANES, (hp + 1) * LANES)
        q = q_ref[0, :, cols] * scale
        fill = jnp.zeros((LANES - NEW_ROWS, LANES), F32)
        knew = jnp.concatenate([q_ref[0, :, ATT_W + hp * LANES:ATT_W + (hp + 1) * LANES], fill], axis=0)
        vnew = jnp.concatenate([q_ref[0, :, 2 * ATT_W + hp * LANES:2 * ATT_W + (hp + 1) * LANES], fill], axis=0)
        qq = jnp.concatenate([jnp.where(left, q, 0.0), jnp.where(left, 0.0, q)], axis=0).astype(BF16)
        s = jnp.dot(qq, kt_ref[0, cols, :].astype(BF16), preferred_element_type=F32) + bias[hp]
        sn = lax.dot_general(qq, knew.astype(BF16), nt, preferred_element_type=F32) + bias_new[hp]
        m = jnp.maximum(jnp.max(s, axis=-1, keepdims=True), jnp.max(sn, axis=-1, keepdims=True))
        m = jnp.where(m == NEG_INF, 0.0, m)
        p = jnp.exp(s - m)
        pn = jnp.exp(sn - m)
        den = jnp.sum(p, axis=-1, keepdims=True) + jnp.sum(pn, axis=-1, keepdims=True)
        r = lax.dot_general(p.astype(BF16), vt_ref[0, cols, :].astype(BF16), nt, preferred_element_type=F32)
        r = r + jnp.dot(pn.astype(BF16), vnew.astype(BF16), preferred_element_type=F32)
        o = r / jnp.where(den == 0.0, 1.0, den)
        outs.append(jnp.where(left, o[:NEW_ROWS], o[NEW_ROWS:]))
    o_ref[0] = jnp.concatenate(outs, axis=1)[:DEC_SEQ]


def _attn_sample_call(qkv, cache_kt, cache_vt, rel_bias):
    db, _, wb = cache_kt.shape
    assert wb >= WIN_MAX and qkv.shape[0] == db * DEC_SEQ
    q8 = jnp.pad(qkv.reshape(db, DEC_SEQ, 3 * ATT_W), ((0, 0), (0, NEW_ROWS - DEC_SEQ), (0, 0)))
    consts = list(_sample_bias_tables(wb))
    hp_n = ATT_W // LANES
    seq = lambda n, m: pl.BlockSpec((1, n, m), lambda b: (b, 0, 0))
    return dict(
        steps=db,
        operands=[rel_bias, q8, cache_kt, cache_vt] + consts,
        in_specs=[pl.BlockSpec(memory_space=pltpu.SMEM), seq(NEW_ROWS, 3 * ATT_W), seq(ATT_W, wb),
                  seq(ATT_W, wb)] + [_const_spec(c.shape) for c in consts],
        out_specs=[seq(DEC_SEQ, ATT_W)],
        out_shape=[jax.ShapeDtypeStruct((db, DEC_SEQ, ATT_W), F32)],
        scratch=[pltpu.VMEM((hp_n, 2 * NEW_ROWS, wb), F32), pltpu.VMEM((hp_n, 2 * NEW_ROWS, LANES), F32)])


def _attn_sample(qkv, cache_kt, cache_vt, rel_bias):
    out, = _run(_attn_sample_kernel, "attn_sample", _attn_sample_call(qkv, cache_kt, cache_vt, rel_bias))
    return out.reshape(-1, ATT_W)


def _ssd_prompt_attn_sample_kernel(*refs):
    a = N_SSD_IN + N_ATTN_S_IN
    ssd_in = refs[:N_SSD_IN]
    tbl_ref, q_ref, kt_ref, vt_ref, bkt_ref, ladd_ref, bktn_ref, laddn_ref = refs[N_SSD_IN:a]
    y_ref, st_ref, o_ref, cbuf, state, yoff, bias, bias_new = refs[a:]
    _ssd_prompt_init(cbuf, state)
    _attn_sample_init(tbl_ref, bkt_ref, ladd_ref, bktn_ref, laddn_ref, bias, bias_new)
    _ssd_prompt_chunk(*ssd_in, y_ref, cbuf, state, yoff)
    _attn_sample_seq(q_ref, kt_ref, vt_ref, o_ref, bias, bias_new)
    _ssd_prompt_emit(st_ref, state)


def _ssd_prompt_attn_sample(ssd_args, attn_args):
    y, st, att = _run(_ssd_prompt_attn_sample_kernel, "ssd_prompt_attn_sample",
                      _ssd_prompt_call(*ssd_args), _attn_sample_call(*attn_args))
    return y, st, att.reshape(-1, ATT_W)


SEQ_ROWS = 8
SEQ_PER_STEP = 8


def _ssd_sample_kernel(z_ref, xp_ref, dt_ref, st_ref, cw_ref, cb_ref, dtb_ref, alog_ref, dsk_ref, gout_ref,
                       hexp_ref, gexp_ref, gred_ref, y_ref, sto_ref, cbuf, yoff):
    rows = SEQ_PER_STEP * SEQ_ROWS
    cbuf[:, 0:CARRY_ROW, :] = jnp.zeros((CONV_DIM // LANES, CARRY_ROW, LANES), F32)
    xc = _conv_silu(cbuf, xp_ref, cw_ref, cb_ref, rows)
    nx = SSD_W // LANES
    xs = jnp.concatenate(xc[:nx], axis=1)
    bmf = xc[nx:nx + N_GROUPS]
    bm = [b.astype(BF16) for b in bmf]
    cm = [b.astype(BF16) for b in xc[nx + N_GROUPS:]]

    li = lax.broadcasted_iota(jnp.int32, (rows, rows), 0)
    si = lax.broadcasted_iota(jnp.int32, (rows, rows), 1)
    same = (li // SEQ_ROWS) == (si // SEQ_ROWS)
    mask = same & (li >= si)
    rowi = lax.broadcasted_iota(jnp.int32, (rows, 1), 0)
    is_token = (rowi % SEQ_ROWS) >= SEQ_ROWS - DEC_SEQ

    dt = jnp.where(is_token, _softplus(dt_ref[...] + dtb_ref[...]), 0.0)
    acs = _dot_exact_lhs(mask.astype(BF16), dt * (-jnp.exp(alog_ref[...])))
    acs_end = _dot_exact_lhs(same.astype(BF16), dt * (-jnp.exp(alog_ref[...])))
    hexp = hexp_ref[...]
    e_in, e_end, e_all = _dot_exact_rhs(
        [(jnp.exp(acs), 2), (jnp.exp(acs_end - acs) * dt, 1), (jnp.exp(acs_end), 3)], hexp)
    xw = xs * e_end

    sub = rowi % SEQ_ROWS
    pieces = [p.astype(F32) for p in _split(e_all, 3)]
    dstack = jnp.where(sub == 0, pieces[0], jnp.where(sub == 1, pieces[1], jnp.where(sub == 2, pieces[2], 0.0)))
    ones = jnp.ones((SEQ_ROWS, D_STATE), BF16)
    gmask = lax.broadcasted_iota(jnp.int32, (1, SSD_W), 1) // GROUP_W
    tn = (((0,), (0,)), ((), ()))
    nt = (((1,), (1,)), ((), ()))
    for i in range(SEQ_PER_STEP):
        rs = slice(i * SEQ_ROWS, (i + 1) * SEQ_ROWS)
        for g in range(N_GROUPS):
            lo = GROUP_LO[g]
            win = slice(lo, lo + GROUP_WIN)
            inside = gmask[:, win] == g
            yo = lax.dot_general(cm[g][rs], st_ref[i, win, :].astype(BF16), nt, preferred_element_type=F32)
            yoff[rs, win] = jnp.where(inside, yo, 0.0 if g == 0 else yoff[rs, win])
        xstack = jnp.concatenate([jnp.where(gmask == g, xw[rs, :], 0.0) for g in range(N_GROUPS)], axis=0)
        bstack = jnp.concatenate([bmf[g][rs] for g in range(N_GROUPS)], axis=0)
        upd = lax.dot_general(xstack.astype(BF16), bstack.astype(BF16), tn, preferred_element_type=F32)
        dcol = lax.dot_general(dstack[rs, :].astype(BF16), ones, tn, preferred_element_type=F32)
        sto_ref[i] = st_ref[i] * dcol + upd

    ydiag = _ssd_ydiag(xs, bm, cm, acs, dt, mask)
    y_ref[...] = _ssd_gate_norm(ydiag + yoff[...] * e_in + xs * dsk_ref[...], z_ref[...],
                                gred_ref[...], gexp_ref[...], gout_ref[...])


def _ssd_sample(z, xbc, dt, conv_state, ssm_state, conv_w, conv_b, dt_bias, a_log, dskip_x, g_out):
    db = conv_state.shape[0]
    assert db % SEQ_PER_STEP == 0 and z.shape[0] == db * DEC_SEQ
    lead = SEQ_ROWS - DEC_SEQ

    def padded(a):
        a = a.reshape(db, DEC_SEQ, a.shape[-1])
        return jnp.pad(a, ((0, 0), (lead, 0), (0, 0))).reshape(db * SEQ_ROWS, a.shape[-1])

    xp = jnp.concatenate([jnp.zeros((db, lead - (CONV_W - 1), CONV_DIM), F32), conv_state,
                          xbc.reshape(db, DEC_SEQ, CONV_DIM)], axis=1).reshape(db * SEQ_ROWS, CONV_DIM)
    hexp, gexp, gred = _ssd_consts()
    rows = SEQ_PER_STEP * SEQ_ROWS
    row = lambda n: pl.BlockSpec((rows, n), lambda i: (i, 0))
    st_spec = pl.BlockSpec((SEQ_PER_STEP, SSD_W, D_STATE), lambda i: (i, 0, 0))
    consts = [conv_w, conv_b, dt_bias, a_log, dskip_x, g_out, hexp, gexp, gred]
    y, st = pl.pallas_call(
        _ssd_sample_kernel,
        grid=(db // SEQ_PER_STEP,),
        in_specs=[row(SSD_W), row(CONV_DIM), row(LANES), st_spec] + [_const_spec(a.shape) for a in consts],
        out_specs=[row(SSD_W), st_spec],
        out_shape=[jax.ShapeDtypeStruct((db * SEQ_ROWS, SSD_W), F32),
                   jax.ShapeDtypeStruct((db, SSD_W, D_STATE), F32)],
        scratch_shapes=[pltpu.VMEM((CONV_DIM // LANES, CARRY_ROW + rows, LANES), F32),
                        pltpu.VMEM((rows, SSD_W), F32)],
        compiler_params=pltpu.CompilerParams(dimension_semantics=("parallel",),
                                             vmem_limit_bytes=VMEM_LIMIT),
        name="ssd_sample",
    )(padded(z), xp, padded(dt), ssm_state, *consts)
    y = y.reshape(db, SEQ_ROWS, SSD_W)[:, lead:].reshape(db * DEC_SEQ, SSD_W)
    return y, st


def kernel(x_prompt, x_sample, cache_k, cache_v, state_conv, state_ssm, rel_bias, g_mix, w_in, conv_w, conv_b,
           dt_bias, a_log, d_skip, g_attn_out, g_ssd_out, w_out, g_ffn, w_gate, w_up, w_down, g_final):
    depth = w_in.shape[0]
    bp, sp, _ = x_prompt.shape
    db, ds, _ = x_sample.shape
    assert bp == 1 and ds == DEC_SEQ and sp >= WIN_MAX
    xp = x_prompt.reshape(bp * sp, D_MODEL)
    xs = x_sample.reshape(db * ds, D_MODEL)
    pad_heads = lambda v: jnp.pad(v, (0, LANES - SSD_HEADS))[None]
    o_z, o_xbc, o_dt = 3 * ATT_W, 3 * ATT_W + SSD_W, 3 * ATT_W + SSD_W + CONV_DIM
    outs = [[] for _ in range(8)]
    for l in range(depth):
        wi = w_in[l]
        wqkv, wz, wxbc = (wi[:, :o_z].astype(BF16), wi[:, o_z:o_xbc].astype(BF16), wi[:, o_xbc:o_dt].astype(BF16))
        wdt = jnp.pad(wi[:, o_dt:], ((0, 0), (0, LANES - SSD_HEADS))).astype(BF16)
        ssd_prm = (conv_w[l], conv_b[l][None], pad_heads(dt_bias[l]), pad_heads(a_log[l]),
                   jnp.repeat(d_skip[l], SSD_HEAD_DIM)[None], g_ssd_out[l][None])
        tail_prm = (g_attn_out[l][None], w_out[l][:ATT_W].astype(BF16), w_out[l][ATT_W:].astype(BF16),
                    g_ffn[l][None], w_gate[l].astype(BF16), w_up[l].astype(BF16), w_down[l].astype(BF16),
                    g_final[None])
        last = l == depth - 1

        qkv_p, z_p, xbc_p, dt_p = _inproj(xp, g_mix[l][None], wqkv, wz, wxbc, wdt)
        qkv_s, z_s, xbc_s, dt_s = _inproj(xs, g_mix[l][None], wqkv, wz, wxbc, wdt)
        att_p = _attn_prompt(qkv_p, rel_bias)
        feature_major = lambda c: jnp.transpose(c, (0, 2, 3, 1)).reshape(db, ATT_W, c.shape[1])
        ssd_args = (z_p, xbc_p, dt_p, *ssd_prm)
        attn_args = (qkv_s, feature_major(cache_k[l]), feature_major(cache_v[l]), rel_bias)
        if sp // SSD_CHUNK == db:
            ssd_p, st_p, att_s = _ssd_prompt_attn_sample(ssd_args, attn_args)
        else:
            ssd_p, st_p = _ssd_prompt(*ssd_args)
            att_s = _attn_sample(*attn_args)
        xp = _tail(xp, att_p, ssd_p, *tail_prm, final_norm=last)
        ssd_s, st_s = _ssd_sample(z_s, xbc_s, dt_s, state_conv[l], state_ssm[l].reshape(db, SSD_W, D_STATE),
                                  *ssd_prm)
        xs = _tail(xs, att_s, ssd_s, *tail_prm, final_norm=last)

        heads = lambda a: a.reshape(a.shape[0], a.shape[1], ATT_HEADS, HEAD_DIM)
        kv_p = qkv_p.reshape(bp, sp, 3 * ATT_W)[:, sp - WIN_MAX:]
        kv_s = qkv_s.reshape(db, ds, 3 * ATT_W)
        outs[0].append(heads(kv_p[..., ATT_W:2 * ATT_W]))
        outs[1].append(heads(kv_p[..., 2 * ATT_W:]))
        outs[2].append(xbc_p.reshape(bp, sp, CONV_DIM)[:, sp - (CONV_W - 1):])
        outs[3].append(st_p.reshape(bp, SSD_HEADS, SSD_HEAD_DIM, D_STATE))
        outs[4].append(heads(kv_s[..., ATT_W:2 * ATT_W]))
        outs[5].append(heads(kv_s[..., 2 * ATT_W:]))
        outs[6].append(xbc_s.reshape(db, ds, CONV_DIM)[:, ds - (CONV_W - 1):])
        outs[7].append(st_s.reshape(db, SSD_HEADS, SSD_HEAD_DIM, D_STATE))
    return (xp.reshape(bp, sp, D_MODEL), xs.reshape(db, ds, D_MODEL)) + tuple(jnp.stack(o) for o in outs)
```

```python
import functools
import math

import jax
import jax.numpy as jnp
import numpy as np
from jax import lax
from jax.experimental import pallas as pl
from jax.experimental.pallas import tpu as pltpu

F32 = jnp.float32
BF16 = jnp.bfloat16

D_MODEL = 1024
ATT_HEADS = 12
HEAD_DIM = 64
ATT_W = ATT_HEADS * HEAD_DIM
DILATIONS = (1, 4, 16)
QBLK = 128
WIN_MAX = 2048
N_BUCKETS = 32
MAX_EXACT = N_BUCKETS // 2
SSD_HEADS = 20
SSD_HEAD_DIM = 64
SSD_W = SSD_HEADS * SSD_HEAD_DIM
N_GROUPS = 4
D_STATE = 128
CONV_W = 4
CONV_DIM = SSD_W + 2 * N_GROUPS * D_STATE
SSD_CHUNK = 128
EPS = 1e-6

LANES = 128
VMEM_LIMIT = 48 * 1024 * 1024


def _rms(x, g):
    return x * lax.rsqrt(jnp.mean(x * x, axis=-1, keepdims=True) + EPS) * g


def _const_spec(shape):
    nd = len(shape)
    return pl.BlockSpec(shape, lambda *_: (0,) * nd, pipeline_mode=pl.Buffered(1))


def _inproj_kernel(x_ref, g_ref, wqkv_ref, wz_ref, wxbc_ref, wdt_ref,
                   qkv_ref, z_ref, xbc_ref, dt_ref):
    h = _rms(x_ref[...], g_ref[...]).astype(BF16)
    nt = (((1,), (1,)), ((), ()))
    qkv_ref[...] = lax.dot_general(h, wqkv_ref[...], nt, preferred_element_type=F32)
    z_ref[...] = lax.dot_general(h, wz_ref[...], nt, preferred_element_type=F32)
    xbc_ref[...] = lax.dot_general(h, wxbc_ref[...], nt, preferred_element_type=F32)
    dt_ref[...] = lax.dot_general(h, wdt_ref[...], nt, preferred_element_type=F32)


def _inproj(x, g_mix, wqkv, wz, wxbc, wdt, tm=256):
    t = x.shape[0]
    assert t % tm == 0
    row = lambda n: pl.BlockSpec((tm, n), lambda i: (i, 0))
    return pl.pallas_call(
        _inproj_kernel,
        grid=(t // tm,),
        in_specs=[row(D_MODEL), _const_spec((1, D_MODEL)), _const_spec(wqkv.shape),
                  _const_spec(wz.shape), _const_spec(wxbc.shape), _const_spec(wdt.shape)],
        out_specs=[row(3 * ATT_W), row(SSD_W), row(CONV_DIM), row(LANES)],
        out_shape=[jax.ShapeDtypeStruct((t, 3 * ATT_W), F32), jax.ShapeDtypeStruct((t, SSD_W), F32),
                   jax.ShapeDtypeStruct((t, CONV_DIM), F32), jax.ShapeDtypeStruct((t, LANES), F32)],
        compiler_params=pltpu.CompilerParams(dimension_semantics=("parallel",),
                                             vmem_limit_bytes=VMEM_LIMIT),
        name="inproj",
    )(x, g_mix, wqkv, wz, wxbc, wdt)


def _tail_kernel(x_ref, att_ref, ssd_ref, gatt_ref, woa_ref, wos_ref, gffn_ref,
                 wg_ref, wu_ref, wd_ref, gfin_ref, y_ref, *, final_norm):
    an = _rms(att_ref[...], gatt_ref[...]).astype(BF16)
    mix = jnp.dot(an, woa_ref[...], preferred_element_type=F32)
    mix = mix + jnp.dot(ssd_ref[...].astype(BF16), wos_ref[...], preferred_element_type=F32)
    x1 = x_ref[...] + mix
    h2 = _rms(x1, gffn_ref[...]).astype(BF16)
    gate = jnp.dot(h2, wg_ref[...], preferred_element_type=F32)
    up = jnp.dot(h2, wu_ref[...], preferred_element_type=F32)
    act = (gate * jax.nn.sigmoid(gate) * up).astype(BF16)
    x2 = x1 + jnp.dot(act, wd_ref[...], preferred_element_type=F32)
    y_ref[...] = _rms(x2, gfin_ref[...]) if final_norm else x2


def _tail(x, att, ssd, g_att, woa, wos, g_ffn, wg, wu, wd, g_fin, final_norm, tm=256):
    t = x.shape[0]
    assert t % tm == 0
    row = lambda n: pl.BlockSpec((tm, n), lambda i: (i, 0))
    consts = [g_att, woa, wos, g_ffn, wg, wu, wd, g_fin]
    return pl.pallas_call(
        functools.partial(_tail_kernel, final_norm=final_norm),
        grid=(t // tm,),
        in_specs=[row(D_MODEL), row(ATT_W), row(SSD_W)] + [_const_spec(c.shape) for c in consts],
        out_specs=row(D_MODEL),
        out_shape=jax.ShapeDtypeStruct((t, D_MODEL), F32),
        compiler_params=pltpu.CompilerParams(dimension_semantics=("parallel",),
                                             vmem_limit_bytes=VMEM_LIMIT),
        name="tail",
    )(x, att, ssd, *consts)


TQ = QBLK * max(DILATIONS)
NEG_INF = float("-inf")
BLOCK_UNROLL = 16


def _rel_bucket(dist):
    n = np.maximum(dist, 0)
    nf = np.maximum(n, 1).astype(np.float32)
    large = MAX_EXACT + (np.log(nf / MAX_EXACT) / math.log(WIN_MAX / MAX_EXACT)
                         * (N_BUCKETS - MAX_EXACT)).astype(np.int32)
    return np.where(n < MAX_EXACT, n, np.minimum(large, N_BUCKETS - 1))


def _band_buckets():
    qi = np.arange(QBLK)[:, None] + QBLK
    kk = np.arange(2 * QBLK)[None, :]
    dist = qi - kk
    valid = (dist >= 0) & (dist <= QBLK)
    return jnp.asarray(np.stack([np.where(valid, _rel_bucket(dist * d), -1) for d in DILATIONS]).astype(np.int32))


def _attn_prompt_kernel(tbl_ref, q_ref, kp_ref, kc_ref, vp_ref, vc_ref, bkt_ref, o_ref,
                        kbuf, vbuf, bias, macc, nacc, dacc):
    hp = pl.program_id(0)
    t = pl.program_id(1)

    @pl.when(t == 0)
    def _build_bias():
        prev_half = lax.broadcasted_iota(jnp.int32, (1, 2 * QBLK), 1) < QBLK
        for b in range(len(DILATIONS)):
            bk = bkt_ref[b]
            for hh in range(2):
                acc = jnp.full((QBLK, 2 * QBLK), NEG_INF, F32)
                for u in range(N_BUCKETS):
                    acc = jnp.where(bk == u, tbl_ref[u, 2 * hp + hh], acc)
                bias[b, 0, hh * QBLK:(hh + 1) * QBLK] = acc
                bias[b, 1, hh * QBLK:(hh + 1) * QBLK] = jnp.where(prev_half, NEG_INF, acc)

    kbuf[0:TQ] = kp_ref[...]
    kbuf[TQ:2 * TQ] = kc_ref[...]
    vbuf[0:TQ] = vp_ref[...]
    vbuf[TQ:2 * TQ] = vc_ref[...]

    left = lax.broadcasted_iota(jnp.int32, (1, LANES), 1) < HEAD_DIM
    scale = jnp.asarray(HEAD_DIM ** -0.5, BF16)
    nt = (((1,), (1,)), ((), ()))

    for b, d in enumerate(DILATIONS):
        nbr = max(DILATIONS) // d

        def rows(start, size, d=d):
            return pl.ds(start, size) if d == 1 else pl.ds(start, size, stride=d)

        def block(j, carry, b=b, d=d, nbr=nbr, rows=rows):
            r = j // nbr
            m = j % nbr
            qs = r + d * QBLK * m
            ks = TQ + qs - d * QBLK
            q = q_ref[rows(qs, QBLK), :].astype(BF16) * scale
            kc = kbuf[rows(ks, 2 * QBLK), :].astype(BF16)
            vc = vbuf[rows(ks, 2 * QBLK), :].astype(BF16)
            first = jnp.logical_and(m == 0, t == 0).astype(jnp.int32)
            zero = jnp.zeros_like(q)
            qq = jnp.concatenate([jnp.where(left, q, zero), jnp.where(left, zero, q)], axis=0)
            s = lax.dot_general(qq, kc, nt, preferred_element_type=F32) + bias[b, first]
            mx = jnp.max(s, axis=-1, keepdims=True)
            e = jnp.exp(s - mx)
            den = jnp.sum(e, axis=-1, keepdims=True)
            res = jnp.dot(e.astype(BF16), vc, preferred_element_type=F32)
            nacc[b, rows(qs, QBLK), :] = jnp.where(left, res[:QBLK], res[QBLK:])
            dacc[b, rows(qs, QBLK), :] = jnp.where(left, den[:QBLK], den[QBLK:])
            macc[b, rows(qs, QBLK), :] = jnp.where(left, mx[:QBLK], mx[QBLK:])
            return carry

        lax.fori_loop(0, max(DILATIONS), block, 0, unroll=BLOCK_UNROLL)

    def merge(c, carry):
        sl = pl.ds(pl.multiple_of(c * QBLK, QBLK), QBLK)
        ms = [macc[b, sl, :] for b in range(len(DILATIONS))]
        top = jnp.maximum(jnp.maximum(ms[0], ms[1]), ms[2])
        ws = [jnp.exp(mb - top) for mb in ms]
        num = sum(w * nacc[b, sl, :] for b, w in enumerate(ws))
        den = sum(w * dacc[b, sl, :] for b, w in enumerate(ws))
        o_ref[sl, :] = num / den
        return carry

    lax.fori_loop(0, TQ // QBLK, merge, 0)


def _attn_prompt(qkv, rel_bias):
    s = qkv.shape[0]
    assert s % TQ == 0
    nb = len(DILATIONS)
    hp_n = ATT_W // LANES
    tile = lambda f: pl.BlockSpec((TQ, LANES), f)
    return pl.pallas_call(
        _attn_prompt_kernel,
        grid=(hp_n, s // TQ),
        in_specs=[pl.BlockSpec(memory_space=pltpu.SMEM),
                  tile(lambda h, t: (t, h)),
                  tile(lambda h, t: (jnp.maximum(t - 1, 0), hp_n + h)),
                  tile(lambda h, t: (t, hp_n + h)),
                  tile(lambda h, t: (jnp.maximum(t - 1, 0), 2 * hp_n + h)),
                  tile(lambda h, t: (t, 2 * hp_n + h)),
                  _const_spec((nb, QBLK, 2 * QBLK))],
        out_specs=tile(lambda h, t: (t, h)),
        out_shape=jax.ShapeDtypeStruct((s, ATT_W), F32),
        scratch_shapes=[pltpu.VMEM((2 * TQ, LANES), F32), pltpu.VMEM((2 * TQ, LANES), F32),
                        pltpu.VMEM((nb, 2, 2 * QBLK, 2 * QBLK), F32),
                        pltpu.VMEM((nb, TQ, LANES), F32), pltpu.VMEM((nb, TQ, LANES), F32),
                        pltpu.VMEM((nb, TQ, LANES), F32)],
        compiler_params=pltpu.CompilerParams(dimension_semantics=("parallel", "arbitrary"),
                                             vmem_limit_bytes=VMEM_LIMIT),
        name="attn_prompt",
    )(rel_bias, qkv, qkv, qkv, qkv, qkv, _band_buckets())


HPG = SSD_HEADS // N_GROUPS
GROUP_W = SSD_W // N_GROUPS
GROUP_WIN = 3 * LANES
GROUP_LO = tuple((g * GROUP_W) // LANES * LANES for g in range(N_GROUPS))
CARRY_ROW = 8


def _split(x, terms):
    parts = []
    for _ in range(terms):
        p = x.astype(BF16)
        parts.append(p)
        x = x - p.astype(F32)
    return parts


def _dot_exact_rhs(xs_terms, m):
    pieces = [p for x, terms in xs_terms for p in _split(x, terms)]
    out = jnp.dot(jnp.concatenate(pieces, axis=0) if len(pieces) > 1 else pieces[0], m,
                  preferred_element_type=F32)
    res, r = [], 0
    for x, terms in xs_terms:
        n = x.shape[0]
        res.append(sum(out[r + i * n:r + (i + 1) * n] for i in range(terms)))
        r += terms * n
    return res


def _dot_exact_lhs(m, x, terms=3):
    n = x.shape[1]
    out = jnp.dot(m, jnp.concatenate(_split(x, terms), axis=1), preferred_element_type=F32)
    return sum(out[:, i * n:(i + 1) * n] for i in range(terms))


def _softplus(x):
    return jnp.maximum(x, 0.0) + jnp.log1p(jnp.exp(-jnp.abs(x)))


def _silu(x):
    return x * jax.nn.sigmoid(x)


def _ssd_consts():
    c = jnp.arange(SSD_W)
    head_of = (c // SSD_HEAD_DIM)[None, :] == jnp.arange(LANES)[:, None]
    group_of = (c // GROUP_W)[None, :] == jnp.arange(LANES)[:, None]
    return head_of.astype(BF16), group_of.astype(BF16), group_of.T.astype(BF16)


def _conv_silu(cbuf, xbc_ref, w_ref, b_ref, rows):
    out = []
    for c in range(CONV_DIM // LANES):
        cols = slice(c * LANES, (c + 1) * LANES)
        cbuf[c, CARRY_ROW:CARRY_ROW + rows, :] = xbc_ref[:, cols]
        acc = b_ref[:, cols]
        for j in range(CONV_W):
            acc = acc + cbuf[c, pl.ds(CARRY_ROW - (CONV_W - 1) + j, rows), :] * w_ref[j:j + 1, cols]
        out.append(_silu(acc))
    return out


def _ssd_ydiag(xs, bm, cm, acs, dt, mask):
    cbs = [lax.dot_general(cm[g], bm[g], (((1,), (1,)), ((), ())), preferred_element_type=F32)
           for g in range(N_GROUPS)]
    acs_t = acs.T
    dt_t = dt.T
    left = lax.broadcasted_iota(jnp.int32, (1, LANES), 1) < SSD_HEAD_DIM
    pairs = []
    for pair in range(SSD_HEADS // 2):
        xpair = xs[:, pair * LANES:(pair + 1) * LANES]
        acc = None
        for hh in range(2):
            h = 2 * pair + hh
            seg = acs[:, h:h + 1] - acs_t[h:h + 1, :]
            dec = jnp.exp(jnp.where(mask, seg, NEG_INF))
            w = (cbs[h // HPG] * dec * dt_t[h:h + 1, :]).astype(BF16)
            xh = jnp.where(left if hh == 0 else jnp.logical_not(left), xpair, 0.0).astype(BF16)
            part = jnp.dot(w, xh, preferred_element_type=F32)
            acc = part if acc is None else acc + part
        pairs.append(acc)
    return jnp.concatenate(pairs, axis=1)


def _ssd_gate_norm(y, z, gred, gexp, gout):
    y = y * _silu(z)
    ms = _dot_exact_rhs([(y * y, 2)], gred)[0] * (1.0 / GROUP_W)
    return y * _dot_exact_rhs([(lax.rsqrt(ms + EPS), 2)], gexp)[0] * gout


def _ssd_prompt_init(cbuf, state):
    @pl.when(pl.program_id(0) == 0)
    def _init():
        state[...] = jnp.zeros_like(state)
        cbuf[:, 0:CARRY_ROW, :] = jnp.zeros((CONV_DIM // LANES, CARRY_ROW, LANES), F32)


def _ssd_prompt_emit(st_ref, state):
    @pl.when(pl.program_id(0) == pl.num_programs(0) - 1)
    def _emit_state():
        for i in range(SSD_W // LANES):
            st_ref[i * LANES:(i + 1) * LANES, :] = state[:, i * LANES:(i + 1) * LANES].T


def _ssd_prompt_chunk(z_ref, xbc_ref, dt_ref, cw_ref, cb_ref, dtb_ref, alog_ref, dsk_ref, gout_ref,
                      hexp_ref, gexp_ref, gred_ref, y_ref, cbuf, state, yoff):
    cl = SSD_CHUNK
    xc = _conv_silu(cbuf, xbc_ref, cw_ref, cb_ref, cl)
    cbuf[:, CARRY_ROW - (CONV_W - 1):CARRY_ROW, :] = cbuf[:, CARRY_ROW + cl - (CONV_W - 1):CARRY_ROW + cl, :]
    nx = SSD_W // LANES
    xs = jnp.concatenate(xc[:nx], axis=1)
    bm = [b.astype(BF16) for b in xc[nx:nx + N_GROUPS]]
    cm = [b.astype(BF16) for b in xc[nx + N_GROUPS:]]

    li = lax.broadcasted_iota(jnp.int32, (cl, cl), 0)
    si = lax.broadcasted_iota(jnp.int32, (cl, cl), 1)
    causal = li >= si

    dt = _softplus(dt_ref[...] + dtb_ref[...])
    acs = _dot_exact_lhs(causal.astype(BF16), dt * (-jnp.exp(alog_ref[...])))
    hexp = hexp_ref[...]
    e_in, e_end = _dot_exact_rhs([(jnp.exp(acs), 2), (jnp.exp(acs[cl - 1:cl, :] - acs) * dt, 1)], hexp)
    xw = (xs * e_end).astype(BF16)
    sub = lax.broadcasted_iota(jnp.int32, (CARRY_ROW, 1), 0)
    last = [p.astype(F32) for p in _split(jnp.exp(acs[cl - 1:cl, :]), 3)]
    stack = jnp.where(sub == 0, last[0], jnp.where(sub == 1, last[1], jnp.where(sub == 2, last[2], 0.0)))
    chunk_decay = jnp.sum(jnp.dot(stack.astype(BF16), hexp, preferred_element_type=F32), axis=0, keepdims=True)

    gmask = lax.broadcasted_iota(jnp.int32, (1, SSD_W), 1) // GROUP_W
    for g in range(N_GROUPS):
        lo = GROUP_LO[g]
        win = slice(lo, lo + GROUP_WIN)
        inside = gmask[:, win] == g
        yo = jnp.dot(cm[g], state[:, win].astype(BF16), preferred_element_type=F32)
        yoff[:, win] = jnp.where(inside, yo, 0.0 if g == 0 else yoff[:, win])
    for g in range(N_GROUPS):
        lo = GROUP_LO[g]
        win = slice(lo, lo + GROUP_WIN)
        inside = gmask[:, win] == g
        upd = lax.dot_general(bm[g], xw[:, win], (((0,), (0,)), ((), ())), preferred_element_type=F32)
        old = state[:, win]
        state[:, win] = jnp.where(inside, old * chunk_decay[:, win] + upd, old)

    ydiag = _ssd_ydiag(xs, bm, cm, acs, dt, causal)
    y_ref[...] = _ssd_gate_norm(ydiag + yoff[...] * e_in + xs * dsk_ref[...], z_ref[...],
                                gred_ref[...], gexp_ref[...], gout_ref[...])


N_SSD_IN = 12


def _ssd_prompt_kernel(*refs):
    ins, (y_ref, st_ref, cbuf, state, yoff) = refs[:N_SSD_IN], refs[N_SSD_IN:]
    _ssd_prompt_init(cbuf, state)
    _ssd_prompt_chunk(*ins, y_ref, cbuf, state, yoff)
    _ssd_prompt_emit(st_ref, state)


def _ssd_prompt_call(z, xbc, dt, conv_w, conv_b, dt_bias, a_log, dskip_x, g_out):
    s = z.shape[0]
    assert s % SSD_CHUNK == 0
    hexp, gexp, gred = _ssd_consts()
    row = lambda n: pl.BlockSpec((SSD_CHUNK, n), lambda i: (i, 0))
    consts = [conv_w, conv_b, dt_bias, a_log, dskip_x, g_out, hexp, gexp, gred]
    return dict(
        steps=s // SSD_CHUNK,
        operands=[z, xbc, dt] + consts,
        in_specs=[row(SSD_W), row(CONV_DIM), row(LANES)] + [_const_spec(a.shape) for a in consts],
        out_specs=[row(SSD_W), pl.BlockSpec((SSD_W, D_STATE), lambda i: (0, 0))],
        out_shape=[jax.ShapeDtypeStruct((s, SSD_W), F32), jax.ShapeDtypeStruct((SSD_W, D_STATE), F32)],
        scratch=[pltpu.VMEM((CONV_DIM // LANES, CARRY_ROW + SSD_CHUNK, LANES), F32),
                 pltpu.VMEM((D_STATE, SSD_W), F32), pltpu.VMEM((SSD_CHUNK, SSD_W), F32)])


def _run(kernel_fn, name, *calls):
    steps = calls[0]["steps"]
    assert all(c["steps"] == steps for c in calls)
    cat = lambda key: [x for c in calls for x in c[key]]
    return pl.pallas_call(
        kernel_fn, grid=(steps,), in_specs=cat("in_specs"), out_specs=cat("out_specs"),
        out_shape=cat("out_shape"), scratch_shapes=cat("scratch"),
        compiler_params=pltpu.CompilerParams(dimension_semantics=("arbitrary",), vmem_limit_bytes=VMEM_LIMIT),
        name=name)(*cat("operands"))


def _ssd_prompt(*args):
    return _run(_ssd_prompt_kernel, "ssd_prompt", _ssd_prompt_call(*args))


DEC_SEQ = 4
NEW_ROWS = 8


def _branch_count(dist):
    return sum(((dist % d == 0) & (dist <= QBLK * d)).astype(np.int32) for d in DILATIONS)


def _sample_bias_tables(wb):
    t = np.arange(NEW_ROWS)[:, None]
    dist = wb + t - np.arange(wb)[None, :]
    cnt = np.where(t < DEC_SEQ, _branch_count(dist), 0)
    bkt = np.where(cnt > 0, _rel_bucket(dist), -1).astype(np.int32)
    ladd = np.log(np.maximum(cnt, 1).astype(np.float32))
    u = np.arange(LANES)[None, :]
    dnew = t - u
    cnew = np.where((t < DEC_SEQ) & (dnew > 0), _branch_count(dnew),
                    np.where((t < DEC_SEQ) & (dnew == 0), len(DILATIONS), 0))
    bkt_new = np.where(cnew > 0, _rel_bucket(dnew), -1).astype(np.int32)
    ladd_new = np.log(np.maximum(cnew, 1).astype(np.float32))
    return tuple(jnp.asarray(a) for a in (bkt, ladd, bkt_new, ladd_new))


def _attn_sample_init(tbl_ref, bkt_ref, ladd_ref, bktn_ref, laddn_ref, bias, bias_new):
    @pl.when(pl.program_id(0) == 0)
    def _build_bias():
        for h in range(ATT_HEADS):
            for src, add, dst in ((bkt_ref, ladd_ref, bias), (bktn_ref, laddn_ref, bias_new)):
                bk = src[...]
                acc = jnp.full(bk.shape, NEG_INF, F32)
                for u in range(N_BUCKETS):
                    acc = jnp.where(bk == u, tbl_ref[u, h], acc)
                dst[h // 2, (h % 2) * NEW_ROWS:(h % 2 + 1) * NEW_ROWS, :] = acc + add[...]


N_ATTN_S_IN = 8


def _attn_sample_kernel(tbl_ref, q_ref, kt_ref, vt_ref, bkt_ref, ladd_ref, bktn_ref, laddn_ref,
                        o_ref, bias, bias_new):
    _attn_sample_init(tbl_ref, bkt_ref, ladd_ref, bktn_ref, laddn_ref, bias, bias_new)
    _attn_sample_seq(q_ref, kt_ref, vt_ref, o_ref, bias, bias_new)


def _attn_sample_seq(q_ref, kt_ref, vt_ref, o_ref, bias, bias_new):
    left = lax.broadcasted_iota(jnp.int32, (1, LANES), 1) < HEAD_DIM
    nt = (((1,), (1,)), ((), ()))
    scale = HEAD_DIM ** -0.5
    outs = []
    for hp in range(ATT_W // LANES):
        cols = slice(hp * LANES, (hp + 1) * LANES)
        q = q_ref[0, :, cols] * scale
        fill = jnp.zeros((LANES - NEW_ROWS, LANES), F32)
        knew = jnp.concatenate([q_ref[0, :, ATT_W + hp * LANES:ATT_W + (hp + 1) * LANES], fill], axis=0)
        vnew = jnp.concatenate([q_ref[0, :, 2 * ATT_W + hp * LANES:2 * ATT_W + (hp + 1) * LANES], fill], axis=0)
        qq = jnp.concatenate([jnp.where(left, q, 0.0), jnp.where(left, 0.0, q)], axis=0).astype(BF16)
        s = jnp.dot(qq, kt_ref[0, cols, :].astype(BF16), preferred_element_type=F32) + bias[hp]
        sn = lax.dot_general(qq, knew.astype(BF16), nt, preferred_element_type=F32) + bias_new[hp]
        m = jnp.maximum(jnp.max(s, axis=-1, keepdims=True), jnp.max(sn, axis=-1, keepdims=True))
        m = jnp.where(m == NEG_INF, 0.0, m)
        p = jnp.exp(s - m)
        pn = jnp.exp(sn - m)
        den = jnp.sum(p, axis=-1, keepdims=True) + jnp.sum(pn, axis=-1, keepdims=True)
        r = lax.dot_general(p.astype(BF16), vt_ref[0, cols, :].astype(BF16), nt, preferred_element_type=F32)
        r = r + jnp.dot(pn.astype(BF16), vnew.astype(BF16), preferred_element_type=F32)
        o = r / jnp.where(den == 0.0, 1.0, den)
        outs.append(jnp.where(left, o[:NEW_ROWS], o[NEW_ROWS:]))
    o_ref[0] = jnp.concatenate(outs, axis=1)[:DEC_SEQ]


def _attn_sample_call(qkv, cache_kt, cache_vt, rel_bias):
    db, _, wb = cache_kt.shape
    assert wb >= WIN_MAX and qkv.shape[0] == db * DEC_SEQ
    q8 = jnp.pad(qkv.reshape(db, DEC_SEQ, 3 * ATT_W), ((0, 0), (0, NEW_ROWS - DEC_SEQ), (0, 0)))
    consts = list(_sample_bias_tables(wb))
    hp_n = ATT_W // LANES
    seq = lambda n, m: pl.BlockSpec((1, n, m), lambda b: (b, 0, 0))
    return dict(
        steps=db,
        operands=[rel_bias, q8, cache_kt, cache_vt] + consts,
        in_specs=[pl.BlockSpec(memory_space=pltpu.SMEM), seq(NEW_ROWS, 3 * ATT_W), seq(ATT_W, wb),
                  seq(ATT_W, wb)] + [_const_spec(c.shape) for c in consts],
        out_specs=[seq(DEC_SEQ, ATT_W)],
        out_shape=[jax.ShapeDtypeStruct((db, DEC_SEQ, ATT_W), F32)],
        scratch=[pltpu.VMEM((hp_n, 2 * NEW_ROWS, wb), F32), pltpu.VMEM((hp_n, 2 * NEW_ROWS, LANES), F32)])


def _attn_sample(qkv, cache_kt, cache_vt, rel_bias):
    out, = _run(_attn_sample_kernel, "attn_sample", _attn_sample_call(qkv, cache_kt, cache_vt, rel_bias))
    return out.reshape(-1, ATT_W)


def _ssd_prompt_attn_sample_kernel(*refs):
    a = N_SSD_IN + N_ATTN_S_IN
    ssd_in = refs[:N_SSD_IN]
    tbl_ref, q_ref, kt_ref, vt_ref, bkt_ref, ladd_ref, bktn_ref, laddn_ref = refs[N_SSD_IN:a]
    y_ref, st_ref, o_ref, cbuf, state, yoff, bias, bias_new = refs[a:]
    _ssd_prompt_init(cbuf, state)
    _attn_sample_init(tbl_ref, bkt_ref, ladd_ref, bktn_ref, laddn_ref, bias, bias_new)
    _ssd_prompt_chunk(*ssd_in, y_ref, cbuf, state, yoff)
    _attn_sample_seq(q_ref, kt_ref, vt_ref, o_ref, bias, bias_new)
    _ssd_prompt_emit(st_ref, state)


def _ssd_prompt_attn_sample(ssd_args, attn_args):
    y, st, att = _run(_ssd_prompt_attn_sample_kernel, "ssd_prompt_attn_sample",
                      _ssd_prompt_call(*ssd_args), _attn_sample_call(*attn_args))
    return y, st, att.reshape(-1, ATT_W)


SEQ_ROWS = 8
SEQ_PER_STEP = 8


def _ssd_sample_kernel(z_ref, xp_ref, dt_ref, st_ref, cw_ref, cb_ref, dtb_ref, alog_ref, dsk_ref, gout_ref,
                       hexp_ref, gexp_ref, gred_ref, y_ref, sto_ref, cbuf, yoff):
    rows = SEQ_PER_STEP * SEQ_ROWS
    cbuf[:, 0:CARRY_ROW, :] = jnp.zeros((CONV_DIM // LANES, CARRY_ROW, LANES), F32)
    xc = _conv_silu(cbuf, xp_ref, cw_ref, cb_ref, rows)
    nx = SSD_W // LANES
    xs = jnp.concatenate(xc[:nx], axis=1)
    bmf = xc[nx:nx + N_GROUPS]
    bm = [b.astype(BF16) for b in bmf]
    cm = [b.astype(BF16) for b in xc[nx + N_GROUPS:]]

    li = lax.broadcasted_iota(jnp.int32, (rows, rows), 0)
    si = lax.broadcasted_iota(jnp.int32, (rows, rows), 1)
    same = (li // SEQ_ROWS) == (si // SEQ_ROWS)
    mask = same & (li >= si)
    rowi = lax.broadcasted_iota(jnp.int32, (rows, 1), 0)
    is_token = (rowi % SEQ_ROWS) >= SEQ_ROWS - DEC_SEQ

    dt = jnp.where(is_token, _softplus(dt_ref[...] + dtb_ref[...]), 0.0)
    acs = _dot_exact_lhs(mask.astype(BF16), dt * (-jnp.exp(alog_ref[...])))
    acs_end = _dot_exact_lhs(same.astype(BF16), dt * (-jnp.exp(alog_ref[...])))
    hexp = hexp_ref[...]
    e_in, e_end, e_all = _dot_exact_rhs(
        [(jnp.exp(acs), 2), (jnp.exp(acs_end - acs) * dt, 1), (jnp.exp(acs_end), 3)], hexp)
    xw = xs * e_end

    sub = rowi % SEQ_ROWS
    pieces = [p.astype(F32) for p in _split(e_all, 3)]
    dstack = jnp.where(sub == 0, pieces[0], jnp.where(sub == 1, pieces[1], jnp.where(sub == 2, pieces[2], 0.0)))
    ones = jnp.ones((SEQ_ROWS, D_STATE), BF16)
    gmask = lax.broadcasted_iota(jnp.int32, (1, SSD_W), 1) // GROUP_W
    tn = (((0,), (0,)), ((), ()))
    nt = (((1,), (1,)), ((), ()))
    for i in range(SEQ_PER_STEP):
        rs = slice(i * SEQ_ROWS, (i + 1) * SEQ_ROWS)
        for g in range(N_GROUPS):
            lo = GROUP_LO[g]
            win = slice(lo, lo + GROUP_WIN)
            inside = gmask[:, win] == g
            yo = lax.dot_general(cm[g][rs], st_ref[i, win, :].astype(BF16), nt, preferred_element_type=F32)
            yoff[rs, win] = jnp.where(inside, yo, 0.0 if g == 0 else yoff[rs, win])
        xstack = jnp.concatenate([jnp.where(gmask == g, xw[rs, :], 0.0) for g in range(N_GROUPS)], axis=0)
        bstack = jnp.concatenate([bmf[g][rs] for g in range(N_GROUPS)], axis=0)
        upd = lax.dot_general(xstack.astype(BF16), bstack.astype(BF16), tn, preferred_element_type=F32)
        dcol = lax.dot_general(dstack[rs, :].astype(BF16), ones, tn, preferred_element_type=F32)
        sto_ref[i] = st_ref[i] * dcol + upd

    ydiag = _ssd_ydiag(xs, bm, cm, acs, dt, mask)
    y_ref[...] = _ssd_gate_norm(ydiag + yoff[...] * e_in + xs * dsk_ref[...], z_ref[...],
                                gred_ref[...], gexp_ref[...], gout_ref[...])


def _ssd_sample(z, xbc, dt, conv_state, ssm_state, conv_w, conv_b, dt_bias, a_log, dskip_x, g_out):
    db = conv_state.shape[0]
    assert db % SEQ_PER_STEP == 0 and z.shape[0] == db * DEC_SEQ
    lead = SEQ_ROWS - DEC_SEQ

    def padded(a):
        a = a.reshape(db, DEC_SEQ, a.shape[-1])
        return jnp.pad(a, ((0, 0), (lead, 0), (0, 0))).reshape(db * SEQ_ROWS, a.shape[-1])

    xp = jnp.concatenate([jnp.zeros((db, lead - (CONV_W - 1), CONV_DIM), F32), conv_state,
                          xbc.reshape(db, DEC_SEQ, CONV_DIM)], axis=1).reshape(db * SEQ_ROWS, CONV_DIM)
    hexp, gexp, gred = _ssd_consts()
    rows = SEQ_PER_STEP * SEQ_ROWS
    row = lambda n: pl.BlockSpec((rows, n), lambda i: (i, 0))
    st_spec = pl.BlockSpec((SEQ_PER_STEP, SSD_W, D_STATE), lambda i: (i, 0, 0))
    consts = [conv_w, conv_b, dt_bias, a_log, dskip_x, g_out, hexp, gexp, gred]
    y, st = pl.pallas_call(
        _ssd_sample_kernel,
        grid=(db // SEQ_PER_STEP,),
        in_specs=[row(SSD_W), row(CONV_DIM), row(LANES), st_spec] + [_const_spec(a.shape) for a in consts],
        out_specs=[row(SSD_W), st_spec],
        out_shape=[jax.ShapeDtypeStruct((db * SEQ_ROWS, SSD_W), F32),
                   jax.ShapeDtypeStruct((db, SSD_W, D_STATE), F32)],
        scratch_shapes=[pltpu.VMEM((CONV_DIM // LANES, CARRY_ROW + rows, LANES), F32),
                        pltpu.VMEM((rows, SSD_W), F32)],
        compiler_params=pltpu.CompilerParams(dimension_semantics=("parallel",),
                                             vmem_limit_bytes=VMEM_LIMIT),
        name="ssd_sample",
    )(padded(z), xp, padded(dt), ssm_state, *consts)
    y = y.reshape(db, SEQ_ROWS, SSD_W)[:, lead:].reshape(db * DEC_SEQ, SSD_W)
    return y, st


def kernel(x_prompt, x_sample, cache_k, cache_v, state_conv, state_ssm, rel_bias, g_mix, w_in, conv_w, conv_b,
           dt_bias, a_log, d_skip, g_attn_out, g_ssd_out, w_out, g_ffn, w_gate, w_up, w_down, g_final):
    depth = w_in.shape[0]
    bp, sp, _ = x_prompt.shape
    db, ds, _ = x_sample.shape
    assert bp == 1 and ds == DEC_SEQ and sp >= WIN_MAX
    xp = x_prompt.reshape(bp * sp, D_MODEL)
    xs = x_sample.reshape(db * ds, D_MODEL)
    pad_heads = lambda v: jnp.pad(v, (0, LANES - SSD_HEADS))[None]
    o_z, o_xbc, o_dt = 3 * ATT_W, 3 * ATT_W + SSD_W, 3 * ATT_W + SSD_W + CONV_DIM
    outs = [[] for _ in range(8)]
    for l in range(depth):
        wi = w_in[l].T
        wqkv, wz, wxbc = (wi[:o_z].astype(BF16), wi[o_z:o_xbc].astype(BF16), wi[o_xbc:o_dt].astype(BF16))
        wdt = jnp.pad(wi[o_dt:], ((0, LANES - SSD_HEADS), (0, 0))).astype(BF16)
        ssd_prm = (conv_w[l], conv_b[l][None], pad_heads(dt_bias[l]), pad_heads(a_log[l]),
                   jnp.repeat(d_skip[l], SSD_HEAD_DIM)[None], g_ssd_out[l][None])
        tail_prm = (g_attn_out[l][None], w_out[l][:ATT_W].astype(BF16), w_out[l][ATT_W:].astype(BF16),
                    g_ffn[l][None], w_gate[l].astype(BF16), w_up[l].astype(BF16), w_down[l].astype(BF16),
                    g_final[None])
        last = l == depth - 1

        qkv_p, z_p, xbc_p, dt_p = _inproj(xp, g_mix[l][None], wqkv, wz, wxbc, wdt)
        qkv_s, z_s, xbc_s, dt_s = _inproj(xs, g_mix[l][None], wqkv, wz, wxbc, wdt)
        att_p = _attn_prompt(qkv_p, rel_bias)
        feature_major = lambda c: jnp.transpose(c, (0, 2, 3, 1)).reshape(db, ATT_W, c.shape[1])
        ssd_args = (z_p, xbc_p, dt_p, *ssd_prm)
        attn_args = (qkv_s, feature_major(cache_k[l]), feature_major(cache_v[l]), rel_bias)
        if sp // SSD_CHUNK == db:
            ssd_p, st_p, att_s = _ssd_prompt_attn_sample(ssd_args, attn_args)
        else:
            ssd_p, st_p = _ssd_prompt(*ssd_args)
            att_s = _attn_sample(*attn_args)
        xp = _tail(xp, att_p, ssd_p, *tail_prm, final_norm=last)
        ssd_s, st_s = _ssd_sample(z_s, xbc_s, dt_s, state_conv[l], state_ssm[l].reshape(db, SSD_W, D_STATE),
                                  *ssd_prm)
        xs = _tail(xs, att_s, ssd_s, *tail_prm, final_norm=last)

        heads = lambda a: a.reshape(a.shape[0], a.shape[1], ATT_HEADS, HEAD_DIM)
        kv_p = qkv_p.reshape(bp, sp, 3 * ATT_W)[:, sp - WIN_MAX:]
        kv_s = qkv_s.reshape(db, ds, 3 * ATT_W)
        outs[0].append(heads(kv_p[..., ATT_W:2 * ATT_W]))
        outs[1].append(heads(kv_p[..., 2 * ATT_W:]))
        outs[2].append(xbc_p.reshape(bp, sp, CONV_DIM)[:, sp - (CONV_W - 1):])
        outs[3].append(st_p.reshape(bp, SSD_HEADS, SSD_HEAD_DIM, D_STATE))
        outs[4].append(heads(kv_s[..., ATT_W:2 * ATT_W]))
        outs[5].append(heads(kv_s[..., 2 * ATT_W:]))
        outs[6].append(xbc_s.reshape(db, ds, CONV_DIM)[:, ds - (CONV_W - 1):])
        outs[7].append(st_s.reshape(db, SSD_HEADS, SSD_HEAD_DIM, D_STATE))
    return (xp.reshape(bp, sp, D_MODEL), xs.reshape(db, ds, D_MODEL)) + tuple(jnp.stack(o) for o in outs)
```

```python
import functools
import math

import jax
import jax.numpy as jnp
import numpy as np
from jax import lax
from jax.experimental import pallas as pl
from jax.experimental.pallas import tpu as pltpu

F32 = jnp.float32
BF16 = jnp.bfloat16

D_MODEL = 1024
ATT_HEADS = 12
HEAD_DIM = 64
ATT_W = ATT_HEADS * HEAD_DIM
DILATIONS = (1, 4, 16)
QBLK = 128
WIN_MAX = 2048
N_BUCKETS = 32
MAX_EXACT = N_BUCKETS // 2
SSD_HEADS = 20
SSD_HEAD_DIM = 64
SSD_W = SSD_HEADS * SSD_HEAD_DIM
N_GROUPS = 4
D_STATE = 128
CONV_W = 4
CONV_DIM = SSD_W + 2 * N_GROUPS * D_STATE
SSD_CHUNK = 128
EPS = 1e-6

LANES = 128
VMEM_LIMIT = 48 * 1024 * 1024


def _rms(x, g):
    return x * lax.rsqrt(jnp.mean(x * x, axis=-1, keepdims=True) + EPS) * g


def _const_spec(shape):
    nd = len(shape)
    return pl.BlockSpec(shape, lambda *_: (0,) * nd, pipeline_mode=pl.Buffered(1))


def _inproj_kernel(x_ref, g_ref, wqkv_ref, wz_ref, wxbc_ref, wdt_ref,
                   qkv_ref, z_ref, xbc_ref, dt_ref):
    h = _rms(x_ref[...], g_ref[...]).astype(BF16)
    nt = (((1,), (1,)), ((), ()))
    qkv_ref[...] = lax.dot_general(h, wqkv_ref[...], nt, preferred_element_type=F32)
    z_ref[...] = lax.dot_general(h, wz_ref[...], nt, preferred_element_type=F32)
    xbc_ref[...] = lax.dot_general(h, wxbc_ref[...], nt, preferred_element_type=F32)
    dt_ref[...] = lax.dot_general(h, wdt_ref[...], nt, preferred_element_type=F32)


def _inproj(x, g_mix, wqkv, wz, wxbc, wdt, tm=256):
    t = x.shape[0]
    assert t % tm == 0
    row = lambda n: pl.BlockSpec((tm, n), lambda i: (i, 0))
    return pl.pallas_call(
        _inproj_kernel,
        grid=(t // tm,),
        in_specs=[row(D_MODEL), _const_spec((1, D_MODEL)), _const_spec(wqkv.shape),
                  _const_spec(wz.shape), _const_spec(wxbc.shape), _const_spec(wdt.shape)],
        out_specs=[row(3 * ATT_W), row(SSD_W), row(CONV_DIM), row(LANES)],
        out_shape=[jax.ShapeDtypeStruct((t, 3 * ATT_W), F32), jax.ShapeDtypeStruct((t, SSD_W), F32),
                   jax.ShapeDtypeStruct((t, CONV_DIM), F32), jax.ShapeDtypeStruct((t, LANES), F32)],
        compiler_params=pltpu.CompilerParams(dimension_semantics=("parallel",),
                                             vmem_limit_bytes=VMEM_LIMIT),
        name="inproj",
    )(x, g_mix, wqkv, wz, wxbc, wdt)


def _tail_kernel(x_ref, att_ref, ssd_ref, gatt_ref, woa_ref, wos_ref, gffn_ref,
                 wg_ref, wu_ref, wd_ref, gfin_ref, y_ref, *, final_norm):
    an = _rms(att_ref[...], gatt_ref[...]).astype(BF16)
    mix = jnp.dot(an, woa_ref[...], preferred_element_type=F32)
    mix = mix + jnp.dot(ssd_ref[...].astype(BF16), wos_ref[...], preferred_element_type=F32)
    x1 = x_ref[...] + mix
    h2 = _rms(x1, gffn_ref[...]).astype(BF16)
    gate = jnp.dot(h2, wg_ref[...], preferred_element_type=F32)
    up = jnp.dot(h2, wu_ref[...], preferred_element_type=F32)
    act = (gate * jax.nn.sigmoid(gate) * up).astype(BF16)
    x2 = x1 + jnp.dot(act, wd_ref[...], preferred_element_type=F32)
    y_ref[...] = _rms(x2, gfin_ref[...]) if final_norm else x2


def _tail(x, att, ssd, g_att, woa, wos, g_ffn, wg, wu, wd, g_fin, final_norm, tm=256):
    t = x.shape[0]
    assert t % tm == 0
    row = lambda n: pl.BlockSpec((tm, n), lambda i: (i, 0))
    consts = [g_att, woa, wos, g_ffn, wg, wu, wd, g_fin]
    return pl.pallas_call(
        functools.partial(_tail_kernel, final_norm=final_norm),
        grid=(t // tm,),
        in_specs=[row(D_MODEL), row(ATT_W), row(SSD_W)] + [_const_spec(c.shape) for c in consts],
        out_specs=row(D_MODEL),
        out_shape=jax.ShapeDtypeStruct((t, D_MODEL), F32),
        compiler_params=pltpu.CompilerParams(dimension_semantics=("parallel",),
                                             vmem_limit_bytes=VMEM_LIMIT),
        name="tail",
    )(x, att, ssd, *consts)


TQ = QBLK * max(DILATIONS)
NEG_INF = float("-inf")
LOG2E = math.log2(math.e)
BLOCK_UNROLL = 16


def _rel_bucket(dist):
    n = np.maximum(dist, 0)
    nf = np.maximum(n, 1).astype(np.float32)
    large = MAX_EXACT + (np.log(nf / MAX_EXACT) / math.log(WIN_MAX / MAX_EXACT)
                         * (N_BUCKETS - MAX_EXACT)).astype(np.int32)
    return np.where(n < MAX_EXACT, n, np.minimum(large, N_BUCKETS - 1))


def _band_buckets():
    qi = np.arange(QBLK)[:, None] + QBLK
    kk = np.arange(2 * QBLK)[None, :]
    dist = qi - kk
    valid = (dist >= 0) & (dist <= QBLK)
    return jnp.asarray(np.stack([np.where(valid, _rel_bucket(dist * d), -1) for d in DILATIONS]).astype(np.int32))


STAGE = 4
assert DILATIONS == (1, STAGE, STAGE * STAGE)


def _regroup(src_ref, tmp, dsts, scale=None):
    cast = (lambda x: x.astype(BF16)) if scale is None else (lambda x: (x * scale).astype(BF16))
    sub = TQ // STAGE
    for c in range(STAGE):
        dsts[0][c * sub:(c + 1) * sub, :] = cast(src_ref[c * sub:(c + 1) * sub, :])
    for r in range(STAGE):
        x = src_ref[pl.ds(r, sub, stride=STAGE), :]
        tmp[r] = x
        dsts[1][r * sub:(r + 1) * sub, :] = cast(x)
    for r in range(STAGE):
        for r2 in range(STAGE):
            res = r + STAGE * r2
            dsts[2][res * QBLK:(res + 1) * QBLK, :] = cast(tmp[r, pl.ds(r2, QBLK, stride=STAGE), :])


def _attn_prompt_kernel(tbl_ref, q_ref, k_ref, v_ref, bkt_ref, o_ref,
                        qd, kd, vd, tmp, bias, macc, nacc, dacc):
    hp = pl.program_id(0)
    t = pl.program_id(1)
    nb = len(DILATIONS)
    slot = t % 2

    @pl.when(t == 0)
    def _no_previous_tile():
        kd[:, 1] = jnp.zeros((nb, TQ, LANES), BF16)
        vd[:, 1] = jnp.zeros((nb, TQ, LANES), BF16)

    @pl.when(t == 0)
    def _build_bias():
        prev_half = lax.broadcasted_iota(jnp.int32, (1, 2 * QBLK), 1) < QBLK
        for b in range(len(DILATIONS)):
            bk = bkt_ref[b]
            for hh in range(2):
                acc = jnp.full((QBLK, 2 * QBLK), NEG_INF, F32)
                for u in range(N_BUCKETS):
                    acc = jnp.where(bk == u, tbl_ref[u, 2 * hp + hh], acc)
                acc = acc * LOG2E
                bias[b, 0, hh * QBLK:(hh + 1) * QBLK] = acc
                bias[b, 1, hh * QBLK:(hh + 1) * QBLK] = jnp.where(prev_half, NEG_INF, acc)

    _regroup(q_ref, tmp, [qd.at[b] for b in range(nb)], HEAD_DIM ** -0.5 * LOG2E)
    _regroup(k_ref, tmp, [kd.at[b, slot] for b in range(nb)])
    _regroup(v_ref, tmp, [vd.at[b, slot] for b in range(nb)])

    left = lax.broadcasted_iota(jnp.int32, (1, LANES), 1) < HEAD_DIM
    nt = (((1,), (1,)), ((), ()))

    for b, d in enumerate(DILATIONS):
        nbr = max(DILATIONS) // d

        def rows(start, size, d=d):
            return pl.ds(start, size) if d == 1 else pl.ds(start, size, stride=d)

        def block(j, carry, b=b, d=d, nbr=nbr, rows=rows):
            r = j // nbr
            m = j % nbr
            qs = r + d * QBLK * m
            gs = pl.multiple_of((r * nbr + m) * QBLK, QBLK)
            wrap = m == 0
            ps = pl.multiple_of((r * nbr + jnp.where(wrap, nbr - 1, m - 1)) * QBLK, QBLK)
            pslot = jnp.where(wrap, 1 - slot, slot)
            q = qd[b, pl.ds(gs, QBLK), :]
            kc = jnp.concatenate([kd[b, pslot, pl.ds(ps, QBLK), :], kd[b, slot, pl.ds(gs, QBLK), :]], axis=0)
            vc = jnp.concatenate([vd[b, pslot, pl.ds(ps, QBLK), :], vd[b, slot, pl.ds(gs, QBLK), :]], axis=0)
            first = jnp.logical_and(wrap, t == 0).astype(jnp.int32)
            zero = jnp.zeros_like(q)
            qq = jnp.concatenate([jnp.where(left, q, zero), jnp.where(left, zero, q)], axis=0)
            s = lax.dot_general(qq, kc, nt, preferred_element_type=F32) + bias[b, first]
            mx = jnp.max(s, axis=-1, keepdims=True)
            e = jnp.exp2(s - mx)
            den = jnp.sum(e, axis=-1, keepdims=True)
            res = jnp.dot(e.astype(BF16), vc, preferred_element_type=F32)
            nacc[b, rows(qs, QBLK), :] = jnp.where(left, res[:QBLK], res[QBLK:])
            dacc[b, rows(qs, QBLK), :] = jnp.where(left, den[:QBLK], den[QBLK:])
            macc[b, rows(qs, QBLK), :] = jnp.where(left, mx[:QBLK], mx[QBLK:])
            return carry

        lax.fori_loop(0, max(DILATIONS), block, 0, unroll=BLOCK_UNROLL)

    def merge(c, carry):
        sl = pl.ds(pl.multiple_of(c * QBLK, QBLK), QBLK)
        ms = [macc[b, sl, :] for b in range(len(DILATIONS))]
        top = jnp.maximum(jnp.maximum(ms[0], ms[1]), ms[2])
        ws = [jnp.exp2(mb - top) for mb in ms]
        num = sum(w * nacc[b, sl, :] for b, w in enumerate(ws))
        den = sum(w * dacc[b, sl, :] for b, w in enumerate(ws))
        o_ref[sl, :] = num / den
        return carry

    lax.fori_loop(0, TQ // QBLK, merge, 0)


def _attn_prompt(qkv, rel_bias):
    s = qkv.shape[0]
    assert s % TQ == 0
    nb = len(DILATIONS)
    hp_n = ATT_W // LANES
    tile = lambda f: pl.BlockSpec((TQ, LANES), f)
    return pl.pallas_call(
        _attn_prompt_kernel,
        grid=(hp_n, s // TQ),
        in_specs=[pl.BlockSpec(memory_space=pltpu.SMEM),
                  tile(lambda h, t: (t, h)),
                  tile(lambda h, t: (t, hp_n + h)),
                  tile(lambda h, t: (t, 2 * hp_n + h)),
                  _const_spec((nb, QBLK, 2 * QBLK))],
        out_specs=tile(lambda h, t: (t, h)),
        out_shape=jax.ShapeDtypeStruct((s, ATT_W), F32),
        scratch_shapes=[pltpu.VMEM((nb, TQ, LANES), BF16), pltpu.VMEM((nb, 2, TQ, LANES), BF16),
                        pltpu.VMEM((nb, 2, TQ, LANES), BF16), pltpu.VMEM((STAGE, TQ // STAGE, LANES), F32),
                        pltpu.VMEM((nb, 2, 2 * QBLK, 2 * QBLK), F32),
                        pltpu.VMEM((nb, TQ, LANES), F32), pltpu.VMEM((nb, TQ, LANES), F32),
                        pltpu.VMEM((nb, TQ, LANES), F32)],
        compiler_params=pltpu.CompilerParams(dimension_semantics=("parallel", "arbitrary"),
                                             vmem_limit_bytes=VMEM_LIMIT),
        name="attn_prompt",
    )(rel_bias, qkv, qkv, qkv, _band_buckets())


HPG = SSD_HEADS // N_GROUPS
GROUP_W = SSD_W // N_GROUPS
GROUP_WIN = 3 * LANES
GROUP_LO = tuple((g * GROUP_W) // LANES * LANES for g in range(N_GROUPS))
CARRY_ROW = 8


def _split(x, terms):
    parts = []
    for _ in range(terms):
        p = x.astype(BF16)
        parts.append(p)
        x = x - p.astype(F32)
    return parts


def _dot_exact_rhs(xs_terms, m):
    pieces = [p for x, terms in xs_terms for p in _split(x, terms)]
    out = jnp.dot(jnp.concatenate(pieces, axis=0) if len(pieces) > 1 else pieces[0], m,
                  preferred_element_type=F32)
    res, r = [], 0
    for x, terms in xs_terms:
        n = x.shape[0]
        res.append(sum(out[r + i * n:r + (i + 1) * n] for i in range(terms)))
        r += terms * n
    return res


def _dot_exact_lhs(m, x, terms=3):
    n = x.shape[1]
    out = jnp.dot(m, jnp.concatenate(_split(x, terms), axis=1), preferred_element_type=F32)
    return sum(out[:, i * n:(i + 1) * n] for i in range(terms))


def _softplus(x):
    return jnp.maximum(x, 0.0) + jnp.log1p(jnp.exp(-jnp.abs(x)))


def _silu(x):
    h = 0.5 * x
    return h + h * jnp.tanh(h)


def _ssd_consts():
    c = jnp.arange(SSD_W)
    head_of = (c // SSD_HEAD_DIM)[None, :] == jnp.arange(LANES)[:, None]
    group_of = (c // GROUP_W)[None, :] == jnp.arange(LANES)[:, None]
    return head_of.astype(BF16), group_of.astype(BF16), group_of.T.astype(BF16)


def _conv_silu(cbuf, xbc_ref, w_ref, b_ref, rows):
    out = []
    for c in range(CONV_DIM // LANES):
        cols = slice(c * LANES, (c + 1) * LANES)
        cbuf[c, CARRY_ROW:CARRY_ROW + rows, :] = xbc_ref[:, cols]
        acc = b_ref[:, cols]
        for j in range(CONV_W):
            acc = acc + cbuf[c, pl.ds(CARRY_ROW - (CONV_W - 1) + j, rows), :] * w_ref[j:j + 1, cols]
        out.append(_silu(acc))
    return out


def _ssd_ydiag(xs, bm, cm, acs, dt, mask):
    cbs = [lax.dot_general(cm[g], bm[g], (((1,), (1,)), ((), ())), preferred_element_type=F32)
           for g in range(N_GROUPS)]
    acs_t = acs.T
    dt_t = dt.T
    left = lax.broadcasted_iota(jnp.int32, (1, LANES), 1) < SSD_HEAD_DIM
    pairs = []
    for pair in range(SSD_HEADS // 2):
        xpair = xs[:, pair * LANES:(pair + 1) * LANES]
        acc = None
        for hh in range(2):
            h = 2 * pair + hh
            seg = acs[:, h:h + 1] - acs_t[h:h + 1, :]
            dec = jnp.exp2(jnp.where(mask, seg, NEG_INF))
            w = (cbs[h // HPG] * dec * dt_t[h:h + 1, :]).astype(BF16)
            xh = jnp.where(left if hh == 0 else jnp.logical_not(left), xpair, 0.0).astype(BF16)
            part = jnp.dot(w, xh, preferred_element_type=F32)
            acc = part if acc is None else acc + part
        pairs.append(acc)
    return jnp.concatenate(pairs, axis=1)


def _ssd_gate_norm(y, z, gred, gexp, gout):
    y = y * _silu(z)
    ms = _dot_exact_rhs([(y * y, 2)], gred)[0] * (1.0 / GROUP_W)
    return y * _dot_exact_rhs([(lax.rsqrt(ms + EPS), 2)], gexp)[0] * gout


def _ssd_prompt_init(cbuf, state):
    @pl.when(pl.program_id(0) == 0)
    def _init():
        state[...] = jnp.zeros_like(state)
        cbuf[:, 0:CARRY_ROW, :] = jnp.zeros((CONV_DIM // LANES, CARRY_ROW, LANES), F32)


def _ssd_prompt_emit(st_ref, state):
    @pl.when(pl.program_id(0) == pl.num_programs(0) - 1)
    def _emit_state():
        for i in range(SSD_W // LANES):
            st_ref[i * LANES:(i + 1) * LANES, :] = state[:, i * LANES:(i + 1) * LANES].T


def _ssd_prompt_chunk(z_ref, xbc_ref, dt_ref, cw_ref, cb_ref, dtb_ref, alog_ref, dsk_ref, gout_ref,
                      hexp_ref, gexp_ref, gred_ref, y_ref, cbuf, state, yoff):
    cl = SSD_CHUNK
    xc = _conv_silu(cbuf, xbc_ref, cw_ref, cb_ref, cl)
    cbuf[:, CARRY_ROW - (CONV_W - 1):CARRY_ROW, :] = cbuf[:, CARRY_ROW + cl - (CONV_W - 1):CARRY_ROW + cl, :]
    nx = SSD_W // LANES
    xs = jnp.concatenate(xc[:nx], axis=1)
    bm = [b.astype(BF16) for b in xc[nx:nx + N_GROUPS]]
    cm = [b.astype(BF16) for b in xc[nx + N_GROUPS:]]

    li = lax.broadcasted_iota(jnp.int32, (cl, cl), 0)
    si = lax.broadcasted_iota(jnp.int32, (cl, cl), 1)
    causal = li >= si

    dt = _softplus(dt_ref[...] + dtb_ref[...])
    acs = _dot_exact_lhs(causal.astype(BF16), dt * (-jnp.exp(alog_ref[...]) * LOG2E))
    hexp = hexp_ref[...]
    e_in, e_end = _dot_exact_rhs([(jnp.exp2(acs), 2), (jnp.exp2(acs[cl - 1:cl, :] - acs) * dt, 1)], hexp)
    xw = (xs * e_end).astype(BF16)
    sub = lax.broadcasted_iota(jnp.int32, (CARRY_ROW, 1), 0)
    last = [p.astype(F32) for p in _split(jnp.exp2(acs[cl - 1:cl, :]), 3)]
    stack = jnp.where(sub == 0, last[0], jnp.where(sub == 1, last[1], jnp.where(sub == 2, last[2], 0.0)))
    chunk_decay = jnp.sum(jnp.dot(stack.astype(BF16), hexp, preferred_element_type=F32), axis=0, keepdims=True)

    gmask = lax.broadcasted_iota(jnp.int32, (1, SSD_W), 1) // GROUP_W
    for g in range(N_GROUPS):
        lo = GROUP_LO[g]
        win = slice(lo, lo + GROUP_WIN)
        inside = gmask[:, win] == g
        yo = jnp.dot(cm[g], state[:, win].astype(BF16), preferred_element_type=F32)
        yoff[:, win] = jnp.where(inside, yo, 0.0 if g == 0 else yoff[:, win])
    for g in range(N_GROUPS):
        lo = GROUP_LO[g]
        win = slice(lo, lo + GROUP_WIN)
        inside = gmask[:, win] == g
        upd = lax.dot_general(bm[g], xw[:, win], (((0,), (0,)), ((), ())), preferred_element_type=F32)
        old = state[:, win]
        state[:, win] = jnp.where(inside, old * chunk_decay[:, win] + upd, old)

    ydiag = _ssd_ydiag(xs, bm, cm, acs, dt, causal)
    y_ref[...] = _ssd_gate_norm(ydiag + yoff[...] * e_in + xs * dsk_ref[...], z_ref[...],
                                gred_ref[...], gexp_ref[...], gout_ref[...])


N_SSD_IN = 12


def _ssd_prompt_kernel(*refs):
    ins, (y_ref, st_ref, cbuf, state, yoff) = refs[:N_SSD_IN], refs[N_SSD_IN:]
    _ssd_prompt_init(cbuf, state)
    _ssd_prompt_chunk(*ins, y_ref, cbuf, state, yoff)
    _ssd_prompt_emit(st_ref, state)


def _ssd_prompt_call(z, xbc, dt, conv_w, conv_b, dt_bias, a_log, dskip_x, g_out):
    s = z.shape[0]
    assert s % SSD_CHUNK == 0
    hexp, gexp, gred = _ssd_consts()
    row = lambda n: pl.BlockSpec((SSD_CHUNK, n), lambda i: (i, 0))
    consts = [conv_w, conv_b, dt_bias, a_log, dskip_x, g_out, hexp, gexp, gred]
    return dict(
        steps=s // SSD_CHUNK,
        operands=[z, xbc, dt] + consts,
        in_specs=[row(SSD_W), row(CONV_DIM), row(LANES)] + [_const_spec(a.shape) for a in consts],
        out_specs=[row(SSD_W), pl.BlockSpec((SSD_W, D_STATE), lambda i: (0, 0))],
        out_shape=[jax.ShapeDtypeStruct((s, SSD_W), F32), jax.ShapeDtypeStruct((SSD_W, D_STATE), F32)],
        scratch=[pltpu.VMEM((CONV_DIM // LANES, CARRY_ROW + SSD_CHUNK, LANES), F32),
                 pltpu.VMEM((D_STATE, SSD_W), F32), pltpu.VMEM((SSD_CHUNK, SSD_W), F32)])


def _run(kernel_fn, name, *calls):
    steps = calls[0]["steps"]
    assert all(c["steps"] == steps for c in calls)
    cat = lambda key: [x for c in calls for x in c[key]]
    return pl.pallas_call(
        kernel_fn, grid=(steps,), in_specs=cat("in_specs"), out_specs=cat("out_specs"),
        out_shape=cat("out_shape"), scratch_shapes=cat("scratch"),
        compiler_params=pltpu.CompilerParams(dimension_semantics=("arbitrary",), vmem_limit_bytes=VMEM_LIMIT),
        name=name)(*cat("operands"))


def _ssd_prompt(*args):
    return _run(_ssd_prompt_kernel, "ssd_prompt", _ssd_prompt_call(*args))


DEC_SEQ = 4
NEW_ROWS = 8


def _branch_count(dist):
    return sum(((dist % d == 0) & (dist <= QBLK * d)).astype(np.int32) for d in DILATIONS)


def _sample_bias_tables(wb):
    t = np.arange(NEW_ROWS)[:, None]
    dist = wb + t - np.arange(wb)[None, :]
    cnt = np.where(t < DEC_SEQ, _branch_count(dist), 0)
    bkt = np.where(cnt > 0, _rel_bucket(dist), -1).astype(np.int32)
    ladd = np.log(np.maximum(cnt, 1).astype(np.float32))
    u = np.arange(LANES)[None, :]
    dnew = t - u
    cnew = np.where((t < DEC_SEQ) & (dnew > 0), _branch_count(dnew),
                    np.where((t < DEC_SEQ) & (dnew == 0), len(DILATIONS), 0))
    bkt_new = np.where(cnew > 0, _rel_bucket(dnew), -1).astype(np.int32)
    ladd_new = np.log(np.maximum(cnew, 1).astype(np.float32))
    return tuple(jnp.asarray(a) for a in (bkt, ladd, bkt_new, ladd_new))


def _attn_sample_init(tbl_ref, bkt_ref, ladd_ref, bktn_ref, laddn_ref, bias, bias_new):
    @pl.when(pl.program_id(0) == 0)
    def _build_bias():
        for h in range(ATT_HEADS):
            for src, add, dst in ((bkt_ref, ladd_ref, bias), (bktn_ref, laddn_ref, bias_new)):
                bk = src[...]
                acc = jnp.full(bk.shape, NEG_INF, F32)
                for u in range(N_BUCKETS):
                    acc = jnp.where(bk == u, tbl_ref[u, h], acc)
                dst[h // 2, (h % 2) * NEW_ROWS:(h % 2 + 1) * NEW_ROWS, :] = (acc + add[...]) * LOG2E


N_ATTN_S_IN = 8


def _attn_sample_kernel(tbl_ref, q_ref, kt_ref, vt_ref, bkt_ref, ladd_ref, bktn_ref, laddn_ref,
                        o_ref, bias, bias_new):
    _attn_sample_init(tbl_ref, bkt_ref, ladd_ref, bktn_ref, laddn_ref, bias, bias_new)
    _attn_sample_seq(q_ref, kt_ref, vt_ref, o_ref, bias, bias_new)


def _attn_sample_seq(q_ref, kt_ref, vt_ref, o_ref, bias, bias_new):
    left = lax.broadcasted_iota(jnp.int32, (1, LANES), 1) < HEAD_DIM
    nt = (((1,), (1,)), ((), ()))
    scale = HEAD_DIM ** -0.5 * LOG2E
    outs = []
    for hp in range(ATT_W // LANES):
        cols = slice(hp * LANES, (hp + 1) * LANES)
        q = q_ref[0, :, cols] * scale
        fill = jnp.zeros((LANES - NEW_ROWS, LANES), F32)
        knew = jnp.concatenate([q_ref[0, :, ATT_W + hp * LANES:ATT_W + (hp + 1) * LANES], fill], axis=0)
        vnew = jnp.concatenate([q_ref[0, :, 2 * ATT_W + hp * LANES:2 * ATT_W + (hp + 1) * LANES], fill], axis=0)
        qq = jnp.concatenate([jnp.where(left, q, 0.0), jnp.where(left, 0.0, q)], axis=0).astype(BF16)
        s = jnp.dot(qq, kt_ref[0, cols, :].astype(BF16), preferred_element_type=F32) + bias[hp]
        sn = lax.dot_general(qq, knew.astype(BF16), nt, preferred_element_type=F32) + bias_new[hp]
        m = jnp.maximum(jnp.max(s, axis=-1, keepdims=True), jnp.max(sn, axis=-1, keepdims=True))
        m = jnp.where(m == NEG_INF, 0.0, m)
        p = jnp.exp2(s - m)
        pn = jnp.exp2(sn - m)
        den = jnp.sum(p, axis=-1, keepdims=True) + jnp.sum(pn, axis=-1, keepdims=True)
        r = lax.dot_general(p.astype(BF16), vt_ref[0, cols, :].astype(BF16), nt, preferred_element_type=F32)
        r = r + jnp.dot(pn.astype(BF16), vnew.astype(BF16), preferred_element_type=F32)
        o = r / jnp.where(den == 0.0, 1.0, den)
        outs.append(jnp.where(left, o[:NEW_ROWS], o[NEW_ROWS:]))
    o_ref[0] = jnp.concatenate(outs, axis=1)[:DEC_SEQ]


def _attn_sample_call(qkv, cache_kt, cache_vt, rel_bias):
    db, _, wb = cache_kt.shape
    assert wb >= WIN_MAX and qkv.shape[0] == db * DEC_SEQ
    q8 = jnp.pad(qkv.reshape(db, DEC_SEQ, 3 * ATT_W), ((0, 0), (0, NEW_ROWS - DEC_SEQ), (0, 0)))
    consts = list(_sample_bias_tables(wb))
    hp_n = ATT_W // LANES
    seq = lambda n, m: pl.BlockSpec((1, n, m), lambda b: (b, 0, 0))
    return dict(
        steps=db,
        operands=[rel_bias, q8, cache_kt, cache_vt] + consts,
        in_specs=[pl.BlockSpec(memory_space=pltpu.SMEM), seq(NEW_ROWS, 3 * ATT_W), seq(ATT_W, wb),
                  seq(ATT_W, wb)] + [_const_spec(c.shape) for c in consts],
        out_specs=[seq(DEC_SEQ, ATT_W)],
        out_shape=[jax.ShapeDtypeStruct((db, DEC_SEQ, ATT_W), F32)],
        scratch=[pltpu.VMEM((hp_n, 2 * NEW_ROWS, wb), F32), pltpu.VMEM((hp_n, 2 * NEW_ROWS, LANES), F32)])


def _attn_sample(qkv, cache_kt, cache_vt, rel_bias):
    out, = _run(_attn_sample_kernel, "attn_sample", _attn_sample_call(qkv, cache_kt, cache_vt, rel_bias))
    return out.reshape(-1, ATT_W)


def _ssd_prompt_attn_sample_kernel(*refs):
    a = N_SSD_IN + N_ATTN_S_IN
    ssd_in = refs[:N_SSD_IN]
    tbl_ref, q_ref, kt_ref, vt_ref, bkt_ref, ladd_ref, bktn_ref, laddn_ref = refs[N_SSD_IN:a]
    y_ref, st_ref, o_ref, cbuf, state, yoff, bias, bias_new = refs[a:]
    _ssd_prompt_init(cbuf, state)
    _attn_sample_init(tbl_ref, bkt_ref, ladd_ref, bktn_ref, laddn_ref, bias, bias_new)
    _ssd_prompt_chunk(*ssd_in, y_ref, cbuf, state, yoff)
    _attn_sample_seq(q_ref, kt_ref, vt_ref, o_ref, bias, bias_new)
    _ssd_prompt_emit(st_ref, state)


def _ssd_prompt_attn_sample(ssd_args, attn_args):
    y, st, att = _run(_ssd_prompt_attn_sample_kernel, "ssd_prompt_attn_sample",
                      _ssd_prompt_call(*ssd_args), _attn_sample_call(*attn_args))
    return y, st, att.reshape(-1, ATT_W)


SEQ_ROWS = 8
SEQ_PER_STEP = 8


def _ssd_sample_kernel(z_ref, xp_ref, dt_ref, st_ref, cw_ref, cb_ref, dtb_ref, alog_ref, dsk_ref, gout_ref,
                       hexp_ref, gexp_ref, gred_ref, y_ref, sto_ref, cbuf, yoff):
    rows = SEQ_PER_STEP * SEQ_ROWS
    cbuf[:, 0:CARRY_ROW, :] = jnp.zeros((CONV_DIM // LANES, CARRY_ROW, LANES), F32)
    xc = _conv_silu(cbuf, xp_ref, cw_ref, cb_ref, rows)
    nx = SSD_W // LANES
    xs = jnp.concatenate(xc[:nx], axis=1)
    bmf = xc[nx:nx + N_GROUPS]
    bm = [b.astype(BF16) for b in bmf]
    cm = [b.astype(BF16) for b in xc[nx + N_GROUPS:]]

    li = lax.broadcasted_iota(jnp.int32, (rows, rows), 0)
    si = lax.broadcasted_iota(jnp.int32, (rows, rows), 1)
    same = (li // SEQ_ROWS) == (si // SEQ_ROWS)
    mask = same & (li >= si)
    rowi = lax.broadcasted_iota(jnp.int32, (rows, 1), 0)
    is_token = (rowi % SEQ_ROWS) >= SEQ_ROWS - DEC_SEQ

    dt = jnp.where(is_token, _softplus(dt_ref[...] + dtb_ref[...]), 0.0)
    da = dt * (-jnp.exp(alog_ref[...]) * LOG2E)
    acs = _dot_exact_lhs(mask.astype(BF16), da)
    acs_end = _dot_exact_lhs(same.astype(BF16), da)
    hexp = hexp_ref[...]
    e_in, e_end, e_all = _dot_exact_rhs(
        [(jnp.exp2(acs), 2), (jnp.exp2(acs_end - acs) * dt, 1), (jnp.exp2(acs_end), 3)], hexp)
    xw = xs * e_end

    sub = rowi % SEQ_ROWS
    pieces = [p.astype(F32) for p in _split(e_all, 3)]
    dstack = jnp.where(sub == 0, pieces[0], jnp.where(sub == 1, pieces[1], jnp.where(sub == 2, pieces[2], 0.0)))
    ones = jnp.ones((SEQ_ROWS, D_STATE), BF16)
    gmask = lax.broadcasted_iota(jnp.int32, (1, SSD_W), 1) // GROUP_W
    tn = (((0,), (0,)), ((), ()))
    nt = (((1,), (1,)), ((), ()))
    for i in range(SEQ_PER_STEP):
        rs = slice(i * SEQ_ROWS, (i + 1) * SEQ_ROWS)
        for g in range(N_GROUPS):
            lo = GROUP_LO[g]
            win = slice(lo, lo + GROUP_WIN)
            inside = gmask[:, win] == g
            yo = lax.dot_general(cm[g][rs], st_ref[i, win, :].astype(BF16), nt, preferred_element_type=F32)
            yoff[rs, win] = jnp.where(inside, yo, 0.0 if g == 0 else yoff[rs, win])
        xstack = jnp.concatenate([jnp.where(gmask == g, xw[rs, :], 0.0) for g in range(N_GROUPS)], axis=0)
        bstack = jnp.concatenate([bmf[g][rs] for g in range(N_GROUPS)], axis=0)
        upd = lax.dot_general(xstack.astype(BF16), bstack.astype(BF16), tn, preferred_element_type=F32)
        dcol = lax.dot_general(dstack[rs, :].astype(BF16), ones, tn, preferred_element_type=F32)
        sto_ref[i] = st_ref[i] * dcol + upd

    ydiag = _ssd_ydiag(xs, bm, cm, acs, dt, mask)
    y_ref[...] = _ssd_gate_norm(ydiag + yoff[...] * e_in + xs * dsk_ref[...], z_ref[...],
                                gred_ref[...], gexp_ref[...], gout_ref[...])


def _ssd_sample(z, xbc, dt, conv_state, ssm_state, conv_w, conv_b, dt_bias, a_log, dskip_x, g_out):
    db = conv_state.shape[0]
    assert db % SEQ_PER_STEP == 0 and z.shape[0] == db * DEC_SEQ
    lead = SEQ_ROWS - DEC_SEQ

    def padded(a):
        a = a.reshape(db, DEC_SEQ, a.shape[-1])
        return jnp.pad(a, ((0, 0), (lead, 0), (0, 0))).reshape(db * SEQ_ROWS, a.shape[-1])

    xp = jnp.concatenate([jnp.zeros((db, lead - (CONV_W - 1), CONV_DIM), F32), conv_state,
                          xbc.reshape(db, DEC_SEQ, CONV_DIM)], axis=1).reshape(db * SEQ_ROWS, CONV_DIM)
    hexp, gexp, gred = _ssd_consts()
    rows = SEQ_PER_STEP * SEQ_ROWS
    row = lambda n: pl.BlockSpec((rows, n), lambda i: (i, 0))
    st_spec = pl.BlockSpec((SEQ_PER_STEP, SSD_W, D_STATE), lambda i: (i, 0, 0))
    consts = [conv_w, conv_b, dt_bias, a_log, dskip_x, g_out, hexp, gexp, gred]
    y, st = pl.pallas_call(
        _ssd_sample_kernel,
        grid=(db // SEQ_PER_STEP,),
        in_specs=[row(SSD_W), row(CONV_DIM), row(LANES), st_spec] + [_const_spec(a.shape) for a in consts],
        out_specs=[row(SSD_W), st_spec],
        out_shape=[jax.ShapeDtypeStruct((db * SEQ_ROWS, SSD_W), F32),
                   jax.ShapeDtypeStruct((db, SSD_W, D_STATE), F32)],
        scratch_shapes=[pltpu.VMEM((CONV_DIM // LANES, CARRY_ROW + rows, LANES), F32),
                        pltpu.VMEM((rows, SSD_W), F32)],
        compiler_params=pltpu.CompilerParams(dimension_semantics=("parallel",),
                                             vmem_limit_bytes=VMEM_LIMIT),
        name="ssd_sample",
    )(padded(z), xp, padded(dt), ssm_state, *consts)
    y = y.reshape(db, SEQ_ROWS, SSD_W)[:, lead:].reshape(db * DEC_SEQ, SSD_W)
    return y, st


def kernel(x_prompt, x_sample, cache_k, cache_v, state_conv, state_ssm, rel_bias, g_mix, w_in, conv_w, conv_b,
           dt_bias, a_log, d_skip, g_attn_out, g_ssd_out, w_out, g_ffn, w_gate, w_up, w_down, g_final):
    depth = w_in.shape[0]
    bp, sp, _ = x_prompt.shape
    db, ds, _ = x_sample.shape
    assert bp == 1 and ds == DEC_SEQ and sp >= WIN_MAX
    xp = x_prompt.reshape(bp * sp, D_MODEL)
    xs = x_sample.reshape(db * ds, D_MODEL)
    pad_heads = lambda v: jnp.pad(v, (0, LANES - SSD_HEADS))[None]
    o_z, o_xbc, o_dt = 3 * ATT_W, 3 * ATT_W + SSD_W, 3 * ATT_W + SSD_W + CONV_DIM
    outs = [[] for _ in range(8)]
    for l in range(depth):
        wi = w_in[l].T
        wqkv, wz, wxbc = (wi[:o_z].astype(BF16), wi[o_z:o_xbc].astype(BF16), wi[o_xbc:o_dt].astype(BF16))
        wdt = jnp.pad(wi[o_dt:], ((0, LANES - SSD_HEADS), (0, 0))).astype(BF16)
        ssd_prm = (conv_w[l], conv_b[l][None], pad_heads(dt_bias[l]), pad_heads(a_log[l]),
                   jnp.repeat(d_skip[l], SSD_HEAD_DIM)[None], g_ssd_out[l][None])
        tail_prm = (g_attn_out[l][None], w_out[l][:ATT_W].astype(BF16), w_out[l][ATT_W:].astype(BF16),
                    g_ffn[l][None], w_gate[l].astype(BF16), w_up[l].astype(BF16), w_down[l].astype(BF16),
                    g_final[None])
        last = l == depth - 1

        qkv_p, z_p, xbc_p, dt_p = _inproj(xp, g_mix[l][None], wqkv, wz, wxbc, wdt)
        qkv_s, z_s, xbc_s, dt_s = _inproj(xs, g_mix[l][None], wqkv, wz, wxbc, wdt)
        att_p = _attn_prompt(qkv_p, rel_bias)
        feature_major = lambda c: jnp.transpose(c, (0, 2, 3, 1)).reshape(db, ATT_W, c.shape[1])
        ssd_args = (z_p, xbc_p, dt_p, *ssd_prm)
        attn_args = (qkv_s, feature_major(cache_k[l]), feature_major(cache_v[l]), rel_bias)
        if sp // SSD_CHUNK == db:
            ssd_p, st_p, att_s = _ssd_prompt_attn_sample(ssd_args, attn_args)
        else:
            ssd_p, st_p = _ssd_prompt(*ssd_args)
            att_s = _attn_sample(*attn_args)
        xp = _tail(xp, att_p, ssd_p, *tail_prm, final_norm=last)
        ssd_s, st_s = _ssd_sample(z_s, xbc_s, dt_s, state_conv[l], state_ssm[l].reshape(db, SSD_W, D_STATE),
                                  *ssd_prm)
        xs = _tail(xs, att_s, ssd_s, *tail_prm, final_norm=last)

        heads = lambda a: a.reshape(a.shape[0], a.shape[1], ATT_HEADS, HEAD_DIM)
        kv_p = qkv_p.reshape(bp, sp, 3 * ATT_W)[:, sp - WIN_MAX:]
        kv_s = qkv_s.reshape(db, ds, 3 * ATT_W)
        outs[0].append(heads(kv_p[..., ATT_W:2 * ATT_W]))
        outs[1].append(heads(kv_p[..., 2 * ATT_W:]))
        outs[2].append(xbc_p.reshape(bp, sp, CONV_DIM)[:, sp - (CONV_W - 1):])
        outs[3].append(st_p.reshape(bp, SSD_HEADS, SSD_HEAD_DIM, D_STATE))
        outs[4].append(heads(kv_s[..., ATT_W:2 * ATT_W]))
        outs[5].append(heads(kv_s[..., 2 * ATT_W:]))
        outs[6].append(xbc_s.reshape(db, ds, CONV_DIM)[:, ds - (CONV_W - 1):])
        outs[7].append(st_s.reshape(db, SSD_HEADS, SSD_HEAD_DIM, D_STATE))
    return (xp.reshape(bp, sp, D_MODEL), xs.reshape(db, ds, D_MODEL)) + tuple(jnp.stack(o) for o in outs)
```

```python
import functools
import math

import jax
import jax.numpy as jnp
import numpy as np
from jax import lax
from jax.experimental import pallas as pl
from jax.experimental.pallas import tpu as pltpu

F32 = jnp.float32
BF16 = jnp.bfloat16

D_MODEL = 1024
ATT_HEADS = 12
HEAD_DIM = 64
ATT_W = ATT_HEADS * HEAD_DIM
DILATIONS = (1, 4, 16)
QBLK = 128
WIN_MAX = 2048
N_BUCKETS = 32
MAX_EXACT = N_BUCKETS // 2
SSD_HEADS = 20
SSD_HEAD_DIM = 64
SSD_W = SSD_HEADS * SSD_HEAD_DIM
N_GROUPS = 4
D_STATE = 128
CONV_W = 4
CONV_DIM = SSD_W + 2 * N_GROUPS * D_STATE
SSD_CHUNK = 128
EPS = 1e-6

LANES = 128
VMEM_LIMIT = 48 * 1024 * 1024


def _rms(x, g):
    return x * lax.rsqrt(jnp.mean(x * x, axis=-1, keepdims=True) + EPS) * g


def _const_spec(shape):
    nd = len(shape)
    return pl.BlockSpec(shape, lambda *_: (0,) * nd, pipeline_mode=pl.Buffered(1))


def _inproj_kernel(x_ref, g_ref, wqkv_ref, wz_ref, wxbc_ref, wdt_ref,
                   qkv_ref, z_ref, xbc_ref, dt_ref):
    h = _rms(x_ref[...], g_ref[...]).astype(BF16)
    nt = (((1,), (1,)), ((), ()))
    qkv_ref[...] = lax.dot_general(h, wqkv_ref[...], nt, preferred_element_type=F32)
    z_ref[...] = lax.dot_general(h, wz_ref[...], nt, preferred_element_type=F32)
    xbc_ref[...] = lax.dot_general(h, wxbc_ref[...], nt, preferred_element_type=F32)
    dt_ref[...] = lax.dot_general(h, wdt_ref[...], nt, preferred_element_type=F32)


def _inproj(x, g_mix, wqkv, wz, wxbc, wdt, tm=256):
    t = x.shape[0]
    assert t % tm == 0
    row = lambda n: pl.BlockSpec((tm, n), lambda i: (i, 0))
    return pl.pallas_call(
        _inproj_kernel,
        grid=(t // tm,),
        in_specs=[row(D_MODEL), _const_spec((1, D_MODEL)), _const_spec(wqkv.shape),
                  _const_spec(wz.shape), _const_spec(wxbc.shape), _const_spec(wdt.shape)],
        out_specs=[row(3 * ATT_W), row(SSD_W), row(CONV_DIM), row(LANES)],
        out_shape=[jax.ShapeDtypeStruct((t, 3 * ATT_W), F32), jax.ShapeDtypeStruct((t, SSD_W), F32),
                   jax.ShapeDtypeStruct((t, CONV_DIM), F32), jax.ShapeDtypeStruct((t, LANES), F32)],
        compiler_params=pltpu.CompilerParams(dimension_semantics=("parallel",),
                                             vmem_limit_bytes=VMEM_LIMIT),
        name="inproj",
    )(x, g_mix, wqkv, wz, wxbc, wdt)


def _tail_kernel(x_ref, att_ref, ssd_ref, gatt_ref, woa_ref, wos_ref, gffn_ref,
                 wg_ref, wu_ref, wd_ref, gfin_ref, y_ref, *, final_norm):
    an = _rms(att_ref[...], gatt_ref[...]).astype(BF16)
    mix = jnp.dot(an, woa_ref[...], preferred_element_type=F32)
    mix = mix + jnp.dot(ssd_ref[...].astype(BF16), wos_ref[...], preferred_element_type=F32)
    x1 = x_ref[...] + mix
    h2 = _rms(x1, gffn_ref[...]).astype(BF16)
    gate = jnp.dot(h2, wg_ref[...], preferred_element_type=F32)
    up = jnp.dot(h2, wu_ref[...], preferred_element_type=F32)
    act = (gate * jax.nn.sigmoid(gate) * up).astype(BF16)
    x2 = x1 + jnp.dot(act, wd_ref[...], preferred_element_type=F32)
    y_ref[...] = _rms(x2, gfin_ref[...]) if final_norm else x2


def _tail(x, att, ssd, g_att, woa, wos, g_ffn, wg, wu, wd, g_fin, final_norm, tm=256):
    t = x.shape[0]
    assert t % tm == 0
    row = lambda n: pl.BlockSpec((tm, n), lambda i: (i, 0))
    consts = [g_att, woa, wos, g_ffn, wg, wu, wd, g_fin]
    return pl.pallas_call(
        functools.partial(_tail_kernel, final_norm=final_norm),
        grid=(t // tm,),
        in_specs=[row(D_MODEL), row(ATT_W), row(SSD_W)] + [_const_spec(c.shape) for c in consts],
        out_specs=row(D_MODEL),
        out_shape=jax.ShapeDtypeStruct((t, D_MODEL), F32),
        compiler_params=pltpu.CompilerParams(dimension_semantics=("parallel",),
                                             vmem_limit_bytes=VMEM_LIMIT),
        name="tail",
    )(x, att, ssd, *consts)


TQ = QBLK * max(DILATIONS)
NEG_INF = float("-inf")
LOG2E = math.log2(math.e)
BLOCK_UNROLL = 16


def _rel_bucket(dist):
    n = np.maximum(dist, 0)
    nf = np.maximum(n, 1).astype(np.float32)
    large = MAX_EXACT + (np.log(nf / MAX_EXACT) / math.log(WIN_MAX / MAX_EXACT)
                         * (N_BUCKETS - MAX_EXACT)).astype(np.int32)
    return np.where(n < MAX_EXACT, n, np.minimum(large, N_BUCKETS - 1))


def _band_buckets():
    qi = np.arange(QBLK)[:, None] + QBLK
    kk = np.arange(2 * QBLK)[None, :]
    dist = qi - kk
    valid = (dist >= 0) & (dist <= QBLK)
    return jnp.asarray(np.stack([np.where(valid, _rel_bucket(dist * d), -1) for d in DILATIONS]).astype(np.int32))


STAGE = 4
assert DILATIONS == (1, STAGE, STAGE * STAGE)


def _regroup(src_ref, tmp, dsts, scale=None):
    cast = (lambda x: x.astype(BF16)) if scale is None else (lambda x: (x * scale).astype(BF16))
    sub = TQ // STAGE
    for c in range(STAGE):
        dsts[0][c * sub:(c + 1) * sub, :] = cast(src_ref[c * sub:(c + 1) * sub, :])
    for r in range(STAGE):
        x = src_ref[pl.ds(r, sub, stride=STAGE), :]
        tmp[r] = x
        dsts[1][r * sub:(r + 1) * sub, :] = cast(x)
    for r in range(STAGE):
        for r2 in range(STAGE):
            res = r + STAGE * r2
            dsts[2][res * QBLK:(res + 1) * QBLK, :] = cast(tmp[r, pl.ds(r2, QBLK, stride=STAGE), :])


def _attn_prompt_kernel(tbl_ref, q_ref, k_ref, v_ref, bkt_ref, o_ref,
                        qd, kd, vd, tmp, bias, macc, nacc, dacc):
    hp = pl.program_id(0)
    t = pl.program_id(1)
    nb = len(DILATIONS)
    slot = t % 2

    @pl.when(t == 0)
    def _no_previous_tile():
        kd[:, 1] = jnp.zeros((nb, TQ, LANES), BF16)
        vd[:, 1] = jnp.zeros((nb, TQ, LANES), BF16)

    @pl.when(t == 0)
    def _build_bias():
        prev_half = lax.broadcasted_iota(jnp.int32, (1, 2 * QBLK), 1) < QBLK
        for b in range(len(DILATIONS)):
            bk = bkt_ref[b]
            for hh in range(2):
                acc = jnp.full((QBLK, 2 * QBLK), NEG_INF, F32)
                for u in range(N_BUCKETS):
                    acc = jnp.where(bk == u, tbl_ref[u, 2 * hp + hh], acc)
                acc = acc * LOG2E
                bias[b, 0, hh * QBLK:(hh + 1) * QBLK] = acc
                bias[b, 1, hh * QBLK:(hh + 1) * QBLK] = jnp.where(prev_half, NEG_INF, acc)

    _regroup(q_ref, tmp, [qd.at[b] for b in range(nb)], HEAD_DIM ** -0.5 * LOG2E)
    _regroup(k_ref, tmp, [kd.at[b, slot] for b in range(nb)])
    _regroup(v_ref, tmp, [vd.at[b, slot] for b in range(nb)])

    left = lax.broadcasted_iota(jnp.int32, (1, LANES), 1) < HEAD_DIM
    nt = (((1,), (1,)), ((), ()))

    for b, d in enumerate(DILATIONS):
        nbr = max(DILATIONS) // d

        def rows(start, size, d=d):
            return pl.ds(start, size) if d == 1 else pl.ds(start, size, stride=d)

        def block(j, carry, b=b, d=d, nbr=nbr, rows=rows):
            r = j // nbr
            m = j % nbr
            qs = r + d * QBLK * m
            gs = pl.multiple_of((r * nbr + m) * QBLK, QBLK)
            wrap = m == 0
            ps = pl.multiple_of((r * nbr + jnp.where(wrap, nbr - 1, m - 1)) * QBLK, QBLK)
            pslot = jnp.where(wrap, 1 - slot, slot)
            q = qd[b, pl.ds(gs, QBLK), :]
            kc = jnp.concatenate([kd[b, pslot, pl.ds(ps, QBLK), :], kd[b, slot, pl.ds(gs, QBLK), :]], axis=0)
            vc = jnp.concatenate([vd[b, pslot, pl.ds(ps, QBLK), :], vd[b, slot, pl.ds(gs, QBLK), :]], axis=0)
            first = jnp.logical_and(wrap, t == 0).astype(jnp.int32)
            zero = jnp.zeros_like(q)
            qq = jnp.concatenate([jnp.where(left, q, zero), jnp.where(left, zero, q)], axis=0)
            s = lax.dot_general(qq, kc, nt, preferred_element_type=F32) + bias[b, first]
            mx = jnp.max(s, axis=-1, keepdims=True)
            e = jnp.exp2(s - mx)
            den = jnp.sum(e, axis=-1, keepdims=True)
            res = jnp.dot(e.astype(BF16), vc, preferred_element_type=F32)
            nacc[b, rows(qs, QBLK), :] = jnp.where(left, res[:QBLK], res[QBLK:])
            dacc[b, rows(qs, QBLK), :] = jnp.where(left, den[:QBLK], den[QBLK:])
            macc[b, rows(qs, QBLK), :] = jnp.where(left, mx[:QBLK], mx[QBLK:])
            return carry

        lax.fori_loop(0, max(DILATIONS), block, 0, unroll=BLOCK_UNROLL)

    def merge(c, carry):
        sl = pl.ds(pl.multiple_of(c * QBLK, QBLK), QBLK)
        ms = [macc[b, sl, :] for b in range(len(DILATIONS))]
        top = jnp.maximum(jnp.maximum(ms[0], ms[1]), ms[2])
        ws = [jnp.exp2(mb - top) for mb in ms]
        num = sum(w * nacc[b, sl, :] for b, w in enumerate(ws))
        den = sum(w * dacc[b, sl, :] for b, w in enumerate(ws))
        o_ref[sl, :] = num / den
        return carry

    lax.fori_loop(0, TQ // QBLK, merge, 0)


def _attn_prompt(qkv, rel_bias):
    s = qkv.shape[0]
    assert s % TQ == 0
    nb = len(DILATIONS)
    hp_n = ATT_W // LANES
    tile = lambda f: pl.BlockSpec((TQ, LANES), f)
    return pl.pallas_call(
        _attn_prompt_kernel,
        grid=(hp_n, s // TQ),
        in_specs=[pl.BlockSpec(memory_space=pltpu.SMEM),
                  tile(lambda h, t: (t, h)),
                  tile(lambda h, t: (t, hp_n + h)),
                  tile(lambda h, t: (t, 2 * hp_n + h)),
                  _const_spec((nb, QBLK, 2 * QBLK))],
        out_specs=tile(lambda h, t: (t, h)),
        out_shape=jax.ShapeDtypeStruct((s, ATT_W), F32),
        scratch_shapes=[pltpu.VMEM((nb, TQ, LANES), BF16), pltpu.VMEM((nb, 2, TQ, LANES), BF16),
                        pltpu.VMEM((nb, 2, TQ, LANES), BF16), pltpu.VMEM((STAGE, TQ // STAGE, LANES), F32),
                        pltpu.VMEM((nb, 2, 2 * QBLK, 2 * QBLK), F32),
                        pltpu.VMEM((nb, TQ, LANES), F32), pltpu.VMEM((nb, TQ, LANES), F32),
                        pltpu.VMEM((nb, TQ, LANES), F32)],
        compiler_params=pltpu.CompilerParams(dimension_semantics=("parallel", "arbitrary"),
                                             vmem_limit_bytes=VMEM_LIMIT),
        name="attn_prompt",
    )(rel_bias, qkv, qkv, qkv, _band_buckets())


HPG = SSD_HEADS // N_GROUPS
GROUP_W = SSD_W // N_GROUPS
GROUP_WIN = 3 * LANES
GROUP_LO = tuple((g * GROUP_W) // LANES * LANES for g in range(N_GROUPS))
CARRY_ROW = 8


def _split(x, terms):
    parts = []
    for _ in range(terms):
        p = x.astype(BF16)
        parts.append(p)
        x = x - p.astype(F32)
    return parts


def _dot_exact_rhs(xs_terms, m):
    pieces = [p for x, terms in xs_terms for p in _split(x, terms)]
    out = jnp.dot(jnp.concatenate(pieces, axis=0) if len(pieces) > 1 else pieces[0], m,
                  preferred_element_type=F32)
    res, r = [], 0
    for x, terms in xs_terms:
        n = x.shape[0]
        res.append(sum(out[r + i * n:r + (i + 1) * n] for i in range(terms)))
        r += terms * n
    return res


def _dot_exact_lhs(m, x, terms=3):
    n = x.shape[1]
    out = jnp.dot(m, jnp.concatenate(_split(x, terms), axis=1), preferred_element_type=F32)
    return sum(out[:, i * n:(i + 1) * n] for i in range(terms))


def _softplus(x):
    return jnp.maximum(x, 0.0) + jnp.log1p(jnp.exp(-jnp.abs(x)))


def _silu(x):
    h = 0.5 * x
    return h + h * jnp.tanh(h)


def _ssd_consts():
    c = jnp.arange(SSD_W)
    head_of = (c // SSD_HEAD_DIM)[None, :] == jnp.arange(LANES)[:, None]
    group_of = (c // GROUP_W)[None, :] == jnp.arange(LANES)[:, None]
    return head_of.astype(BF16), group_of.astype(BF16), group_of.T.astype(BF16)


def _conv_silu(cbuf, xbc_ref, w_ref, b_ref, rows):
    out = []
    for c in range(CONV_DIM // LANES):
        cols = slice(c * LANES, (c + 1) * LANES)
        cbuf[c, CARRY_ROW:CARRY_ROW + rows, :] = xbc_ref[:, cols]
        acc = b_ref[:, cols]
        for j in range(CONV_W):
            acc = acc + cbuf[c, pl.ds(CARRY_ROW - (CONV_W - 1) + j, rows), :] * w_ref[j:j + 1, cols]
        out.append(_silu(acc))
    return out


def _drain(steps):
    try:
        while True:
            next(steps)
    except StopIteration as done:
        return done.value


def _interleave(*bodies):
    live = list(bodies)
    while live:
        for body in list(live):
            try:
                next(body)
            except StopIteration:
                live.remove(body)


def _ssd_ydiag(*args):
    return _drain(_ssd_ydiag_steps(*args))


def _ssd_ydiag_steps(xs, bm, cm, acs, dt, mask):
    cbs = [lax.dot_general(cm[g], bm[g], (((1,), (1,)), ((), ())), preferred_element_type=F32)
           for g in range(N_GROUPS)]
    acs_t = acs.T
    dt_t = dt.T
    left = lax.broadcasted_iota(jnp.int32, (1, LANES), 1) < SSD_HEAD_DIM
    pairs = []
    for pair in range(SSD_HEADS // 2):
        xpair = xs[:, pair * LANES:(pair + 1) * LANES]
        acc = None
        for hh in range(2):
            h = 2 * pair + hh
            seg = acs[:, h:h + 1] - acs_t[h:h + 1, :]
            dec = jnp.exp2(jnp.where(mask, seg, NEG_INF))
            w = (cbs[h // HPG] * dec * dt_t[h:h + 1, :]).astype(BF16)
            xh = jnp.where(left if hh == 0 else jnp.logical_not(left), xpair, 0.0).astype(BF16)
            part = jnp.dot(w, xh, preferred_element_type=F32)
            acc = part if acc is None else acc + part
        pairs.append(acc)
        if pair % 2 == 1:
            yield
    return jnp.concatenate(pairs, axis=1)


def _ssd_gate_norm(y, gate, gred, gexp, gout):
    y = y * gate
    ms = _dot_exact_rhs([(y * y, 2)], gred)[0] * (1.0 / GROUP_W)
    return y * _dot_exact_rhs([(lax.rsqrt(ms + EPS), 2)], gexp)[0] * gout


def _ssd_prompt_init(cbuf, state):
    @pl.when(pl.program_id(0) == 0)
    def _init():
        state[...] = jnp.zeros_like(state)
        cbuf[:, 0:CARRY_ROW, :] = jnp.zeros((CONV_DIM // LANES, CARRY_ROW, LANES), F32)


def _ssd_prompt_emit(st_ref, state):
    @pl.when(pl.program_id(0) == pl.num_programs(0) - 1)
    def _emit_state():
        for i in range(SSD_W // LANES):
            st_ref[i * LANES:(i + 1) * LANES, :] = state[:, i * LANES:(i + 1) * LANES].T


def _ssd_prompt_chunk(z_ref, xbc_ref, dt_ref, cw_ref, cb_ref, dtb_ref, alog_ref, dsk_ref, gout_ref,
                      hexp_ref, gexp_ref, gred_ref, y_ref, cbuf, state, yoff):
    cl = SSD_CHUNK
    xc = _conv_silu(cbuf, xbc_ref, cw_ref, cb_ref, cl)
    cbuf[:, CARRY_ROW - (CONV_W - 1):CARRY_ROW, :] = cbuf[:, CARRY_ROW + cl - (CONV_W - 1):CARRY_ROW + cl, :]
    nx = SSD_W // LANES
    xs = jnp.concatenate(xc[:nx], axis=1)
    bm = [b.astype(BF16) for b in xc[nx:nx + N_GROUPS]]
    cm = [b.astype(BF16) for b in xc[nx + N_GROUPS:]]
    yield

    li = lax.broadcasted_iota(jnp.int32, (cl, cl), 0)
    si = lax.broadcasted_iota(jnp.int32, (cl, cl), 1)
    causal = li >= si

    dt = _softplus(dt_ref[...] + dtb_ref[...])
    acs = _dot_exact_lhs(causal.astype(BF16), dt * (-jnp.exp(alog_ref[...]) * LOG2E))
    hexp = hexp_ref[...]
    e_in, e_end = _dot_exact_rhs([(jnp.exp2(acs), 3), (jnp.exp2(acs[cl - 1:cl, :] - acs) * dt, 1)], hexp)
    xw = (xs * e_end).astype(BF16)
    chunk_decay = e_in[cl - 1:cl]
    yield

    gmask = lax.broadcasted_iota(jnp.int32, (1, SSD_W), 1) // GROUP_W
    for g in range(N_GROUPS):
        lo = GROUP_LO[g]
        win = slice(lo, lo + GROUP_WIN)
        inside = gmask[:, win] == g
        yo = jnp.dot(cm[g], state[:, win].astype(BF16), preferred_element_type=F32)
        yoff[:, win] = jnp.where(inside, yo, 0.0 if g == 0 else yoff[:, win])
    yield
    for g in range(N_GROUPS):
        lo = GROUP_LO[g]
        win = slice(lo, lo + GROUP_WIN)
        inside = gmask[:, win] == g
        upd = lax.dot_general(bm[g], xw[:, win], (((0,), (0,)), ((), ())), preferred_element_type=F32)
        old = state[:, win]
        state[:, win] = jnp.where(inside, old * chunk_decay[:, win] + upd, old)
    yield

    ydiag = yield from _ssd_ydiag_steps(xs, bm, cm, acs, dt, causal)
    y_ref[...] = _ssd_gate_norm(ydiag + yoff[...] * e_in + xs * dsk_ref[...], _silu(z_ref[...]),
                                gred_ref[...], gexp_ref[...], gout_ref[...])


N_SSD_IN = 12


def _ssd_prompt_kernel(*refs):
    ins, (y_ref, st_ref, cbuf, state, yoff) = refs[:N_SSD_IN], refs[N_SSD_IN:]
    _ssd_prompt_init(cbuf, state)
    _drain(_ssd_prompt_chunk(*ins, y_ref, cbuf, state, yoff))
    _ssd_prompt_emit(st_ref, state)


def _ssd_prompt_call(z, xbc, dt, conv_w, conv_b, dt_bias, a_log, dskip_x, g_out):
    s = z.shape[0]
    assert s % SSD_CHUNK == 0
    hexp, gexp, gred = _ssd_consts()
    row = lambda n: pl.BlockSpec((SSD_CHUNK, n), lambda i: (i, 0))
    consts = [conv_w, conv_b, dt_bias, a_log, dskip_x, g_out, hexp, gexp, gred]
    return dict(
        steps=s // SSD_CHUNK,
        operands=[z, xbc, dt] + consts,
        in_specs=[row(SSD_W), row(CONV_DIM), row(LANES)] + [_const_spec(a.shape) for a in consts],
        out_specs=[row(SSD_W), pl.BlockSpec((SSD_W, D_STATE), lambda i: (0, 0))],
        out_shape=[jax.ShapeDtypeStruct((s, SSD_W), F32), jax.ShapeDtypeStruct((SSD_W, D_STATE), F32)],
        scratch=[pltpu.VMEM((CONV_DIM // LANES, CARRY_ROW + SSD_CHUNK, LANES), F32),
                 pltpu.VMEM((D_STATE, SSD_W), F32), pltpu.VMEM((SSD_CHUNK, SSD_W), F32)])


def _run(kernel_fn, name, *calls, vmem_limit=VMEM_LIMIT):
    steps = calls[0]["steps"]
    assert all(c["steps"] == steps for c in calls)
    cat = lambda key: [x for c in calls for x in c[key]]
    return pl.pallas_call(
        kernel_fn, grid=(steps,), in_specs=cat("in_specs"), out_specs=cat("out_specs"),
        out_shape=cat("out_shape"), scratch_shapes=cat("scratch"),
        compiler_params=pltpu.CompilerParams(dimension_semantics=("arbitrary",), vmem_limit_bytes=vmem_limit),
        name=name)(*cat("operands"))


def _ssd_prompt(*args):
    return _run(_ssd_prompt_kernel, "ssd_prompt", _ssd_prompt_call(*args))


DEC_SEQ = 4
NEW_ROWS = 8


def _branch_count(dist):
    return sum(((dist % d == 0) & (dist <= QBLK * d)).astype(np.int32) for d in DILATIONS)


def _sample_bias_tables(wb):
    t = np.arange(NEW_ROWS)[:, None]
    dist = wb + t - np.arange(wb)[None, :]
    cnt = np.where(t < DEC_SEQ, _branch_count(dist), 0)
    bkt = np.where(cnt > 0, _rel_bucket(dist), -1).astype(np.int32)
    ladd = np.log(np.maximum(cnt, 1).astype(np.float32))
    u = np.arange(LANES)[None, :]
    dnew = t - u
    cnew = np.where((t < DEC_SEQ) & (dnew > 0), _branch_count(dnew),
                    np.where((t < DEC_SEQ) & (dnew == 0), len(DILATIONS), 0))
    bkt_new = np.where(cnew > 0, _rel_bucket(dnew), -1).astype(np.int32)
    ladd_new = np.log(np.maximum(cnew, 1).astype(np.float32))
    return tuple(jnp.asarray(a) for a in (bkt, ladd, bkt_new, ladd_new))


def _attn_sample_init(tbl_ref, bkt_ref, ladd_ref, bktn_ref, laddn_ref, bias, bias_new):
    @pl.when(pl.program_id(0) == 0)
    def _build_bias():
        for h in range(ATT_HEADS):
            for src, add, dst in ((bkt_ref, ladd_ref, bias), (bktn_ref, laddn_ref, bias_new)):
                bk = src[...]
                acc = jnp.full(bk.shape, NEG_INF, F32)
                for u in range(N_BUCKETS):
                    acc = jnp.where(bk == u, tbl_ref[u, h], acc)
                dst[h // 2, (h % 2) * NEW_ROWS:(h % 2 + 1) * NEW_ROWS, :] = (acc + add[...]) * LOG2E


N_ATTN_S_IN = 8


def _attn_sample_kernel(tbl_ref, q_ref, kt_ref, vt_ref, bkt_ref, ladd_ref, bktn_ref, laddn_ref,
                        o_ref, bias, bias_new):
    _attn_sample_init(tbl_ref, bkt_ref, ladd_ref, bktn_ref, laddn_ref, bias, bias_new)
    whole = lambda hp: (kt_ref.at[0], vt_ref.at[0], hp * LANES)
    _drain(_attn_sample_seq(q_ref, whole, o_ref, bias, bias_new))


RING = 3
RING_VMEM_LIMIT = 58 * 1024 * 1024


def _ring_kv(k_hbm, v_hbm, kring, vring, sems):
    i = pl.program_id(0)
    n = pl.num_programs(0)

    def copies(seq, slot):
        return (pltpu.make_async_copy(k_hbm.at[seq], kring.at[slot], sems.at[0, slot]),
                pltpu.make_async_copy(v_hbm.at[seq], vring.at[slot], sems.at[1, slot]))

    def start(seq, slot):
        for c in copies(jnp.minimum(seq, n - 1), slot):
            c.start()

    @pl.when(i == 0)
    def _prime():
        for s in range(RING - 1):
            start(s, s)

    start(i + RING - 1, (i + RING - 1) % RING)
    slot = i % RING
    for c in copies(i, slot):
        c.wait()

    def finish():
        @pl.when(i == n - 1)
        def _drain_tail():
            for s in range(RING - 1):
                for c in copies(0, (n + s) % RING):
                    c.wait()

    return (lambda hp: (kring.at[slot], vring.at[slot], hp * LANES)), finish


def _attn_sample_seq(q_ref, kv_for_pair, o_ref, bias, bias_new):
    left = lax.broadcasted_iota(jnp.int32, (1, LANES), 1) < HEAD_DIM
    nt = (((1,), (1,)), ((), ()))
    scale = HEAD_DIM ** -0.5 * LOG2E
    outs = []
    for hp in range(ATT_W // LANES):
        cols = slice(hp * LANES, (hp + 1) * LANES)
        k_ref, v_ref, row0 = kv_for_pair(hp)
        kv_rows = slice(row0, row0 + LANES)
        q = q_ref[0, :, cols] * scale
        fill = jnp.zeros((LANES - NEW_ROWS, LANES), F32)
        knew = jnp.concatenate([q_ref[0, :, ATT_W + hp * LANES:ATT_W + (hp + 1) * LANES], fill], axis=0)
        vnew = jnp.concatenate([q_ref[0, :, 2 * ATT_W + hp * LANES:2 * ATT_W + (hp + 1) * LANES], fill], axis=0)
        qq = jnp.concatenate([jnp.where(left, q, 0.0), jnp.where(left, 0.0, q)], axis=0).astype(BF16)
        s = jnp.dot(qq, k_ref[kv_rows, :].astype(BF16), preferred_element_type=F32) + bias[hp]
        sn = lax.dot_general(qq, knew.astype(BF16), nt, preferred_element_type=F32) + bias_new[hp]
        m = jnp.maximum(jnp.max(s, axis=-1, keepdims=True), jnp.max(sn, axis=-1, keepdims=True))
        m = jnp.where(m == NEG_INF, 0.0, m)
        p = jnp.exp2(s - m)
        pn = jnp.exp2(sn - m)
        den = jnp.sum(p, axis=-1, keepdims=True) + jnp.sum(pn, axis=-1, keepdims=True)
        r = lax.dot_general(p.astype(BF16), v_ref[kv_rows, :].astype(BF16), nt, preferred_element_type=F32)
        r = r + jnp.dot(pn.astype(BF16), vnew.astype(BF16), preferred_element_type=F32)
        o = r / jnp.where(den == 0.0, 1.0, den)
        outs.append(jnp.where(left, o[:NEW_ROWS], o[NEW_ROWS:]))
        yield
    o_ref[0] = jnp.concatenate(outs, axis=1)[:DEC_SEQ]


def _attn_sample_call(qkv, cache_kt, cache_vt, rel_bias, ring=False):
    db, _, wb = cache_kt.shape
    assert wb >= WIN_MAX and qkv.shape[0] == db * DEC_SEQ
    q8 = jnp.pad(qkv.reshape(db, DEC_SEQ, 3 * ATT_W), ((0, 0), (0, NEW_ROWS - DEC_SEQ), (0, 0)))
    consts = list(_sample_bias_tables(wb))
    hp_n = ATT_W // LANES
    seq = lambda n, m: pl.BlockSpec((1, n, m), lambda b: (b, 0, 0))
    kv_spec = pl.BlockSpec(memory_space=pl.ANY) if ring else seq(ATT_W, wb)
    scratch = [pltpu.VMEM((hp_n, 2 * NEW_ROWS, wb), F32), pltpu.VMEM((hp_n, 2 * NEW_ROWS, LANES), F32)]
    if ring:
        scratch += [pltpu.VMEM((RING, ATT_W, wb), F32), pltpu.VMEM((RING, ATT_W, wb), F32),
                    pltpu.SemaphoreType.DMA((2, RING))]
    return dict(
        steps=db,
        operands=[rel_bias, q8, cache_kt, cache_vt] + consts,
        in_specs=[pl.BlockSpec(memory_space=pltpu.SMEM), seq(NEW_ROWS, 3 * ATT_W), kv_spec, kv_spec]
                 + [_const_spec(c.shape) for c in consts],
        out_specs=[seq(DEC_SEQ, ATT_W)],
        out_shape=[jax.ShapeDtypeStruct((db, DEC_SEQ, ATT_W), F32)],
        scratch=scratch)


def _attn_sample(qkv, cache_kt, cache_vt, rel_bias):
    out, = _run(_attn_sample_kernel, "attn_sample", _attn_sample_call(qkv, cache_kt, cache_vt, rel_bias))
    return out.reshape(-1, ATT_W)


def _ssd_prompt_attn_sample_kernel(*refs):
    a = N_SSD_IN + N_ATTN_S_IN
    ssd_in = refs[:N_SSD_IN]
    tbl_ref, q_ref, kt_ref, vt_ref, bkt_ref, ladd_ref, bktn_ref, laddn_ref = refs[N_SSD_IN:a]
    y_ref, st_ref, o_ref, cbuf, state, yoff, bias, bias_new, kring, vring, sems = refs[a:]
    _ssd_prompt_init(cbuf, state)
    _attn_sample_init(tbl_ref, bkt_ref, ladd_ref, bktn_ref, laddn_ref, bias, bias_new)
    kv_for_pair, finish_ring = _ring_kv(kt_ref, vt_ref, kring, vring, sems)
    _interleave(_ssd_prompt_chunk(*ssd_in, y_ref, cbuf, state, yoff),
                _attn_sample_seq(q_ref, kv_for_pair, o_ref, bias, bias_new))
    finish_ring()
    _ssd_prompt_emit(st_ref, state)


def _ssd_prompt_attn_sample(ssd_args, attn_args):
    y, st, att = _run(_ssd_prompt_attn_sample_kernel, "ssd_prompt_attn_sample",
                      _ssd_prompt_call(*ssd_args), _attn_sample_call(*attn_args, ring=True),
                      vmem_limit=RING_VMEM_LIMIT)
    return y, st, att.reshape(-1, ATT_W)


SEQ_ROWS = 8
SEQ_PER_STEP = 8


def _ssd_sample_kernel(z_ref, xp_ref, dt_ref, st_ref, cw_ref, cb_ref, dtb_ref, alog_ref, dsk_ref, gout_ref,
                       hexp_ref, gexp_ref, gred_ref, y_ref, sto_ref, cbuf, yoff):
    rows = SEQ_PER_STEP * SEQ_ROWS
    cbuf[:, 0:CARRY_ROW, :] = jnp.zeros((CONV_DIM // LANES, CARRY_ROW, LANES), F32)
    xc = _conv_silu(cbuf, xp_ref, cw_ref, cb_ref, rows)
    nx = SSD_W // LANES
    xs = jnp.concatenate(xc[:nx], axis=1)
    bmf = xc[nx:nx + N_GROUPS]
    bm = [b.astype(BF16) for b in bmf]
    cm = [b.astype(BF16) for b in xc[nx + N_GROUPS:]]

    li = lax.broadcasted_iota(jnp.int32, (rows, rows), 0)
    si = lax.broadcasted_iota(jnp.int32, (rows, rows), 1)
    same = (li // SEQ_ROWS) == (si // SEQ_ROWS)
    mask = same & (li >= si)
    rowi = lax.broadcasted_iota(jnp.int32, (rows, 1), 0)
    is_token = (rowi % SEQ_ROWS) >= SEQ_ROWS - DEC_SEQ

    dt = jnp.where(is_token, _softplus(dt_ref[...] + dtb_ref[...]), 0.0)
    da = dt * (-jnp.exp(alog_ref[...]) * LOG2E)
    acs = _dot_exact_lhs(mask.astype(BF16), da)
    acs_end = _dot_exact_lhs(same.astype(BF16), da)
    hexp = hexp_ref[...]
    e_in, e_end, e_all = _dot_exact_rhs(
        [(jnp.exp2(acs), 2), (jnp.exp2(acs_end - acs) * dt, 1), (jnp.exp2(acs_end), 3)], hexp)
    xw = xs * e_end

    sub = rowi % SEQ_ROWS
    pieces = [p.astype(F32) for p in _split(e_all, 3)]
    dstack = jnp.where(sub == 0, pieces[0], jnp.where(sub == 1, pieces[1], jnp.where(sub == 2, pieces[2], 0.0)))
    ones = jnp.ones((SEQ_ROWS, D_STATE), BF16)
    gmask = lax.broadcasted_iota(jnp.int32, (1, SSD_W), 1) // GROUP_W
    tn = (((0,), (0,)), ((), ()))
    nt = (((1,), (1,)), ((), ()))
    for i in range(SEQ_PER_STEP):
        rs = slice(i * SEQ_ROWS, (i + 1) * SEQ_ROWS)
        for g in range(N_GROUPS):
            lo = GROUP_LO[g]
            win = slice(lo, lo + GROUP_WIN)
            inside = gmask[:, win] == g
            yo = lax.dot_general(cm[g][rs], st_ref[i, win, :].astype(BF16), nt, preferred_element_type=F32)
            yoff[rs, win] = jnp.where(inside, yo, 0.0 if g == 0 else yoff[rs, win])
        xstack = jnp.concatenate([jnp.where(gmask == g, xw[rs, :], 0.0) for g in range(N_GROUPS)], axis=0)
        bstack = jnp.concatenate([bmf[g][rs] for g in range(N_GROUPS)], axis=0)
        upd = lax.dot_general(xstack.astype(BF16), bstack.astype(BF16), tn, preferred_element_type=F32)
        dcol = lax.dot_general(dstack[rs, :].astype(BF16), ones, tn, preferred_element_type=F32)
        sto_ref[i] = st_ref[i] * dcol + upd

    ydiag = _ssd_ydiag(xs, bm, cm, acs, dt, mask)
    y_ref[...] = _ssd_gate_norm(ydiag + yoff[...] * e_in + xs * dsk_ref[...], _silu(z_ref[...]),
                                gred_ref[...], gexp_ref[...], gout_ref[...])


def _ssd_sample(z, xbc, dt, conv_state, ssm_state, conv_w, conv_b, dt_bias, a_log, dskip_x, g_out):
    db = conv_state.shape[0]
    assert db % SEQ_PER_STEP == 0 and z.shape[0] == db * DEC_SEQ
    lead = SEQ_ROWS - DEC_SEQ

    def padded(a):
        a = a.reshape(db, DEC_SEQ, a.shape[-1])
        return jnp.pad(a, ((0, 0), (lead, 0), (0, 0))).reshape(db * SEQ_ROWS, a.shape[-1])

    xp = jnp.concatenate([jnp.zeros((db, lead - (CONV_W - 1), CONV_DIM), F32), conv_state,
                          xbc.reshape(db, DEC_SEQ, CONV_DIM)], axis=1).reshape(db * SEQ_ROWS, CONV_DIM)
    hexp, gexp, gred = _ssd_consts()
    rows = SEQ_PER_STEP * SEQ_ROWS
    row = lambda n: pl.BlockSpec((rows, n), lambda i: (i, 0))
    st_spec = pl.BlockSpec((SEQ_PER_STEP, SSD_W, D_STATE), lambda i: (i, 0, 0))
    consts = [conv_w, conv_b, dt_bias, a_log, dskip_x, g_out, hexp, gexp, gred]
    y, st = pl.pallas_call(
        _ssd_sample_kernel,
        grid=(db // SEQ_PER_STEP,),
        in_specs=[row(SSD_W), row(CONV_DIM), row(LANES), st_spec] + [_const_spec(a.shape) for a in consts],
        out_specs=[row(SSD_W), st_spec],
        out_shape=[jax.ShapeDtypeStruct((db * SEQ_ROWS, SSD_W), F32),
                   jax.ShapeDtypeStruct((db, SSD_W, D_STATE), F32)],
        scratch_shapes=[pltpu.VMEM((CONV_DIM // LANES, CARRY_ROW + rows, LANES), F32),
                        pltpu.VMEM((rows, SSD_W), F32)],
        compiler_params=pltpu.CompilerParams(dimension_semantics=("parallel",),
                                             vmem_limit_bytes=VMEM_LIMIT),
        name="ssd_sample",
    )(padded(z), xp, padded(dt), ssm_state, *consts)
    y = y.reshape(db, SEQ_ROWS, SSD_W)[:, lead:].reshape(db * DEC_SEQ, SSD_W)
    return y, st


def kernel(x_prompt, x_sample, cache_k, cache_v, state_conv, state_ssm, rel_bias, g_mix, w_in, conv_w, conv_b,
           dt_bias, a_log, d_skip, g_attn_out, g_ssd_out, w_out, g_ffn, w_gate, w_up, w_down, g_final):
    depth = w_in.shape[0]
    bp, sp, _ = x_prompt.shape
    db, ds, _ = x_sample.shape
    assert bp == 1 and ds == DEC_SEQ and sp >= WIN_MAX
    xp = x_prompt.reshape(bp * sp, D_MODEL)
    xs = x_sample.reshape(db * ds, D_MODEL)
    pad_heads = lambda v: jnp.pad(v, (0, LANES - SSD_HEADS))[None]
    o_z, o_xbc, o_dt = 3 * ATT_W, 3 * ATT_W + SSD_W, 3 * ATT_W + SSD_W + CONV_DIM
    outs = [[] for _ in range(8)]
    for l in range(depth):
        wi = w_in[l].T
        wqkv, wz, wxbc = (wi[:o_z].astype(BF16), wi[o_z:o_xbc].astype(BF16), wi[o_xbc:o_dt].astype(BF16))
        wdt = jnp.pad(wi[o_dt:], ((0, LANES - SSD_HEADS), (0, 0))).astype(BF16)
        ssd_prm = (conv_w[l], conv_b[l][None], pad_heads(dt_bias[l]), pad_heads(a_log[l]),
                   jnp.repeat(d_skip[l], SSD_HEAD_DIM)[None], g_ssd_out[l][None])
        tail_prm = (g_attn_out[l][None], w_out[l][:ATT_W].astype(BF16), w_out[l][ATT_W:].astype(BF16),
                    g_ffn[l][None], w_gate[l].astype(BF16), w_up[l].astype(BF16), w_down[l].astype(BF16),
                    g_final[None])
        last = l == depth - 1

        qkv_p, z_p, xbc_p, dt_p = _inproj(xp, g_mix[l][None], wqkv, wz, wxbc, wdt)
        qkv_s, z_s, xbc_s, dt_s = _inproj(xs, g_mix[l][None], wqkv, wz, wxbc, wdt)
        att_p = _attn_prompt(qkv_p, rel_bias)
        feature_major = lambda c: jnp.transpose(c, (0, 2, 3, 1)).reshape(db, ATT_W, c.shape[1])
        ssd_args = (z_p, xbc_p, dt_p, *ssd_prm)
        attn_args = (qkv_s, feature_major(cache_k[l]), feature_major(cache_v[l]), rel_bias)
        if sp // SSD_CHUNK == db:
            ssd_p, st_p, att_s = _ssd_prompt_attn_sample(ssd_args, attn_args)
        else:
            ssd_p, st_p = _ssd_prompt(*ssd_args)
            att_s = _attn_sample(*attn_args)
        xp = _tail(xp, att_p, ssd_p, *tail_prm, final_norm=last)
        ssd_s, st_s = _ssd_sample(z_s, xbc_s, dt_s, state_conv[l], state_ssm[l].reshape(db, SSD_W, D_STATE),
                                  *ssd_prm)
        xs = _tail(xs, att_s, ssd_s, *tail_prm, final_norm=last)

        heads = lambda a: a.reshape(a.shape[0], a.shape[1], ATT_HEADS, HEAD_DIM)
        kv_p = qkv_p.reshape(bp, sp, 3 * ATT_W)[:, sp - WIN_MAX:]
        kv_s = qkv_s.reshape(db, ds, 3 * ATT_W)
        outs[0].append(heads(kv_p[..., ATT_W:2 * ATT_W]))
        outs[1].append(heads(kv_p[..., 2 * ATT_W:]))
        outs[2].append(xbc_p.reshape(bp, sp, CONV_DIM)[:, sp - (CONV_W - 1):])
        outs[3].append(st_p.reshape(bp, SSD_HEADS, SSD_HEAD_DIM, D_STATE))
        outs[4].append(heads(kv_s[..., ATT_W:2 * ATT_W]))
        outs[5].append(heads(kv_s[..., 2 * ATT_W:]))
        outs[6].append(xbc_s.reshape(db, ds, CONV_DIM)[:, ds - (CONV_W - 1):])
        outs[7].append(st_s.reshape(db, SSD_HEADS, SSD_HEAD_DIM, D_STATE))
    return (xp.reshape(bp, sp, D_MODEL), xs.reshape(db, ds, D_MODEL)) + tuple(jnp.stack(o) for o in outs)
```

```python
import functools
import math

import jax
import jax.numpy as jnp
import numpy as np
from jax import lax
from jax.experimental import pallas as pl
from jax.experimental.pallas import tpu as pltpu

F32 = jnp.float32
BF16 = jnp.bfloat16

D_MODEL = 1024
ATT_HEADS = 12
HEAD_DIM = 64
ATT_W = ATT_HEADS * HEAD_DIM
DILATIONS = (1, 4, 16)
QBLK = 128
WIN_MAX = 2048
N_BUCKETS = 32
MAX_EXACT = N_BUCKETS // 2
SSD_HEADS = 20
SSD_HEAD_DIM = 64
SSD_W = SSD_HEADS * SSD_HEAD_DIM
N_GROUPS = 4
D_STATE = 128
CONV_W = 4
CONV_DIM = SSD_W + 2 * N_GROUPS * D_STATE
SSD_CHUNK = 128
EPS = 1e-6

LANES = 128
VMEM_LIMIT = 48 * 1024 * 1024


def _rms(x, g):
    return x * lax.rsqrt(jnp.mean(x * x, axis=-1, keepdims=True) + EPS) * g


def _const_spec(shape):
    nd = len(shape)
    return pl.BlockSpec(shape, lambda *_: (0,) * nd, pipeline_mode=pl.Buffered(1))


IN_SPLITS = (0, 3 * ATT_W, 3 * ATT_W + SSD_W, 3 * ATT_W + SSD_W + CONV_DIM)


def _inproj_kernel(x_ref, g_ref, w_ref, wdt_ref, qkv_ref, z_ref, xbc_ref, dt_ref):
    h = _rms(x_ref[...], g_ref[...]).astype(BF16)
    nt = (((1,), (1,)), ((), ()))
    for out, lo, hi in zip((qkv_ref, z_ref, xbc_ref), IN_SPLITS[:-1], IN_SPLITS[1:]):
        out[...] = lax.dot_general(h, w_ref[lo:hi, :], nt, preferred_element_type=F32)
    dt_ref[...] = lax.dot_general(h, wdt_ref[...], nt, preferred_element_type=F32)


def _inproj(x, g_mix, w_t, wdt, tm=256):
    t = x.shape[0]
    assert t % tm == 0
    row = lambda n: pl.BlockSpec((tm, n), lambda i: (i, 0))
    return pl.pallas_call(
        _inproj_kernel,
        grid=(t // tm,),
        in_specs=[row(D_MODEL), _const_spec((1, D_MODEL)), _const_spec(w_t.shape), _const_spec(wdt.shape)],
        out_specs=[row(3 * ATT_W), row(SSD_W), row(CONV_DIM), row(LANES)],
        out_shape=[jax.ShapeDtypeStruct((t, 3 * ATT_W), F32), jax.ShapeDtypeStruct((t, SSD_W), F32),
                   jax.ShapeDtypeStruct((t, CONV_DIM), F32), jax.ShapeDtypeStruct((t, LANES), F32)],
        compiler_params=pltpu.CompilerParams(dimension_semantics=("parallel",),
                                             vmem_limit_bytes=VMEM_LIMIT),
        name="inproj",
    )(x, g_mix, w_t, wdt)


def _tail_kernel(x_ref, att_ref, ssd_ref, gatt_ref, woa_ref, wos_ref, gffn_ref,
                 wg_ref, wu_ref, wd_ref, gfin_ref, y_ref, *, final_norm):
    an = _rms(att_ref[...], gatt_ref[...]).astype(BF16)
    mix = jnp.dot(an, woa_ref[...], preferred_element_type=F32)
    mix = mix + jnp.dot(ssd_ref[...].astype(BF16), wos_ref[...], preferred_element_type=F32)
    x1 = x_ref[...] + mix
    h2 = _rms(x1, gffn_ref[...]).astype(BF16)
    gate = jnp.dot(h2, wg_ref[...], preferred_element_type=F32)
    up = jnp.dot(h2, wu_ref[...], preferred_element_type=F32)
    act = (gate * jax.nn.sigmoid(gate) * up).astype(BF16)
    x2 = x1 + jnp.dot(act, wd_ref[...], preferred_element_type=F32)
    y_ref[...] = _rms(x2, gfin_ref[...]) if final_norm else x2


def _tail(x, att, ssd, g_att, woa, wos, g_ffn, wg, wu, wd, g_fin, final_norm, tm=256):
    t = x.shape[0]
    assert t % tm == 0
    row = lambda n: pl.BlockSpec((tm, n), lambda i: (i, 0))
    consts = [g_att, woa, wos, g_ffn, wg, wu, wd, g_fin]
    return pl.pallas_call(
        functools.partial(_tail_kernel, final_norm=final_norm),
        grid=(t // tm,),
        in_specs=[row(D_MODEL), row(ATT_W), row(SSD_W)] + [_const_spec(c.shape) for c in consts],
        out_specs=row(D_MODEL),
        out_shape=jax.ShapeDtypeStruct((t, D_MODEL), F32),
        compiler_params=pltpu.CompilerParams(dimension_semantics=("parallel",),
                                             vmem_limit_bytes=VMEM_LIMIT),
        name="tail",
    )(x, att, ssd, *consts)


TQ = QBLK * max(DILATIONS)
NEG_INF = float("-inf")
LOG2E = math.log2(math.e)


def _rel_bucket(dist):
    n = np.maximum(dist, 0)
    nf = np.maximum(n, 1).astype(np.float32)
    large = MAX_EXACT + (np.log(nf / MAX_EXACT) / math.log(WIN_MAX / MAX_EXACT)
                         * (N_BUCKETS - MAX_EXACT)).astype(np.int32)
    return np.where(n < MAX_EXACT, n, np.minimum(large, N_BUCKETS - 1))


def _band_buckets():
    qi = np.arange(QBLK)[:, None] + QBLK
    kk = np.arange(2 * QBLK)[None, :]
    dist = qi - kk
    valid = (dist >= 0) & (dist <= QBLK)
    return jnp.asarray(np.stack([np.where(valid, _rel_bucket(dist * d), -1) for d in DILATIONS]).astype(np.int32))


STAGE = 4
assert DILATIONS == (1, STAGE, STAGE * STAGE)


def _regroup(src_ref, tmp, dsts, scale=None):
    cast = (lambda x: x.astype(BF16)) if scale is None else (lambda x: (x * scale).astype(BF16))
    sub = TQ // STAGE
    for c in range(STAGE):
        dsts[0][c * sub:(c + 1) * sub, :] = cast(src_ref[c * sub:(c + 1) * sub, :])
    for r in range(STAGE):
        x = src_ref[pl.ds(r, sub, stride=STAGE), :]
        tmp[r] = x
        dsts[1][r * sub:(r + 1) * sub, :] = cast(x)
    for r in range(STAGE):
        for r2 in range(STAGE):
            res = r + STAGE * r2
            dsts[2][res * QBLK:(res + 1) * QBLK, :] = cast(tmp[r, pl.ds(r2, QBLK, stride=STAGE), :])


def _attn_prompt_kernel(tbl_ref, q_ref, k_ref, v_ref, bkt_ref, o_ref,
                        qd, kd, vd, tmp, bias, macc, nacc, dacc):
    hp = pl.program_id(0)
    t = pl.program_id(1)
    nb = len(DILATIONS)
    slot = t % 2

    @pl.when(t == 0)
    def _no_previous_tile():
        kd[:, 1] = jnp.zeros((nb, TQ, LANES), BF16)
        vd[:, 1] = jnp.zeros((nb, TQ, LANES), BF16)

    @pl.when(t == 0)
    def _build_bias():
        prev_half = lax.broadcasted_iota(jnp.int32, (1, 2 * QBLK), 1) < QBLK
        for b in range(len(DILATIONS)):
            bk = bkt_ref[b]
            for hh in range(2):
                acc = jnp.full((QBLK, 2 * QBLK), NEG_INF, F32)
                for u in range(N_BUCKETS):
                    acc = jnp.where(bk == u, tbl_ref[u, 2 * hp + hh], acc)
                acc = acc * LOG2E
                bias[b, 0, hh * QBLK:(hh + 1) * QBLK] = acc
                bias[b, 1, hh * QBLK:(hh + 1) * QBLK] = jnp.where(prev_half, NEG_INF, acc)

    _regroup(q_ref, tmp, [qd.at[b] for b in range(nb)], HEAD_DIM ** -0.5 * LOG2E)
    _regroup(k_ref, tmp, [kd.at[b, slot] for b in range(nb)])
    _regroup(v_ref, tmp, [vd.at[b, slot] for b in range(nb)])

    left = lax.broadcasted_iota(jnp.int32, (1, LANES), 1) < HEAD_DIM
    nt = (((1,), (1,)), ((), ()))

    def block(b, d, j):
        nbr = max(DILATIONS) // d
        r, m = divmod(j, nbr)
        qs = r + d * QBLK * m
        rows = pl.ds(qs, QBLK) if d == 1 else pl.ds(qs, QBLK, stride=d)
        gs = j * QBLK
        wrap = m == 0
        ps = (r * nbr + (nbr - 1 if wrap else m - 1)) * QBLK
        pslot = 1 - slot if wrap else slot
        variant = (t == 0).astype(jnp.int32) if wrap else 0
        q = qd[b, gs:gs + QBLK, :]
        kc = jnp.concatenate([kd[b, pslot, ps:ps + QBLK, :], kd[b, slot, gs:gs + QBLK, :]], axis=0)
        vc = jnp.concatenate([vd[b, pslot, ps:ps + QBLK, :], vd[b, slot, gs:gs + QBLK, :]], axis=0)
        zero = jnp.zeros_like(q)
        qq = jnp.concatenate([jnp.where(left, q, zero), jnp.where(left, zero, q)], axis=0)
        s = lax.dot_general(qq, kc, nt, preferred_element_type=F32) + bias[b, variant]
        mx = jnp.max(s, axis=-1, keepdims=True)
        e = jnp.exp2(s - mx)
        den = jnp.sum(e, axis=-1, keepdims=True)
        res = jnp.dot(e.astype(BF16), vc, preferred_element_type=F32)
        nacc[b, rows, :] = jnp.where(left, res[:QBLK], res[QBLK:])
        dacc[b, rows, :] = jnp.where(left, den[:QBLK], den[QBLK:])
        macc[b, rows, :] = jnp.where(left, mx[:QBLK], mx[QBLK:])

    def merge(c):
        sl = slice(c * QBLK, (c + 1) * QBLK)
        ms = [macc[b, sl, :] for b in range(nb)]
        top = jnp.maximum(jnp.maximum(ms[0], ms[1]), ms[2])
        ws = [jnp.exp2(mb - top) for mb in ms]
        num = sum(w * nacc[b, sl, :] for b, w in enumerate(ws))
        den = sum(w * dacc[b, sl, :] for b, w in enumerate(ws))
        o_ref[sl, :] = num / den

    for b in reversed(range(nb)):
        for j in range(TQ // QBLK):
            block(b, DILATIONS[b], j)
            if b == 0:
                merge(j)


def _attn_prompt(qkv, rel_bias):
    s = qkv.shape[0]
    assert s % TQ == 0
    nb = len(DILATIONS)
    hp_n = ATT_W // LANES
    tile = lambda f: pl.BlockSpec((TQ, LANES), f)
    return pl.pallas_call(
        _attn_prompt_kernel,
        grid=(hp_n, s // TQ),
        in_specs=[pl.BlockSpec(memory_space=pltpu.SMEM),
                  tile(lambda h, t: (t, h)),
                  tile(lambda h, t: (t, hp_n + h)),
                  tile(lambda h, t: (t, 2 * hp_n + h)),
                  _const_spec((nb, QBLK, 2 * QBLK))],
        out_specs=tile(lambda h, t: (t, h)),
        out_shape=jax.ShapeDtypeStruct((s, ATT_W), F32),
        scratch_shapes=[pltpu.VMEM((nb, TQ, LANES), BF16), pltpu.VMEM((nb, 2, TQ, LANES), BF16),
                        pltpu.VMEM((nb, 2, TQ, LANES), BF16), pltpu.VMEM((STAGE, TQ // STAGE, LANES), F32),
                        pltpu.VMEM((nb, 2, 2 * QBLK, 2 * QBLK), F32),
                        pltpu.VMEM((nb, TQ, LANES), F32), pltpu.VMEM((nb, TQ, LANES), F32),
                        pltpu.VMEM((nb, TQ, LANES), F32)],
        compiler_params=pltpu.CompilerParams(dimension_semantics=("parallel", "arbitrary"),
                                             vmem_limit_bytes=VMEM_LIMIT),
        name="attn_prompt",
    )(rel_bias, qkv, qkv, qkv, _band_buckets())


HPG = SSD_HEADS // N_GROUPS
GROUP_W = SSD_W // N_GROUPS
GROUP_WIN = 3 * LANES
GROUP_LO = tuple((g * GROUP_W) // LANES * LANES for g in range(N_GROUPS))
CARRY_ROW = 8


def _split(x, terms):
    parts = []
    for _ in range(terms):
        p = x.astype(BF16)
        parts.append(p)
        x = x - p.astype(F32)
    return parts


def _dot_exact_rhs(xs_terms, m):
    pieces = [p for x, terms in xs_terms for p in _split(x, terms)]
    out = jnp.dot(jnp.concatenate(pieces, axis=0) if len(pieces) > 1 else pieces[0], m,
                  preferred_element_type=F32)
    res, r = [], 0
    for x, terms in xs_terms:
        n = x.shape[0]
        res.append(sum(out[r + i * n:r + (i + 1) * n] for i in range(terms)))
        r += terms * n
    return res


def _dot_exact_lhs(m, x, terms=3):
    n = x.shape[1]
    out = jnp.dot(m, jnp.concatenate(_split(x, terms), axis=1), preferred_element_type=F32)
    return sum(out[:, i * n:(i + 1) * n] for i in range(terms))


def _softplus(x):
    return jnp.maximum(x, 0.0) + jnp.log1p(jnp.exp(-jnp.abs(x)))


def _silu(x):
    h = 0.5 * x
    return h + h * jnp.tanh(h)


def _ssd_consts():
    c = jnp.arange(SSD_W)
    head_of = (c // SSD_HEAD_DIM)[None, :] == jnp.arange(LANES)[:, None]
    group_of = (c // GROUP_W)[None, :] == jnp.arange(LANES)[:, None]
    return head_of.astype(BF16), group_of.astype(BF16), group_of.T.astype(BF16)


def _conv_silu(cbuf, xbc_ref, w_ref, b_ref, rows):
    out = []
    for c in range(CONV_DIM // LANES):
        cols = slice(c * LANES, (c + 1) * LANES)
        cbuf[c, CARRY_ROW:CARRY_ROW + rows, :] = xbc_ref[:, cols]
        acc = b_ref[:, cols]
        for j in range(CONV_W):
            acc = acc + cbuf[c, pl.ds(CARRY_ROW - (CONV_W - 1) + j, rows), :] * w_ref[j:j + 1, cols]
        out.append(_silu(acc))
    return out


def _drain(steps):
    try:
        while True:
            next(steps)
    except StopIteration as done:
        return done.value


def _interleave(*bodies):
    live = list(bodies)
    while live:
        for body in list(live):
            try:
                next(body)
            except StopIteration:
                live.remove(body)


def _ssd_ydiag(*args):
    return _drain(_ssd_ydiag_steps(*args))


def _ssd_ydiag_steps(xs, bm, cm, acs, dt, mask):
    cbs = [lax.dot_general(cm[g], bm[g], (((1,), (1,)), ((), ())), preferred_element_type=F32)
           for g in range(N_GROUPS)]
    acs_t = acs.T
    dt_t = dt.T
    left = lax.broadcasted_iota(jnp.int32, (1, LANES), 1) < SSD_HEAD_DIM
    pairs = []
    for pair in range(SSD_HEADS // 2):
        xpair = xs[:, pair * LANES:(pair + 1) * LANES]
        acc = None
        for hh in range(2):
            h = 2 * pair + hh
            seg = acs[:, h:h + 1] - acs_t[h:h + 1, :]
            dec = jnp.exp2(jnp.where(mask, seg, NEG_INF))
            w = (cbs[h // HPG] * dec * dt_t[h:h + 1, :]).astype(BF16)
            xh = jnp.where(left if hh == 0 else jnp.logical_not(left), xpair, 0.0).astype(BF16)
            part = jnp.dot(w, xh, preferred_element_type=F32)
            acc = part if acc is None else acc + part
        pairs.append(acc)
        if pair % 2 == 1:
            yield
    return jnp.concatenate(pairs, axis=1)


def _ssd_gate_norm(y, gate, gred, gexp, gout):
    y = y * gate
    ms = _dot_exact_rhs([(y * y, 2)], gred)[0] * (1.0 / GROUP_W)
    return y * _dot_exact_rhs([(lax.rsqrt(ms + EPS), 2)], gexp)[0] * gout


def _ssd_prompt_init(cbuf, state):
    @pl.when(pl.program_id(0) == 0)
    def _init():
        state[...] = jnp.zeros_like(state)
        cbuf[:, 0:CARRY_ROW, :] = jnp.zeros((CONV_DIM // LANES, CARRY_ROW, LANES), F32)


def _ssd_prompt_emit(st_ref, state):
    @pl.when(pl.program_id(0) == pl.num_programs(0) - 1)
    def _emit_state():
        for i in range(SSD_W // LANES):
            st_ref[i * LANES:(i + 1) * LANES, :] = state[:, i * LANES:(i + 1) * LANES].T


def _ssd_prompt_chunk(z_ref, xbc_ref, dt_ref, cw_ref, cb_ref, dtb_ref, alog_ref, dsk_ref, gout_ref,
                      hexp_ref, gexp_ref, gred_ref, y_ref, cbuf, state, yoff):
    cl = SSD_CHUNK
    xc = _conv_silu(cbuf, xbc_ref, cw_ref, cb_ref, cl)
    cbuf[:, CARRY_ROW - (CONV_W - 1):CARRY_ROW, :] = cbuf[:, CARRY_ROW + cl - (CONV_W - 1):CARRY_ROW + cl, :]
    nx = SSD_W // LANES
    xs = jnp.concatenate(xc[:nx], axis=1)
    bm = [b.astype(BF16) for b in xc[nx:nx + N_GROUPS]]
    cm = [b.astype(BF16) for b in xc[nx + N_GROUPS:]]
    yield

    li = lax.broadcasted_iota(jnp.int32, (cl, cl), 0)
    si = lax.broadcasted_iota(jnp.int32, (cl, cl), 1)
    causal = li >= si

    dt = _softplus(dt_ref[...] + dtb_ref[...])
    acs = _dot_exact_lhs(causal.astype(BF16), dt * (-jnp.exp(alog_ref[...]) * LOG2E))
    hexp = hexp_ref[...]
    e_in, e_end = _dot_exact_rhs([(jnp.exp2(acs), 3), (jnp.exp2(acs[cl - 1:cl, :] - acs) * dt, 1)], hexp)
    xw = (xs * e_end).astype(BF16)
    chunk_decay = e_in[cl - 1:cl]
    yield

    gmask = lax.broadcasted_iota(jnp.int32, (1, SSD_W), 1) // GROUP_W
    for g in range(N_GROUPS):
        lo = GROUP_LO[g]
        win = slice(lo, lo + GROUP_WIN)
        inside = gmask[:, win] == g
        yo = jnp.dot(cm[g], state[:, win].astype(BF16), preferred_element_type=F32)
        yoff[:, win] = jnp.where(inside, yo, 0.0 if g == 0 else yoff[:, win])
    yield
    for g in range(N_GROUPS):
        lo = GROUP_LO[g]
        win = slice(lo, lo + GROUP_WIN)
        inside = gmask[:, win] == g
        upd = lax.dot_general(bm[g], xw[:, win], (((0,), (0,)), ((), ())), preferred_element_type=F32)
        old = state[:, win]
        state[:, win] = jnp.where(inside, old * chunk_decay[:, win] + upd, old)
    yield

    ydiag = yield from _ssd_ydiag_steps(xs, bm, cm, acs, dt, causal)
    y_ref[...] = _ssd_gate_norm(ydiag + yoff[...] * e_in + xs * dsk_ref[...], _silu(z_ref[...]),
                                gred_ref[...], gexp_ref[...], gout_ref[...])


N_SSD_IN = 12


def _ssd_prompt_kernel(*refs):
    ins, (y_ref, st_ref, cbuf, state, yoff) = refs[:N_SSD_IN], refs[N_SSD_IN:]
    _ssd_prompt_init(cbuf, state)
    _drain(_ssd_prompt_chunk(*ins, y_ref, cbuf, state, yoff))
    _ssd_prompt_emit(st_ref, state)


def _ssd_prompt_call(z, xbc, dt, conv_w, conv_b, dt_bias, a_log, dskip_x, g_out):
    s = z.shape[0]
    assert s % SSD_CHUNK == 0
    hexp, gexp, gred = _ssd_consts()
    row = lambda n: pl.BlockSpec((SSD_CHUNK, n), lambda i: (i, 0))
    consts = [conv_w, conv_b, dt_bias, a_log, dskip_x, g_out, hexp, gexp, gred]
    return dict(
        steps=s // SSD_CHUNK,
        operands=[z, xbc, dt] + consts,
        in_specs=[row(SSD_W), row(CONV_DIM), row(LANES)] + [_const_spec(a.shape) for a in consts],
        out_specs=[row(SSD_W), pl.BlockSpec((SSD_W, D_STATE), lambda i: (0, 0))],
        out_shape=[jax.ShapeDtypeStruct((s, SSD_W), F32), jax.ShapeDtypeStruct((SSD_W, D_STATE), F32)],
        scratch=[pltpu.VMEM((CONV_DIM // LANES, CARRY_ROW + SSD_CHUNK, LANES), F32),
                 pltpu.VMEM((D_STATE, SSD_W), F32), pltpu.VMEM((SSD_CHUNK, SSD_W), F32)])


def _run(kernel_fn, name, *calls, vmem_limit=VMEM_LIMIT):
    steps = calls[0]["steps"]
    assert all(c["steps"] == steps for c in calls)
    cat = lambda key: [x for c in calls for x in c[key]]
    return pl.pallas_call(
        kernel_fn, grid=(steps,), in_specs=cat("in_specs"), out_specs=cat("out_specs"),
        out_shape=cat("out_shape"), scratch_shapes=cat("scratch"),
        compiler_params=pltpu.CompilerParams(dimension_semantics=("arbitrary",), vmem_limit_bytes=vmem_limit),
        name=name)(*cat("operands"))


def _ssd_prompt(*args):
    return _run(_ssd_prompt_kernel, "ssd_prompt", _ssd_prompt_call(*args))


DEC_SEQ = 4
NEW_ROWS = 8


def _branch_count(dist):
    return sum(((dist % d == 0) & (dist <= QBLK * d)).astype(np.int32) for d in DILATIONS)


def _sample_bias_tables(wb):
    t = np.arange(NEW_ROWS)[:, None]
    dist = wb + t - np.arange(wb)[None, :]
    cnt = np.where(t < DEC_SEQ, _branch_count(dist), 0)
    bkt = np.where(cnt > 0, _rel_bucket(dist), -1).astype(np.int32)
    ladd = np.log(np.maximum(cnt, 1).astype(np.float32))
    u = np.arange(LANES)[None, :]
    dnew = t - u
    cnew = np.where((t < DEC_SEQ) & (dnew > 0), _branch_count(dnew),
                    np.where((t < DEC_SEQ) & (dnew == 0), len(DILATIONS), 0))
    bkt_new = np.where(cnew > 0, _rel_bucket(dnew), -1).astype(np.int32)
    ladd_new = np.log(np.maximum(cnew, 1).astype(np.float32))
    return tuple(jnp.asarray(a) for a in (bkt, ladd, bkt_new, ladd_new))


def _attn_sample_init(tbl_ref, bkt_ref, ladd_ref, bktn_ref, laddn_ref, bias, bias_new):
    @pl.when(pl.program_id(0) == 0)
    def _build_bias():
        for h in range(ATT_HEADS):
            for src, add, dst in ((bkt_ref, ladd_ref, bias), (bktn_ref, laddn_ref, bias_new)):
                bk = src[...]
                acc = jnp.full(bk.shape, NEG_INF, F32)
                for u in range(N_BUCKETS):
                    acc = jnp.where(bk == u, tbl_ref[u, h], acc)
                dst[h // 2, (h % 2) * NEW_ROWS:(h % 2 + 1) * NEW_ROWS, :] = (acc + add[...]) * LOG2E


N_ATTN_S_IN = 8


def _attn_sample_kernel(tbl_ref, q_ref, kt_ref, vt_ref, bkt_ref, ladd_ref, bktn_ref, laddn_ref,
                        o_ref, bias, bias_new):
    _attn_sample_init(tbl_ref, bkt_ref, ladd_ref, bktn_ref, laddn_ref, bias, bias_new)
    whole = lambda hp: (kt_ref.at[0], vt_ref.at[0], hp * LANES)
    _drain(_attn_sample_seq(q_ref, whole, o_ref, bias, bias_new))


RING = 3
RING_VMEM_LIMIT = 58 * 1024 * 1024


def _ring_kv(k_hbm, v_hbm, kring, vring, sems):
    i = pl.program_id(0)
    n = pl.num_programs(0)

    def copies(seq, slot):
        return (pltpu.make_async_copy(k_hbm.at[seq], kring.at[slot], sems.at[0, slot]),
                pltpu.make_async_copy(v_hbm.at[seq], vring.at[slot], sems.at[1, slot]))

    def start(seq, slot):
        for c in copies(jnp.minimum(seq, n - 1), slot):
            c.start()

    @pl.when(i == 0)
    def _prime():
        for s in range(RING - 1):
            start(s, s)

    start(i + RING - 1, (i + RING - 1) % RING)
    slot = i % RING
    for c in copies(i, slot):
        c.wait()

    def finish():
        @pl.when(i == n - 1)
        def _drain_tail():
            for s in range(RING - 1):
                for c in copies(0, (n + s) % RING):
                    c.wait()

    return (lambda hp: (kring.at[slot], vring.at[slot], hp * LANES)), finish


def _attn_sample_seq(q_ref, kv_for_pair, o_ref, bias, bias_new):
    left = lax.broadcasted_iota(jnp.int32, (1, LANES), 1) < HEAD_DIM
    nt = (((1,), (1,)), ((), ()))
    scale = HEAD_DIM ** -0.5 * LOG2E
    outs = []
    for hp in range(ATT_W // LANES):
        cols = slice(hp * LANES, (hp + 1) * LANES)
        k_ref, v_ref, row0 = kv_for_pair(hp)
        kv_rows = slice(row0, row0 + LANES)
        q = q_ref[0, :, cols] * scale
        fill = jnp.zeros((LANES - NEW_ROWS, LANES), F32)
        knew = jnp.concatenate([q_ref[0, :, ATT_W + hp * LANES:ATT_W + (hp + 1) * LANES], fill], axis=0)
        vnew = jnp.concatenate([q_ref[0, :, 2 * ATT_W + hp * LANES:2 * ATT_W + (hp + 1) * LANES], fill], axis=0)
        qq = jnp.concatenate([jnp.where(left, q, 0.0), jnp.where(left, 0.0, q)], axis=0).astype(BF16)
        s = jnp.dot(qq, k_ref[kv_rows, :].astype(BF16), preferred_element_type=F32) + bias[hp]
        sn = lax.dot_general(qq, knew.astype(BF16), nt, preferred_element_type=F32) + bias_new[hp]
        m = jnp.maximum(jnp.max(s, axis=-1, keepdims=True), jnp.max(sn, axis=-1, keepdims=True))
        m = jnp.where(m == NEG_INF, 0.0, m)
        p = jnp.exp2(s - m)
        pn = jnp.exp2(sn - m)
        den = jnp.sum(p, axis=-1, keepdims=True) + jnp.sum(pn, axis=-1, keepdims=True)
        r = lax.dot_general(p.astype(BF16), v_ref[kv_rows, :].astype(BF16), nt, preferred_element_type=F32)
        r = r + jnp.dot(pn.astype(BF16), vnew.astype(BF16), preferred_element_type=F32)
        o = r / jnp.where(den == 0.0, 1.0, den)
        outs.append(jnp.where(left, o[:NEW_ROWS], o[NEW_ROWS:]))
        yield
    o_ref[0] = jnp.concatenate(outs, axis=1)[:DEC_SEQ]


def _attn_sample_call(qkv, cache_kt, cache_vt, rel_bias, ring=False):
    db, _, wb = cache_kt.shape
    assert wb >= WIN_MAX and qkv.shape[0] == db * DEC_SEQ
    q8 = jnp.pad(qkv.reshape(db, DEC_SEQ, 3 * ATT_W), ((0, 0), (0, NEW_ROWS - DEC_SEQ), (0, 0)))
    consts = list(_sample_bias_tables(wb))
    hp_n = ATT_W // LANES
    seq = lambda n, m: pl.BlockSpec((1, n, m), lambda b: (b, 0, 0))
    kv_spec = pl.BlockSpec(memory_space=pl.ANY) if ring else seq(ATT_W, wb)
    scratch = [pltpu.VMEM((hp_n, 2 * NEW_ROWS, wb), F32), pltpu.VMEM((hp_n, 2 * NEW_ROWS, LANES), F32)]
    if ring:
        scratch += [pltpu.VMEM((RING, ATT_W, wb), F32), pltpu.VMEM((RING, ATT_W, wb), F32),
                    pltpu.SemaphoreType.DMA((2, RING))]
    return dict(
        steps=db,
        operands=[rel_bias, q8, cache_kt, cache_vt] + consts,
        in_specs=[pl.BlockSpec(memory_space=pltpu.SMEM), seq(NEW_ROWS, 3 * ATT_W), kv_spec, kv_spec]
                 + [_const_spec(c.shape) for c in consts],
        out_specs=[seq(DEC_SEQ, ATT_W)],
        out_shape=[jax.ShapeDtypeStruct((db, DEC_SEQ, ATT_W), F32)],
        scratch=scratch)


def _attn_sample(qkv, cache_kt, cache_vt, rel_bias):
    out, = _run(_attn_sample_kernel, "attn_sample", _attn_sample_call(qkv, cache_kt, cache_vt, rel_bias))
    return out.reshape(-1, ATT_W)


def _ssd_prompt_attn_sample_kernel(*refs):
    a = N_SSD_IN + N_ATTN_S_IN
    ssd_in = refs[:N_SSD_IN]
    tbl_ref, q_ref, kt_ref, vt_ref, bkt_ref, ladd_ref, bktn_ref, laddn_ref = refs[N_SSD_IN:a]
    y_ref, st_ref, o_ref, cbuf, state, yoff, bias, bias_new, kring, vring, sems = refs[a:]
    _ssd_prompt_init(cbuf, state)
    _attn_sample_init(tbl_ref, bkt_ref, ladd_ref, bktn_ref, laddn_ref, bias, bias_new)
    kv_for_pair, finish_ring = _ring_kv(kt_ref, vt_ref, kring, vring, sems)
    _interleave(_ssd_prompt_chunk(*ssd_in, y_ref, cbuf, state, yoff),
                _attn_sample_seq(q_ref, kv_for_pair, o_ref, bias, bias_new))
    finish_ring()
    _ssd_prompt_emit(st_ref, state)


def _ssd_prompt_attn_sample(ssd_args, attn_args):
    y, st, att = _run(_ssd_prompt_attn_sample_kernel, "ssd_prompt_attn_sample",
                      _ssd_prompt_call(*ssd_args), _attn_sample_call(*attn_args, ring=True),
                      vmem_limit=RING_VMEM_LIMIT)
    return y, st, att.reshape(-1, ATT_W)


SEQ_ROWS = 8
SEQ_PER_STEP = 8


def _ssd_sample_kernel(z_ref, xp_ref, dt_ref, st_ref, cw_ref, cb_ref, dtb_ref, alog_ref, dsk_ref, gout_ref,
                       hexp_ref, gexp_ref, gred_ref, y_ref, sto_ref, cbuf, yoff):
    rows = SEQ_PER_STEP * SEQ_ROWS
    cbuf[:, 0:CARRY_ROW, :] = jnp.zeros((CONV_DIM // LANES, CARRY_ROW, LANES), F32)
    xc = _conv_silu(cbuf, xp_ref, cw_ref, cb_ref, rows)
    nx = SSD_W // LANES
    xs = jnp.concatenate(xc[:nx], axis=1)
    bmf = xc[nx:nx + N_GROUPS]
    bm = [b.astype(BF16) for b in bmf]
    cm = [b.astype(BF16) for b in xc[nx + N_GROUPS:]]

    li = lax.broadcasted_iota(jnp.int32, (rows, rows), 0)
    si = lax.broadcasted_iota(jnp.int32, (rows, rows), 1)
    same = (li // SEQ_ROWS) == (si // SEQ_ROWS)
    mask = same & (li >= si)
    rowi = lax.broadcasted_iota(jnp.int32, (rows, 1), 0)
    is_token = (rowi % SEQ_ROWS) >= SEQ_ROWS - DEC_SEQ

    dt = jnp.where(is_token, _softplus(dt_ref[...] + dtb_ref[...]), 0.0)
    da = dt * (-jnp.exp(alog_ref[...]) * LOG2E)
    acs = _dot_exact_lhs(mask.astype(BF16), da)
    acs_end = _dot_exact_lhs(same.astype(BF16), da)
    hexp = hexp_ref[...]
    e_in, e_end, e_all = _dot_exact_rhs(
        [(jnp.exp2(acs), 2), (jnp.exp2(acs_end - acs) * dt, 1), (jnp.exp2(acs_end), 3)], hexp)
    xw = xs * e_end

    sub = rowi % SEQ_ROWS
    pieces = [p.astype(F32) for p in _split(e_all, 3)]
    dstack = jnp.where(sub == 0, pieces[0], jnp.where(sub == 1, pieces[1], jnp.where(sub == 2, pieces[2], 0.0)))
    ones = jnp.ones((SEQ_ROWS, D_STATE), BF16)
    gmask = lax.broadcasted_iota(jnp.int32, (1, SSD_W), 1) // GROUP_W
    tn = (((0,), (0,)), ((), ()))
    nt = (((1,), (1,)), ((), ()))
    for i in range(SEQ_PER_STEP):
        rs = slice(i * SEQ_ROWS, (i + 1) * SEQ_ROWS)
        for g in range(N_GROUPS):
            lo = GROUP_LO[g]
            win = slice(lo, lo + GROUP_WIN)
            inside = gmask[:, win] == g
            yo = lax.dot_general(cm[g][rs], st_ref[i, win, :].astype(BF16), nt, preferred_element_type=F32)
            yoff[rs, win] = jnp.where(inside, yo, 0.0 if g == 0 else yoff[rs, win])
        xstack = jnp.concatenate([jnp.where(gmask == g, xw[rs, :], 0.0) for g in range(N_GROUPS)], axis=0)
        bstack = jnp.concatenate([bmf[g][rs] for g in range(N_GROUPS)], axis=0)
        upd = lax.dot_general(xstack.astype(BF16), bstack.astype(BF16), tn, preferred_element_type=F32)
        dcol = lax.dot_general(dstack[rs, :].astype(BF16), ones, tn, preferred_element_type=F32)
        sto_ref[i] = st_ref[i] * dcol + upd

    ydiag = _ssd_ydiag(xs, bm, cm, acs, dt, mask)
    y_ref[...] = _ssd_gate_norm(ydiag + yoff[...] * e_in + xs * dsk_ref[...], _silu(z_ref[...]),
                                gred_ref[...], gexp_ref[...], gout_ref[...])


def _ssd_sample(z, xbc, dt, conv_state, ssm_state, conv_w, conv_b, dt_bias, a_log, dskip_x, g_out):
    db = conv_state.shape[0]
    assert db % SEQ_PER_STEP == 0 and z.shape[0] == db * DEC_SEQ
    lead = SEQ_ROWS - DEC_SEQ

    def padded(a):
        a = a.reshape(db, DEC_SEQ, a.shape[-1])
        return jnp.pad(a, ((0, 0), (lead, 0), (0, 0))).reshape(db * SEQ_ROWS, a.shape[-1])

    xp = jnp.concatenate([jnp.zeros((db, lead - (CONV_W - 1), CONV_DIM), F32), conv_state,
                          xbc.reshape(db, DEC_SEQ, CONV_DIM)], axis=1).reshape(db * SEQ_ROWS, CONV_DIM)
    hexp, gexp, gred = _ssd_consts()
    rows = SEQ_PER_STEP * SEQ_ROWS
    row = lambda n: pl.BlockSpec((rows, n), lambda i: (i, 0))
    st_spec = pl.BlockSpec((SEQ_PER_STEP, SSD_W, D_STATE), lambda i: (i, 0, 0))
    consts = [conv_w, conv_b, dt_bias, a_log, dskip_x, g_out, hexp, gexp, gred]
    y, st = pl.pallas_call(
        _ssd_sample_kernel,
        grid=(db // SEQ_PER_STEP,),
        in_specs=[row(SSD_W), row(CONV_DIM), row(LANES), st_spec] + [_const_spec(a.shape) for a in consts],
        out_specs=[row(SSD_W), st_spec],
        out_shape=[jax.ShapeDtypeStruct((db * SEQ_ROWS, SSD_W), F32),
                   jax.ShapeDtypeStruct((db, SSD_W, D_STATE), F32)],
        scratch_shapes=[pltpu.VMEM((CONV_DIM // LANES, CARRY_ROW + rows, LANES), F32),
                        pltpu.VMEM((rows, SSD_W), F32)],
        compiler_params=pltpu.CompilerParams(dimension_semantics=("parallel",),
                                             vmem_limit_bytes=VMEM_LIMIT),
        name="ssd_sample",
    )(padded(z), xp, padded(dt), ssm_state, *consts)
    y = y.reshape(db, SEQ_ROWS, SSD_W)[:, lead:].reshape(db * DEC_SEQ, SSD_W)
    return y, st


def kernel(x_prompt, x_sample, cache_k, cache_v, state_conv, state_ssm, rel_bias, g_mix, w_in, conv_w, conv_b,
           dt_bias, a_log, d_skip, g_attn_out, g_ssd_out, w_out, g_ffn, w_gate, w_up, w_down, g_final):
    depth = w_in.shape[0]
    bp, sp, _ = x_prompt.shape
    db, ds, _ = x_sample.shape
    assert bp == 1 and ds == DEC_SEQ and sp >= WIN_MAX
    xp = x_prompt.reshape(bp * sp, D_MODEL)
    xs = x_sample.reshape(db * ds, D_MODEL)
    pad_heads = lambda v: jnp.pad(v, (0, LANES - SSD_HEADS))[None]
    outs = [[] for _ in range(8)]
    for l in range(depth):
        w_t = w_in[l].T.astype(BF16)
        wdt = jnp.pad(w_t[IN_SPLITS[-1]:], ((0, LANES - SSD_HEADS), (0, 0)))
        ssd_prm = (conv_w[l], conv_b[l][None], pad_heads(dt_bias[l]), pad_heads(a_log[l]),
                   jnp.repeat(d_skip[l], SSD_HEAD_DIM)[None], g_ssd_out[l][None])
        tail_prm = (g_attn_out[l][None], w_out[l][:ATT_W].astype(BF16), w_out[l][ATT_W:].astype(BF16),
                    g_ffn[l][None], w_gate[l].astype(BF16), w_up[l].astype(BF16), w_down[l].astype(BF16),
                    g_final[None])
        last = l == depth - 1

        qkv_p, z_p, xbc_p, dt_p = _inproj(xp, g_mix[l][None], w_t, wdt)
        qkv_s, z_s, xbc_s, dt_s = _inproj(xs, g_mix[l][None], w_t, wdt)
        att_p = _attn_prompt(qkv_p, rel_bias)
        feature_major = lambda c: jnp.transpose(c, (0, 2, 3, 1)).reshape(db, ATT_W, c.shape[1])
        ssd_args = (z_p, xbc_p, dt_p, *ssd_prm)
        attn_args = (qkv_s, feature_major(cache_k[l]), feature_major(cache_v[l]), rel_bias)
        if sp // SSD_CHUNK == db:
            ssd_p, st_p, att_s = _ssd_prompt_attn_sample(ssd_args, attn_args)
        else:
            ssd_p, st_p = _ssd_prompt(*ssd_args)
            att_s = _attn_sample(*attn_args)
        xp = _tail(xp, att_p, ssd_p, *tail_prm, final_norm=last)
        ssd_s, st_s = _ssd_sample(z_s, xbc_s, dt_s, state_conv[l], state_ssm[l].reshape(db, SSD_W, D_STATE),
                                  *ssd_prm)
        xs = _tail(xs, att_s, ssd_s, *tail_prm, final_norm=last)

        heads = lambda a: a.reshape(a.shape[0], a.shape[1], ATT_HEADS, HEAD_DIM)
        kv_p = qkv_p.reshape(bp, sp, 3 * ATT_W)[:, sp - WIN_MAX:]
        kv_s = qkv_s.reshape(db, ds, 3 * ATT_W)
        outs[0].append(heads(kv_p[..., ATT_W:2 * ATT_W]))
        outs[1].append(heads(kv_p[..., 2 * ATT_W:]))
        outs[2].append(xbc_p.reshape(bp, sp, CONV_DIM)[:, sp - (CONV_W - 1):])
        outs[3].append(st_p.reshape(bp, SSD_HEADS, SSD_HEAD_DIM, D_STATE))
        outs[4].append(heads(kv_s[..., ATT_W:2 * ATT_W]))
        outs[5].append(heads(kv_s[..., 2 * ATT_W:]))
        outs[6].append(xbc_s.reshape(db, ds, CONV_DIM)[:, ds - (CONV_W - 1):])
        outs[7].append(st_s.reshape(db, SSD_HEADS, SSD_HEAD_DIM, D_STATE))
    return (xp.reshape(bp, sp, D_MODEL), xs.reshape(db, ds, D_MODEL)) + tuple(jnp.stack(o) for o in outs)
```

```python
import functools
import math

import jax
import jax.numpy as jnp
import numpy as np
from jax import lax
from jax.experimental import pallas as pl
from jax.experimental.pallas import tpu as pltpu

F32 = jnp.float32
BF16 = jnp.bfloat16

D_MODEL = 1024
ATT_HEADS = 12
HEAD_DIM = 64
ATT_W = ATT_HEADS * HEAD_DIM
DILATIONS = (1, 4, 16)
QBLK = 128
WIN_MAX = 2048
N_BUCKETS = 32
MAX_EXACT = N_BUCKETS // 2
SSD_HEADS = 20
SSD_HEAD_DIM = 64
SSD_W = SSD_HEADS * SSD_HEAD_DIM
N_GROUPS = 4
D_STATE = 128
CONV_W = 4
CONV_DIM = SSD_W + 2 * N_GROUPS * D_STATE
SSD_CHUNK = 128
EPS = 1e-6

LANES = 128
VMEM_LIMIT = 48 * 1024 * 1024


def _rms(x, g):
    return x * lax.rsqrt(jnp.mean(x * x, axis=-1, keepdims=True) + EPS) * g


def _const_spec(shape):
    nd = len(shape)
    return pl.BlockSpec(shape, lambda *_: (0,) * nd, pipeline_mode=pl.Buffered(1))


IN_SPLITS = (0, 3 * ATT_W, 3 * ATT_W + SSD_W, 3 * ATT_W + SSD_W + CONV_DIM)


def _inproj_kernel(x_ref, g_ref, w_ref, wdt_ref, qkv_ref, z_ref, xbc_ref, dt_ref):
    h = _rms(x_ref[...], g_ref[...]).astype(BF16)
    nt = (((1,), (1,)), ((), ()))
    for out, lo, hi in zip((qkv_ref, z_ref, xbc_ref), IN_SPLITS[:-1], IN_SPLITS[1:]):
        out[...] = lax.dot_general(h, w_ref[lo:hi, :], nt, preferred_element_type=F32)
    dt_ref[...] = lax.dot_general(h, wdt_ref[...], nt, preferred_element_type=F32)


def _inproj(x, g_mix, w_t, wdt, tm=256):
    t = x.shape[0]
    assert t % tm == 0
    row = lambda n: pl.BlockSpec((tm, n), lambda i: (i, 0))
    return pl.pallas_call(
        _inproj_kernel,
        grid=(t // tm,),
        in_specs=[row(D_MODEL), _const_spec((1, D_MODEL)), _const_spec(w_t.shape), _const_spec(wdt.shape)],
        out_specs=[row(3 * ATT_W), row(SSD_W), row(CONV_DIM), row(LANES)],
        out_shape=[jax.ShapeDtypeStruct((t, 3 * ATT_W), F32), jax.ShapeDtypeStruct((t, SSD_W), F32),
                   jax.ShapeDtypeStruct((t, CONV_DIM), F32), jax.ShapeDtypeStruct((t, LANES), F32)],
        compiler_params=pltpu.CompilerParams(dimension_semantics=("parallel",),
                                             vmem_limit_bytes=VMEM_LIMIT),
        name="inproj",
    )(x, g_mix, w_t, wdt)


def _tail_kernel(x_ref, att_ref, ssd_ref, gatt_ref, woa_ref, wos_ref, gffn_ref,
                 wg_ref, wu_ref, wd_ref, gfin_ref, y_ref, *, final_norm):
    an = _rms(att_ref[...], gatt_ref[...]).astype(BF16)
    mix = jnp.dot(an, woa_ref[...], preferred_element_type=F32)
    mix = mix + jnp.dot(ssd_ref[...].astype(BF16), wos_ref[...], preferred_element_type=F32)
    x1 = x_ref[...] + mix
    h2 = _rms(x1, gffn_ref[...]).astype(BF16)
    gate = jnp.dot(h2, wg_ref[...], preferred_element_type=F32)
    up = jnp.dot(h2, wu_ref[...], preferred_element_type=F32)
    act = (gate * jax.nn.sigmoid(gate) * up).astype(BF16)
    x2 = x1 + jnp.dot(act, wd_ref[...], preferred_element_type=F32)
    y_ref[...] = _rms(x2, gfin_ref[...]) if final_norm else x2


def _tail(x, att, ssd, g_att, woa, wos, g_ffn, wg, wu, wd, g_fin, final_norm, tm=256):
    t = x.shape[0]
    assert t % tm == 0
    row = lambda n: pl.BlockSpec((tm, n), lambda i: (i, 0))
    consts = [g_att, woa, wos, g_ffn, wg, wu, wd, g_fin]
    return pl.pallas_call(
        functools.partial(_tail_kernel, final_norm=final_norm),
        grid=(t // tm,),
        in_specs=[row(D_MODEL), row(ATT_W), row(SSD_W)] + [_const_spec(c.shape) for c in consts],
        out_specs=row(D_MODEL),
        out_shape=jax.ShapeDtypeStruct((t, D_MODEL), F32),
        compiler_params=pltpu.CompilerParams(dimension_semantics=("parallel",),
                                             vmem_limit_bytes=VMEM_LIMIT),
        name="tail",
    )(x, att, ssd, *consts)


TQ = QBLK * max(DILATIONS)
NEG_INF = float("-inf")
LOG2E = math.log2(math.e)


def _rel_bucket(dist):
    n = np.maximum(dist, 0)
    nf = np.maximum(n, 1).astype(np.float32)
    large = MAX_EXACT + (np.log(nf / MAX_EXACT) / math.log(WIN_MAX / MAX_EXACT)
                         * (N_BUCKETS - MAX_EXACT)).astype(np.int32)
    return np.where(n < MAX_EXACT, n, np.minimum(large, N_BUCKETS - 1))


def _band_buckets():
    qi = np.arange(QBLK)[:, None] + QBLK
    kk = np.arange(2 * QBLK)[None, :]
    dist = qi - kk
    valid = (dist >= 0) & (dist <= QBLK)
    return jnp.asarray(np.stack([np.where(valid, _rel_bucket(dist * d), -1) for d in DILATIONS]).astype(np.int32))


STAGE = 4
assert DILATIONS == (1, STAGE, STAGE * STAGE)


def _regroup(src_ref, tmp, dsts, scale=None):
    cast = (lambda x: x.astype(BF16)) if scale is None else (lambda x: (x * scale).astype(BF16))
    sub = TQ // STAGE
    for c in range(STAGE):
        dsts[0][c * sub:(c + 1) * sub, :] = cast(src_ref[c * sub:(c + 1) * sub, :])
    for r in range(STAGE):
        x = src_ref[pl.ds(r, sub, stride=STAGE), :]
        tmp[r] = x
        dsts[1][r * sub:(r + 1) * sub, :] = cast(x)
    for r in range(STAGE):
        for r2 in range(STAGE):
            res = r + STAGE * r2
            dsts[2][res * QBLK:(res + 1) * QBLK, :] = cast(tmp[r, pl.ds(r2, QBLK, stride=STAGE), :])


def _attn_prompt_kernel(tbl_ref, q_ref, k_ref, v_ref, bkt_ref, o_ref,
                        qd, kd, vd, tmp, bias, macc, nacc, dacc):
    hp = pl.program_id(0)
    t = pl.program_id(1)
    nb = len(DILATIONS)
    slot = t % 2

    @pl.when(t == 0)
    def _no_previous_tile():
        kd[:, 1] = jnp.zeros((nb, TQ, LANES), BF16)
        vd[:, 1] = jnp.zeros((nb, TQ, LANES), BF16)

    @pl.when(t == 0)
    def _build_bias():
        prev_half = lax.broadcasted_iota(jnp.int32, (1, 2 * QBLK), 1) < QBLK
        for b in range(len(DILATIONS)):
            bk = bkt_ref[b]
            for hh in range(2):
                acc = jnp.full((QBLK, 2 * QBLK), NEG_INF, F32)
                for u in range(N_BUCKETS):
                    acc = jnp.where(bk == u, tbl_ref[u, 2 * hp + hh], acc)
                acc = acc * LOG2E
                bias[b, 0, hh * QBLK:(hh + 1) * QBLK] = acc
                bias[b, 1, hh * QBLK:(hh + 1) * QBLK] = jnp.where(prev_half, NEG_INF, acc)

    _regroup(q_ref, tmp, [qd.at[b] for b in range(nb)], HEAD_DIM ** -0.5 * LOG2E)
    _regroup(k_ref, tmp, [kd.at[b, slot] for b in range(nb)])
    _regroup(v_ref, tmp, [vd.at[b, slot] for b in range(nb)])

    left = lax.broadcasted_iota(jnp.int32, (1, LANES), 1) < HEAD_DIM
    nt = (((1,), (1,)), ((), ()))

    for b, d in enumerate(DILATIONS):
        nbr = max(DILATIONS) // d

        def rows(start, size, d=d):
            return pl.ds(start, size) if d == 1 else pl.ds(start, size, stride=d)

        def block(j, carry, b=b, d=d, nbr=nbr, rows=rows):
            r = j // nbr
            m = j % nbr
            qs = r + d * QBLK * m
            gs = pl.multiple_of((r * nbr + m) * QBLK, QBLK)
            wrap = m == 0
            ps = pl.multiple_of((r * nbr + jnp.where(wrap, nbr - 1, m - 1)) * QBLK, QBLK)
            pslot = jnp.where(wrap, 1 - slot, slot)
            q = qd[b, pl.ds(gs, QBLK), :]
            kc = jnp.concatenate([kd[b, pslot, pl.ds(ps, QBLK), :], kd[b, slot, pl.ds(gs, QBLK), :]], axis=0)
            vc = jnp.concatenate([vd[b, pslot, pl.ds(ps, QBLK), :], vd[b, slot, pl.ds(gs, QBLK), :]], axis=0)
            first = jnp.logical_and(wrap, t == 0).astype(jnp.int32)
            zero = jnp.zeros_like(q)
            qq = jnp.concatenate([jnp.where(left, q, zero), jnp.where(left, zero, q)], axis=0)
            s = lax.dot_general(qq, kc, nt, preferred_element_type=F32) + bias[b, first]
            mx = jnp.max(s, axis=-1, keepdims=True)
            e = jnp.exp2(s - mx)
            den = jnp.sum(e, axis=-1, keepdims=True)
            res = jnp.dot(e.astype(BF16), vc, preferred_element_type=F32)
            nacc[b, rows(qs, QBLK), :] = jnp.where(left, res[:QBLK], res[QBLK:])
            dacc[b, rows(qs, QBLK), :] = jnp.where(left, den[:QBLK], den[QBLK:])
            macc[b, rows(qs, QBLK), :] = jnp.where(left, mx[:QBLK], mx[QBLK:])
            return carry

        lax.fori_loop(0, max(DILATIONS), block, 0, unroll=True)

    def merge(c, carry):
        sl = pl.ds(pl.multiple_of(c * QBLK, QBLK), QBLK)
        ms = [macc[b, sl, :] for b in range(len(DILATIONS))]
        top = jnp.maximum(jnp.maximum(ms[0], ms[1]), ms[2])
        ws = [jnp.exp2(mb - top) for mb in ms]
        num = sum(w * nacc[b, sl, :] for b, w in enumerate(ws))
        den = sum(w * dacc[b, sl, :] for b, w in enumerate(ws))
        o_ref[sl, :] = num / den
        return carry

    lax.fori_loop(0, TQ // QBLK, merge, 0)


def _attn_prompt(qkv, rel_bias):
    s = qkv.shape[0]
    assert s % TQ == 0
    nb = len(DILATIONS)
    hp_n = ATT_W // LANES
    tile = lambda f: pl.BlockSpec((TQ, LANES), f)
    return pl.pallas_call(
        _attn_prompt_kernel,
        grid=(hp_n, s // TQ),
        in_specs=[pl.BlockSpec(memory_space=pltpu.SMEM),
                  tile(lambda h, t: (t, h)),
                  tile(lambda h, t: (t, hp_n + h)),
                  tile(lambda h, t: (t, 2 * hp_n + h)),
                  _const_spec((nb, QBLK, 2 * QBLK))],
        out_specs=tile(lambda h, t: (t, h)),
        out_shape=jax.ShapeDtypeStruct((s, ATT_W), F32),
        scratch_shapes=[pltpu.VMEM((nb, TQ, LANES), BF16), pltpu.VMEM((nb, 2, TQ, LANES), BF16),
                        pltpu.VMEM((nb, 2, TQ, LANES), BF16), pltpu.VMEM((STAGE, TQ // STAGE, LANES), F32),
                        pltpu.VMEM((nb, 2, 2 * QBLK, 2 * QBLK), F32),
                        pltpu.VMEM((nb, TQ, LANES), F32), pltpu.VMEM((nb, TQ, LANES), F32),
                        pltpu.VMEM((nb, TQ, LANES), F32)],
        compiler_params=pltpu.CompilerParams(dimension_semantics=("parallel", "arbitrary"),
                                             vmem_limit_bytes=VMEM_LIMIT),
        name="attn_prompt",
    )(rel_bias, qkv, qkv, qkv, _band_buckets())


HPG = SSD_HEADS // N_GROUPS
GROUP_W = SSD_W // N_GROUPS
GROUP_WIN = 3 * LANES
GROUP_LO = tuple((g * GROUP_W) // LANES * LANES for g in range(N_GROUPS))
CARRY_ROW = 8


def _split(x, terms):
    parts = []
    for _ in range(terms):
        p = x.astype(BF16)
        parts.append(p)
        x = x - p.astype(F32)
    return parts


def _dot_exact_rhs(xs_terms, m):
    pieces = [p for x, terms in xs_terms for p in _split(x, terms)]
    out = jnp.dot(jnp.concatenate(pieces, axis=0) if len(pieces) > 1 else pieces[0], m,
                  preferred_element_type=F32)
    res, r = [], 0
    for x, terms in xs_terms:
        n = x.shape[0]
        res.append(sum(out[r + i * n:r + (i + 1) * n] for i in range(terms)))
        r += terms * n
    return res


def _dot_exact_lhs(m, x, terms=3):
    n = x.shape[1]
    out = jnp.dot(m, jnp.concatenate(_split(x, terms), axis=1), preferred_element_type=F32)
    return sum(out[:, i * n:(i + 1) * n] for i in range(terms))


def _softplus(x):
    return jnp.maximum(x, 0.0) + jnp.log1p(jnp.exp(-jnp.abs(x)))


def _silu(x):
    h = 0.5 * x
    return h + h * jnp.tanh(h)


def _ssd_consts():
    c = jnp.arange(SSD_W)
    head_of = (c // SSD_HEAD_DIM)[None, :] == jnp.arange(LANES)[:, None]
    group_of = (c // GROUP_W)[None, :] == jnp.arange(LANES)[:, None]
    return head_of.astype(BF16), group_of.astype(BF16), group_of.T.astype(BF16)


def _conv_silu(cbuf, xbc_ref, w_ref, b_ref, rows):
    out = []
    for c in range(CONV_DIM // LANES):
        cols = slice(c * LANES, (c + 1) * LANES)
        cbuf[c, CARRY_ROW:CARRY_ROW + rows, :] = xbc_ref[:, cols]
        acc = b_ref[:, cols]
        for j in range(CONV_W):
            acc = acc + cbuf[c, pl.ds(CARRY_ROW - (CONV_W - 1) + j, rows), :] * w_ref[j:j + 1, cols]
        out.append(_silu(acc))
    return out


def _drain(steps):
    try:
        while True:
            next(steps)
    except StopIteration as done:
        return done.value


def _interleave(*bodies):
    live = list(bodies)
    while live:
        for body in list(live):
            try:
                next(body)
            except StopIteration:
                live.remove(body)


def _ssd_ydiag(*args):
    return _drain(_ssd_ydiag_steps(*args))


def _ssd_ydiag_steps(xs, bm, cm, acs, dt, mask):
    cbs = [lax.dot_general(cm[g], bm[g], (((1,), (1,)), ((), ())), preferred_element_type=F32)
           for g in range(N_GROUPS)]
    acs_t = acs.T
    dt_t = dt.T
    left = lax.broadcasted_iota(jnp.int32, (1, LANES), 1) < SSD_HEAD_DIM
    pairs = []
    for pair in range(SSD_HEADS // 2):
        xpair = xs[:, pair * LANES:(pair + 1) * LANES]
        acc = None
        for hh in range(2):
            h = 2 * pair + hh
            seg = acs[:, h:h + 1] - acs_t[h:h + 1, :]
            dec = jnp.exp2(jnp.where(mask, seg, NEG_INF))
            w = (cbs[h // HPG] * dec * dt_t[h:h + 1, :]).astype(BF16)
            xh = jnp.where(left if hh == 0 else jnp.logical_not(left), xpair, 0.0).astype(BF16)
            part = jnp.dot(w, xh, preferred_element_type=F32)
            acc = part if acc is None else acc + part
        pairs.append(acc)
        if pair % 2 == 1:
            yield
    return jnp.concatenate(pairs, axis=1)


def _ssd_gate_norm(y, gate, gred, gexp, gout):
    y = y * gate
    ms = _dot_exact_rhs([(y * y, 2)], gred)[0] * (1.0 / GROUP_W)
    return y * _dot_exact_rhs([(lax.rsqrt(ms + EPS), 2)], gexp)[0] * gout


def _ssd_prompt_init(cbuf, state):
    @pl.when(pl.program_id(0) == 0)
    def _init():
        state[...] = jnp.zeros_like(state)
        cbuf[:, 0:CARRY_ROW, :] = jnp.zeros((CONV_DIM // LANES, CARRY_ROW, LANES), F32)


def _ssd_prompt_emit(st_ref, state):
    @pl.when(pl.program_id(0) == pl.num_programs(0) - 1)
    def _emit_state():
        for i in range(SSD_W // LANES):
            st_ref[i * LANES:(i + 1) * LANES, :] = state[:, i * LANES:(i + 1) * LANES].T


def _ssd_prompt_chunk(z_ref, xbc_ref, dt_ref, cw_ref, cb_ref, dtb_ref, alog_ref, dsk_ref, gout_ref,
                      hexp_ref, gexp_ref, gred_ref, y_ref, cbuf, state, yoff):
    cl = SSD_CHUNK
    xc = _conv_silu(cbuf, xbc_ref, cw_ref, cb_ref, cl)
    cbuf[:, CARRY_ROW - (CONV_W - 1):CARRY_ROW, :] = cbuf[:, CARRY_ROW + cl - (CONV_W - 1):CARRY_ROW + cl, :]
    nx = SSD_W // LANES
    xs = jnp.concatenate(xc[:nx], axis=1)
    bm = [b.astype(BF16) for b in xc[nx:nx + N_GROUPS]]
    cm = [b.astype(BF16) for b in xc[nx + N_GROUPS:]]
    yield

    li = lax.broadcasted_iota(jnp.int32, (cl, cl), 0)
    si = lax.broadcasted_iota(jnp.int32, (cl, cl), 1)
    causal = li >= si

    dt = _softplus(dt_ref[...] + dtb_ref[...])
    acs = _dot_exact_lhs(causal.astype(BF16), dt * (-jnp.exp(alog_ref[...]) * LOG2E))
    hexp = hexp_ref[...]
    e_in, e_end = _dot_exact_rhs([(jnp.exp2(acs), 3), (jnp.exp2(acs[cl - 1:cl, :] - acs) * dt, 1)], hexp)
    xw = (xs * e_end).astype(BF16)
    chunk_decay = e_in[cl - 1:cl]
    yield

    gmask = lax.broadcasted_iota(jnp.int32, (1, SSD_W), 1) // GROUP_W
    for g in range(N_GROUPS):
        lo = GROUP_LO[g]
        win = slice(lo, lo + GROUP_WIN)
        inside = gmask[:, win] == g
        yo = jnp.dot(cm[g], state[:, win].astype(BF16), preferred_element_type=F32)
        yoff[:, win] = jnp.where(inside, yo, 0.0 if g == 0 else yoff[:, win])
    yield
    for g in range(N_GROUPS):
        lo = GROUP_LO[g]
        win = slice(lo, lo + GROUP_WIN)
        inside = gmask[:, win] == g
        upd = lax.dot_general(bm[g], xw[:, win], (((0,), (0,)), ((), ())), preferred_element_type=F32)
        old = state[:, win]
        state[:, win] = jnp.where(inside, old * chunk_decay[:, win] + upd, old)
    yield

    ydiag = yield from _ssd_ydiag_steps(xs, bm, cm, acs, dt, causal)
    y_ref[...] = _ssd_gate_norm(ydiag + yoff[...] * e_in + xs * dsk_ref[...], _silu(z_ref[...]),
                                gred_ref[...], gexp_ref[...], gout_ref[...])


N_SSD_IN = 12


def _ssd_prompt_kernel(*refs):
    ins, (y_ref, st_ref, cbuf, state, yoff) = refs[:N_SSD_IN], refs[N_SSD_IN:]
    _ssd_prompt_init(cbuf, state)
    _drain(_ssd_prompt_chunk(*ins, y_ref, cbuf, state, yoff))
    _ssd_prompt_emit(st_ref, state)


def _ssd_prompt_call(z, xbc, dt, conv_w, conv_b, dt_bias, a_log, dskip_x, g_out):
    s = z.shape[0]
    assert s % SSD_CHUNK == 0
    hexp, gexp, gred = _ssd_consts()
    row = lambda n: pl.BlockSpec((SSD_CHUNK, n), lambda i: (i, 0))
    consts = [conv_w, conv_b, dt_bias, a_log, dskip_x, g_out, hexp, gexp, gred]
    return dict(
        steps=s // SSD_CHUNK,
        operands=[z, xbc, dt] + consts,
        in_specs=[row(SSD_W), row(CONV_DIM), row(LANES)] + [_const_spec(a.shape) for a in consts],
        out_specs=[row(SSD_W), pl.BlockSpec((SSD_W, D_STATE), lambda i: (0, 0))],
        out_shape=[jax.ShapeDtypeStruct((s, SSD_W), F32), jax.ShapeDtypeStruct((SSD_W, D_STATE), F32)],
        scratch=[pltpu.VMEM((CONV_DIM // LANES, CARRY_ROW + SSD_CHUNK, LANES), F32),
                 pltpu.VMEM((D_STATE, SSD_W), F32), pltpu.VMEM((SSD_CHUNK, SSD_W), F32)])


def _run(kernel_fn, name, *calls, vmem_limit=VMEM_LIMIT):
    steps = calls[0]["steps"]
    assert all(c["steps"] == steps for c in calls)
    cat = lambda key: [x for c in calls for x in c[key]]
    return pl.pallas_call(
        kernel_fn, grid=(steps,), in_specs=cat("in_specs"), out_specs=cat("out_specs"),
        out_shape=cat("out_shape"), scratch_shapes=cat("scratch"),
        compiler_params=pltpu.CompilerParams(dimension_semantics=("arbitrary",), vmem_limit_bytes=vmem_limit),
        name=name)(*cat("operands"))


def _ssd_prompt(*args):
    return _run(_ssd_prompt_kernel, "ssd_prompt", _ssd_prompt_call(*args))


DEC_SEQ = 4
NEW_ROWS = 8


def _branch_count(dist):
    return sum(((dist % d == 0) & (dist <= QBLK * d)).astype(np.int32) for d in DILATIONS)


def _sample_bias_tables(wb):
    t = np.arange(NEW_ROWS)[:, None]
    dist = wb + t - np.arange(wb)[None, :]
    cnt = np.where(t < DEC_SEQ, _branch_count(dist), 0)
    bkt = np.where(cnt > 0, _rel_bucket(dist), -1).astype(np.int32)
    ladd = np.log(np.maximum(cnt, 1).astype(np.float32))
    u = np.arange(LANES)[None, :]
    dnew = t - u
    cnew = np.where((t < DEC_SEQ) & (dnew > 0), _branch_count(dnew),
                    np.where((t < DEC_SEQ) & (dnew == 0), len(DILATIONS), 0))
    bkt_new = np.where(cnew > 0, _rel_bucket(dnew), -1).astype(np.int32)
    ladd_new = np.log(np.maximum(cnew, 1).astype(np.float32))
    return tuple(jnp.asarray(a) for a in (bkt, ladd, bkt_new, ladd_new))


def _attn_sample_init(tbl_ref, bkt_ref, ladd_ref, bktn_ref, laddn_ref, bias, bias_new):
    @pl.when(pl.program_id(0) == 0)
    def _build_bias():
        for h in range(ATT_HEADS):
            for src, add, dst in ((bkt_ref, ladd_ref, bias), (bktn_ref, laddn_ref, bias_new)):
                bk = src[...]
                acc = jnp.full(bk.shape, NEG_INF, F32)
                for u in range(N_BUCKETS):
                    acc = jnp.where(bk == u, tbl_ref[u, h], acc)
                dst[h // 2, (h % 2) * NEW_ROWS:(h % 2 + 1) * NEW_ROWS, :] = (acc + add[...]) * LOG2E


N_ATTN_S_IN = 8


def _attn_sample_kernel(tbl_ref, q_ref, kt_ref, vt_ref, bkt_ref, ladd_ref, bktn_ref, laddn_ref,
                        o_ref, bias, bias_new):
    _attn_sample_init(tbl_ref, bkt_ref, ladd_ref, bktn_ref, laddn_ref, bias, bias_new)
    whole = lambda hp: (kt_ref.at[0], vt_ref.at[0], hp * LANES)
    _drain(_attn_sample_seq(q_ref, whole, o_ref, bias, bias_new))


RING = 3
RING_VMEM_LIMIT = 58 * 1024 * 1024


def _ring_kv(k_hbm, v_hbm, kring, vring, sems):
    i = pl.program_id(0)
    n = pl.num_programs(0)

    def copies(seq, slot):
        return (pltpu.make_async_copy(k_hbm.at[seq], kring.at[slot], sems.at[0, slot]),
                pltpu.make_async_copy(v_hbm.at[seq], vring.at[slot], sems.at[1, slot]))

    def start(seq, slot):
        for c in copies(jnp.minimum(seq, n - 1), slot):
            c.start()

    @pl.when(i == 0)
    def _prime():
        for s in range(RING - 1):
            start(s, s)

    start(i + RING - 1, (i + RING - 1) % RING)
    slot = i % RING
    for c in copies(i, slot):
        c.wait()

    def finish():
        @pl.when(i == n - 1)
        def _drain_tail():
            for s in range(RING - 1):
                for c in copies(0, (n + s) % RING):
                    c.wait()

    return (lambda hp: (kring.at[slot], vring.at[slot], hp * LANES)), finish


def _attn_sample_seq(q_ref, kv_for_pair, o_ref, bias, bias_new):
    left = lax.broadcasted_iota(jnp.int32, (1, LANES), 1) < HEAD_DIM
    nt = (((1,), (1,)), ((), ()))
    scale = HEAD_DIM ** -0.5 * LOG2E
    outs = []
    for hp in range(ATT_W // LANES):
        cols = slice(hp * LANES, (hp + 1) * LANES)
        k_ref, v_ref, row0 = kv_for_pair(hp)
        kv_rows = slice(row0, row0 + LANES)
        q = q_ref[0, :, cols] * scale
        fill = jnp.zeros((LANES - NEW_ROWS, LANES), F32)
        knew = jnp.concatenate([q_ref[0, :, ATT_W + hp * LANES:ATT_W + (hp + 1) * LANES], fill], axis=0)
        vnew = jnp.concatenate([q_ref[0, :, 2 * ATT_W + hp * LANES:2 * ATT_W + (hp + 1) * LANES], fill], axis=0)
        qq = jnp.concatenate([jnp.where(left, q, 0.0), jnp.where(left, 0.0, q)], axis=0).astype(BF16)
        s = jnp.dot(qq, k_ref[kv_rows, :].astype(BF16), preferred_element_type=F32) + bias[hp]
        sn = lax.dot_general(qq, knew.astype(BF16), nt, preferred_element_type=F32) + bias_new[hp]
        m = jnp.maximum(jnp.max(s, axis=-1, keepdims=True), jnp.max(sn, axis=-1, keepdims=True))
        m = jnp.where(m == NEG_INF, 0.0, m)
        p = jnp.exp2(s - m)
        pn = jnp.exp2(sn - m)
        den = jnp.sum(p, axis=-1, keepdims=True) + jnp.sum(pn, axis=-1, keepdims=True)
        r = lax.dot_general(p.astype(BF16), v_ref[kv_rows, :].astype(BF16), nt, preferred_element_type=F32)
        r = r + jnp.dot(pn.astype(BF16), vnew.astype(BF16), preferred_element_type=F32)
        o = r / jnp.where(den == 0.0, 1.0, den)
        outs.append(jnp.where(left, o[:NEW_ROWS], o[NEW_ROWS:]))
        yield
    o_ref[0] = jnp.concatenate(outs, axis=1)[:DEC_SEQ]


def _attn_sample_call(qkv, cache_kt, cache_vt, rel_bias, ring=False):
    db, _, wb = cache_kt.shape
    assert wb >= WIN_MAX and qkv.shape[0] == db * DEC_SEQ
    q8 = jnp.pad(qkv.reshape(db, DEC_SEQ, 3 * ATT_W), ((0, 0), (0, NEW_ROWS - DEC_SEQ), (0, 0)))
    consts = list(_sample_bias_tables(wb))
    hp_n = ATT_W // LANES
    seq = lambda n, m: pl.BlockSpec((1, n, m), lambda b: (b, 0, 0))
    kv_spec = pl.BlockSpec(memory_space=pl.ANY) if ring else seq(ATT_W, wb)
    scratch = [pltpu.VMEM((hp_n, 2 * NEW_ROWS, wb), F32), pltpu.VMEM((hp_n, 2 * NEW_ROWS, LANES), F32)]
    if ring:
        scratch += [pltpu.VMEM((RING, ATT_W, wb), F32), pltpu.VMEM((RING, ATT_W, wb), F32),
                    pltpu.SemaphoreType.DMA((2, RING))]
    return dict(
        steps=db,
        operands=[rel_bias, q8, cache_kt, cache_vt] + consts,
        in_specs=[pl.BlockSpec(memory_space=pltpu.SMEM), seq(NEW_ROWS, 3 * ATT_W), kv_spec, kv_spec]
                 + [_const_spec(c.shape) for c in consts],
        out_specs=[seq(DEC_SEQ, ATT_W)],
        out_shape=[jax.ShapeDtypeStruct((db, DEC_SEQ, ATT_W), F32)],
        scratch=scratch)


def _attn_sample(qkv, cache_kt, cache_vt, rel_bias):
    out, = _run(_attn_sample_kernel, "attn_sample", _attn_sample_call(qkv, cache_kt, cache_vt, rel_bias))
    return out.reshape(-1, ATT_W)


def _ssd_prompt_attn_sample_kernel(*refs):
    a = N_SSD_IN + N_ATTN_S_IN
    ssd_in = refs[:N_SSD_IN]
    tbl_ref, q_ref, kt_ref, vt_ref, bkt_ref, ladd_ref, bktn_ref, laddn_ref = refs[N_SSD_IN:a]
    y_ref, st_ref, o_ref, cbuf, state, yoff, bias, bias_new, kring, vring, sems = refs[a:]
    _ssd_prompt_init(cbuf, state)
    _attn_sample_init(tbl_ref, bkt_ref, ladd_ref, bktn_ref, laddn_ref, bias, bias_new)
    kv_for_pair, finish_ring = _ring_kv(kt_ref, vt_ref, kring, vring, sems)
    _interleave(_ssd_prompt_chunk(*ssd_in, y_ref, cbuf, state, yoff),
                _attn_sample_seq(q_ref, kv_for_pair, o_ref, bias, bias_new))
    finish_ring()
    _ssd_prompt_emit(st_ref, state)


def _ssd_prompt_attn_sample(ssd_args, attn_args):
    y, st, att = _run(_ssd_prompt_attn_sample_kernel, "ssd_prompt_attn_sample",
                      _ssd_prompt_call(*ssd_args), _attn_sample_call(*attn_args, ring=True),
                      vmem_limit=RING_VMEM_LIMIT)
    return y, st, att.reshape(-1, ATT_W)


SEQ_ROWS = 8
SEQ_PER_STEP = 8


def _ssd_sample_kernel(z_ref, xp_ref, dt_ref, st_ref, cw_ref, cb_ref, dtb_ref, alog_ref, dsk_ref, gout_ref,
                       hexp_ref, gexp_ref, gred_ref, y_ref, sto_ref, cbuf, yoff):
    rows = SEQ_PER_STEP * SEQ_ROWS
    cbuf[:, 0:CARRY_ROW, :] = jnp.zeros((CONV_DIM // LANES, CARRY_ROW, LANES), F32)
    xc = _conv_silu(cbuf, xp_ref, cw_ref, cb_ref, rows)
    nx = SSD_W // LANES
    xs = jnp.concatenate(xc[:nx], axis=1)
    bmf = xc[nx:nx + N_GROUPS]
    bm = [b.astype(BF16) for b in bmf]
    cm = [b.astype(BF16) for b in xc[nx + N_GROUPS:]]

    li = lax.broadcasted_iota(jnp.int32, (rows, rows), 0)
    si = lax.broadcasted_iota(jnp.int32, (rows, rows), 1)
    same = (li // SEQ_ROWS) == (si // SEQ_ROWS)
    mask = same & (li >= si)
    rowi = lax.broadcasted_iota(jnp.int32, (rows, 1), 0)
    is_token = (rowi % SEQ_ROWS) >= SEQ_ROWS - DEC_SEQ

    dt = jnp.where(is_token, _softplus(dt_ref[...] + dtb_ref[...]), 0.0)
    da = dt * (-jnp.exp(alog_ref[...]) * LOG2E)
    acs = _dot_exact_lhs(mask.astype(BF16), da)
    acs_end = _dot_exact_lhs(same.astype(BF16), da)
    hexp = hexp_ref[...]
    e_in, e_end, e_all = _dot_exact_rhs(
        [(jnp.exp2(acs), 2), (jnp.exp2(acs_end - acs) * dt, 1), (jnp.exp2(acs_end), 3)], hexp)
    xw = xs * e_end

    sub = rowi % SEQ_ROWS
    pieces = [p.astype(F32) for p in _split(e_all, 3)]
    dstack = jnp.where(sub == 0, pieces[0], jnp.where(sub == 1, pieces[1], jnp.where(sub == 2, pieces[2], 0.0)))
    ones = jnp.ones((SEQ_ROWS, D_STATE), BF16)
    gmask = lax.broadcasted_iota(jnp.int32, (1, SSD_W), 1) // GROUP_W
    tn = (((0,), (0,)), ((), ()))
    nt = (((1,), (1,)), ((), ()))
    for i in range(SEQ_PER_STEP):
        rs = slice(i * SEQ_ROWS, (i + 1) * SEQ_ROWS)
        for g in range(N_GROUPS):
            lo = GROUP_LO[g]
            win = slice(lo, lo + GROUP_WIN)
            inside = gmask[:, win] == g
            yo = lax.dot_general(cm[g][rs], st_ref[i, win, :].astype(BF16), nt, preferred_element_type=F32)
            yoff[rs, win] = jnp.where(inside, yo, 0.0 if g == 0 else yoff[rs, win])
        xstack = jnp.concatenate([jnp.where(gmask == g, xw[rs, :], 0.0) for g in range(N_GROUPS)], axis=0)
        bstack = jnp.concatenate([bmf[g][rs] for g in range(N_GROUPS)], axis=0)
        upd = lax.dot_general(xstack.astype(BF16), bstack.astype(BF16), tn, preferred_element_type=F32)
        dcol = lax.dot_general(dstack[rs, :].astype(BF16), ones, tn, preferred_element_type=F32)
        sto_ref[i] = st_ref[i] * dcol + upd

    ydiag = _ssd_ydiag(xs, bm, cm, acs, dt, mask)
    y_ref[...] = _ssd_gate_norm(ydiag + yoff[...] * e_in + xs * dsk_ref[...], _silu(z_ref[...]),
                                gred_ref[...], gexp_ref[...], gout_ref[...])


def _ssd_sample(z, xbc, dt, conv_state, ssm_state, conv_w, conv_b, dt_bias, a_log, dskip_x, g_out):
    db = conv_state.shape[0]
    assert db % SEQ_PER_STEP == 0 and z.shape[0] == db * DEC_SEQ
    lead = SEQ_ROWS - DEC_SEQ

    def padded(a):
        a = a.reshape(db, DEC_SEQ, a.shape[-1])
        return jnp.pad(a, ((0, 0), (lead, 0), (0, 0))).reshape(db * SEQ_ROWS, a.shape[-1])

    xp = jnp.concatenate([jnp.zeros((db, lead - (CONV_W - 1), CONV_DIM), F32), conv_state,
                          xbc.reshape(db, DEC_SEQ, CONV_DIM)], axis=1).reshape(db * SEQ_ROWS, CONV_DIM)
    hexp, gexp, gred = _ssd_consts()
    rows = SEQ_PER_STEP * SEQ_ROWS
    row = lambda n: pl.BlockSpec((rows, n), lambda i: (i, 0))
    st_spec = pl.BlockSpec((SEQ_PER_STEP, SSD_W, D_STATE), lambda i: (i, 0, 0))
    consts = [conv_w, conv_b, dt_bias, a_log, dskip_x, g_out, hexp, gexp, gred]
    y, st = pl.pallas_call(
        _ssd_sample_kernel,
        grid=(db // SEQ_PER_STEP,),
        in_specs=[row(SSD_W), row(CONV_DIM), row(LANES), st_spec] + [_const_spec(a.shape) for a in consts],
        out_specs=[row(SSD_W), st_spec],
        out_shape=[jax.ShapeDtypeStruct((db * SEQ_ROWS, SSD_W), F32),
                   jax.ShapeDtypeStruct((db, SSD_W, D_STATE), F32)],
        scratch_shapes=[pltpu.VMEM((CONV_DIM // LANES, CARRY_ROW + rows, LANES), F32),
                        pltpu.VMEM((rows, SSD_W), F32)],
        compiler_params=pltpu.CompilerParams(dimension_semantics=("parallel",),
                                             vmem_limit_bytes=VMEM_LIMIT),
        name="ssd_sample",
    )(padded(z), xp, padded(dt), ssm_state, *consts)
    y = y.reshape(db, SEQ_ROWS, SSD_W)[:, lead:].reshape(db * DEC_SEQ, SSD_W)
    return y, st


def kernel(x_prompt, x_sample, cache_k, cache_v, state_conv, state_ssm, rel_bias, g_mix, w_in, conv_w, conv_b,
           dt_bias, a_log, d_skip, g_attn_out, g_ssd_out, w_out, g_ffn, w_gate, w_up, w_down, g_final):
    depth = w_in.shape[0]
    bp, sp, _ = x_prompt.shape
    db, ds, _ = x_sample.shape
    assert bp == 1 and ds == DEC_SEQ and sp >= WIN_MAX
    xp = x_prompt.reshape(bp * sp, D_MODEL)
    xs = x_sample.reshape(db * ds, D_MODEL)
    pad_heads = lambda v: jnp.pad(v, (0, LANES - SSD_HEADS))[None]
    outs = [[] for _ in range(8)]
    for l in range(depth):
        w_t = w_in[l].T.astype(BF16)
        wdt = jnp.pad(w_t[IN_SPLITS[-1]:], ((0, LANES - SSD_HEADS), (0, 0)))
        ssd_prm = (conv_w[l], conv_b[l][None], pad_heads(dt_bias[l]), pad_heads(a_log[l]),
                   jnp.repeat(d_skip[l], SSD_HEAD_DIM)[None], g_ssd_out[l][None])
        tail_prm = (g_attn_out[l][None], w_out[l][:ATT_W].astype(BF16), w_out[l][ATT_W:].astype(BF16),
                    g_ffn[l][None], w_gate[l].astype(BF16), w_up[l].astype(BF16), w_down[l].astype(BF16),
                    g_final[None])
        last = l == depth - 1

        qkv_p, z_p, xbc_p, dt_p = _inproj(xp, g_mix[l][None], w_t, wdt)
        qkv_s, z_s, xbc_s, dt_s = _inproj(xs, g_mix[l][None], w_t, wdt)
        att_p = _attn_prompt(qkv_p, rel_bias)
        feature_major = lambda c: jnp.transpose(c, (0, 2, 3, 1)).reshape(db, ATT_W, c.shape[1])
        ssd_args = (z_p, xbc_p, dt_p, *ssd_prm)
        attn_args = (qkv_s, feature_major(cache_k[l]), feature_major(cache_v[l]), rel_bias)
        if sp // SSD_CHUNK == db:
            ssd_p, st_p, att_s = _ssd_prompt_attn_sample(ssd_args, attn_args)
        else:
            ssd_p, st_p = _ssd_prompt(*ssd_args)
            att_s = _attn_sample(*attn_args)
        xp = _tail(xp, att_p, ssd_p, *tail_prm, final_norm=last)
        ssd_s, st_s = _ssd_sample(z_s, xbc_s, dt_s, state_conv[l], state_ssm[l].reshape(db, SSD_W, D_STATE),
                                  *ssd_prm)
        xs = _tail(xs, att_s, ssd_s, *tail_prm, final_norm=last)

        heads = lambda a: a.reshape(a.shape[0], a.shape[1], ATT_HEADS, HEAD_DIM)
        kv_p = qkv_p.reshape(bp, sp, 3 * ATT_W)[:, sp - WIN_MAX:]
        kv_s = qkv_s.reshape(db, ds, 3 * ATT_W)
        outs[0].append(heads(kv_p[..., ATT_W:2 * ATT_W]))
        outs[1].append(heads(kv_p[..., 2 * ATT_W:]))
        outs[2].append(xbc_p.reshape(bp, sp, CONV_DIM)[:, sp - (CONV_W - 1):])
        outs[3].append(st_p.reshape(bp, SSD_HEADS, SSD_HEAD_DIM, D_STATE))
        outs[4].append(heads(kv_s[..., ATT_W:2 * ATT_W]))
        outs[5].append(heads(kv_s[..., 2 * ATT_W:]))
        outs[6].append(xbc_s.reshape(db, ds, CONV_DIM)[:, ds - (CONV_W - 1):])
        outs[7].append(st_s.reshape(db, SSD_HEADS, SSD_HEAD_DIM, D_STATE))
    return (xp.reshape(bp, sp, D_MODEL), xs.reshape(db, ds, D_MODEL)) + tuple(jnp.stack(o) for o in outs)
```

```python
import functools
import math

import jax
import jax.numpy as jnp
import numpy as np
from jax import lax
from jax.experimental import pallas as pl
from jax.experimental.pallas import tpu as pltpu

F32 = jnp.float32
BF16 = jnp.bfloat16

D_MODEL = 1024
ATT_HEADS = 12
HEAD_DIM = 64
ATT_W = ATT_HEADS * HEAD_DIM
DILATIONS = (1, 4, 16)
QBLK = 128
WIN_MAX = 2048
N_BUCKETS = 32
MAX_EXACT = N_BUCKETS // 2
SSD_HEADS = 20
SSD_HEAD_DIM = 64
SSD_W = SSD_HEADS * SSD_HEAD_DIM
N_GROUPS = 4
D_STATE = 128
CONV_W = 4
CONV_DIM = SSD_W + 2 * N_GROUPS * D_STATE
SSD_CHUNK = 128
EPS = 1e-6

LANES = 128
VMEM_LIMIT = 48 * 1024 * 1024


def _rms(x, g):
    return x * lax.rsqrt(jnp.mean(x * x, axis=-1, keepdims=True) + EPS) * g


def _const_spec(shape):
    nd = len(shape)
    return pl.BlockSpec(shape, lambda *_: (0,) * nd, pipeline_mode=pl.Buffered(1))


IN_SPLITS = (0, 3 * ATT_W, 3 * ATT_W + SSD_W, 3 * ATT_W + SSD_W + CONV_DIM)


def _inproj_kernel(x_ref, g_ref, w_ref, wdt_ref, qkv_ref, z_ref, xbc_ref, dt_ref):
    h = _rms(x_ref[...], g_ref[...]).astype(BF16)
    nt = (((1,), (1,)), ((), ()))
    for out, lo, hi in zip((qkv_ref, z_ref, xbc_ref), IN_SPLITS[:-1], IN_SPLITS[1:]):
        out[...] = lax.dot_general(h, w_ref[lo:hi, :], nt, preferred_element_type=F32)
    dt_ref[...] = lax.dot_general(h, wdt_ref[...], nt, preferred_element_type=F32)


def _inproj(x, g_mix, w_t, wdt):
    t = x.shape[0]
    tm = 512 if t % 512 == 0 and t > 512 else 256
    assert t % tm == 0
    row = lambda n: pl.BlockSpec((tm, n), lambda i: (i, 0))
    return pl.pallas_call(
        _inproj_kernel,
        grid=(t // tm,),
        in_specs=[row(D_MODEL), _const_spec((1, D_MODEL)), _const_spec(w_t.shape), _const_spec(wdt.shape)],
        out_specs=[row(3 * ATT_W), row(SSD_W), row(CONV_DIM), row(LANES)],
        out_shape=[jax.ShapeDtypeStruct((t, 3 * ATT_W), F32), jax.ShapeDtypeStruct((t, SSD_W), F32),
                   jax.ShapeDtypeStruct((t, CONV_DIM), F32), jax.ShapeDtypeStruct((t, LANES), F32)],
        compiler_params=pltpu.CompilerParams(dimension_semantics=("parallel",),
                                             vmem_limit_bytes=VMEM_LIMIT),
        name="inproj",
    )(x, g_mix, w_t, wdt)


def _tail_kernel(x_ref, att_ref, ssd_ref, gatt_ref, woa_ref, wos_ref, gffn_ref,
                 wg_ref, wu_ref, wd_ref, gfin_ref, y_ref, *, final_norm):
    an = _rms(att_ref[...], gatt_ref[...]).astype(BF16)
    mix = jnp.dot(an, woa_ref[...], preferred_element_type=F32)
    mix = mix + jnp.dot(ssd_ref[...].astype(BF16), wos_ref[...], preferred_element_type=F32)
    x1 = x_ref[...] + mix
    h2 = _rms(x1, gffn_ref[...]).astype(BF16)
    gate = jnp.dot(h2, wg_ref[...], preferred_element_type=F32)
    up = jnp.dot(h2, wu_ref[...], preferred_element_type=F32)
    act = (gate * jax.nn.sigmoid(gate) * up).astype(BF16)
    x2 = x1 + jnp.dot(act, wd_ref[...], preferred_element_type=F32)
    y_ref[...] = _rms(x2, gfin_ref[...]) if final_norm else x2


def _tail(x, att, ssd, g_att, woa, wos, g_ffn, wg, wu, wd, g_fin, final_norm, tm=256):
    t = x.shape[0]
    assert t % tm == 0
    row = lambda n: pl.BlockSpec((tm, n), lambda i: (i, 0))
    consts = [g_att, woa, wos, g_ffn, wg, wu, wd, g_fin]
    return pl.pallas_call(
        functools.partial(_tail_kernel, final_norm=final_norm),
        grid=(t // tm,),
        in_specs=[row(D_MODEL), row(ATT_W), row(SSD_W)] + [_const_spec(c.shape) for c in consts],
        out_specs=row(D_MODEL),
        out_shape=jax.ShapeDtypeStruct((t, D_MODEL), F32),
        compiler_params=pltpu.CompilerParams(dimension_semantics=("parallel",),
                                             vmem_limit_bytes=VMEM_LIMIT),
        name="tail",
    )(x, att, ssd, *consts)


TQ = QBLK * max(DILATIONS)
NEG_INF = float("-inf")
LOG2E = math.log2(math.e)


def _rel_bucket(dist):
    n = np.maximum(dist, 0)
    nf = np.maximum(n, 1).astype(np.float32)
    large = MAX_EXACT + (np.log(nf / MAX_EXACT) / math.log(WIN_MAX / MAX_EXACT)
                         * (N_BUCKETS - MAX_EXACT)).astype(np.int32)
    return np.where(n < MAX_EXACT, n, np.minimum(large, N_BUCKETS - 1))


def _band_buckets():
    qi = np.arange(QBLK)[:, None] + QBLK
    kk = np.arange(2 * QBLK)[None, :]
    dist = qi - kk
    valid = (dist >= 0) & (dist <= QBLK)
    return jnp.asarray(np.stack([np.where(valid, _rel_bucket(dist * d), -1) for d in DILATIONS]).astype(np.int32))


STAGE = 4
assert DILATIONS == (1, STAGE, STAGE * STAGE)


def _regroup(src_ref, tmp, dsts, scale=None):
    cast = (lambda x: x.astype(BF16)) if scale is None else (lambda x: (x * scale).astype(BF16))
    sub = TQ // STAGE
    for c in range(STAGE):
        dsts[0][c * sub:(c + 1) * sub, :] = cast(src_ref[c * sub:(c + 1) * sub, :])
    for r in range(STAGE):
        x = src_ref[pl.ds(r, sub, stride=STAGE), :]
        tmp[r] = x
        dsts[1][r * sub:(r + 1) * sub, :] = cast(x)
    for r in range(STAGE):
        for r2 in range(STAGE):
            res = r + STAGE * r2
            dsts[2][res * QBLK:(res + 1) * QBLK, :] = cast(tmp[r, pl.ds(r2, QBLK, stride=STAGE), :])


def _attn_prompt_kernel(tbl_ref, q_ref, k_ref, v_ref, bkt_ref, o_ref,
                        qd, kd, vd, tmp, bias, macc, nacc, dacc):
    hp = pl.program_id(0)
    t = pl.program_id(1)
    nb = len(DILATIONS)
    slot = t % 2

    @pl.when(t == 0)
    def _no_previous_tile():
        kd[:, 1] = jnp.zeros((nb, TQ, LANES), BF16)
        vd[:, 1] = jnp.zeros((nb, TQ, LANES), BF16)

    @pl.when(t == 0)
    def _build_bias():
        prev_half = lax.broadcasted_iota(jnp.int32, (1, 2 * QBLK), 1) < QBLK
        for b in range(len(DILATIONS)):
            bk = bkt_ref[b]
            for hh in range(2):
                acc = jnp.full((QBLK, 2 * QBLK), NEG_INF, F32)
                for u in range(N_BUCKETS):
                    acc = jnp.where(bk == u, tbl_ref[u, 2 * hp + hh], acc)
                acc = acc * LOG2E
                bias[b, 0, hh * QBLK:(hh + 1) * QBLK] = acc
                bias[b, 1, hh * QBLK:(hh + 1) * QBLK] = jnp.where(prev_half, NEG_INF, acc)

    _regroup(q_ref, tmp, [qd.at[b] for b in range(nb)], HEAD_DIM ** -0.5 * LOG2E)
    _regroup(k_ref, tmp, [kd.at[b, slot] for b in range(nb)])
    _regroup(v_ref, tmp, [vd.at[b, slot] for b in range(nb)])

    left = lax.broadcasted_iota(jnp.int32, (1, LANES), 1) < HEAD_DIM
    nt = (((1,), (1,)), ((), ()))

    for b, d in enumerate(DILATIONS):
        nbr = max(DILATIONS) // d

        def rows(start, size, d=d):
            return pl.ds(start, size) if d == 1 else pl.ds(start, size, stride=d)

        def block(j, carry, b=b, d=d, nbr=nbr, rows=rows):
            r = j // nbr
            m = j % nbr
            qs = r + d * QBLK * m
            gs = pl.multiple_of((r * nbr + m) * QBLK, QBLK)
            wrap = m == 0
            ps = pl.multiple_of((r * nbr + jnp.where(wrap, nbr - 1, m - 1)) * QBLK, QBLK)
            pslot = jnp.where(wrap, 1 - slot, slot)
            q = qd[b, pl.ds(gs, QBLK), :]
            kc = jnp.concatenate([kd[b, pslot, pl.ds(ps, QBLK), :], kd[b, slot, pl.ds(gs, QBLK), :]], axis=0)
            vc = jnp.concatenate([vd[b, pslot, pl.ds(ps, QBLK), :], vd[b, slot, pl.ds(gs, QBLK), :]], axis=0)
            first = jnp.logical_and(wrap, t == 0).astype(jnp.int32)
            zero = jnp.zeros_like(q)
            qq = jnp.concatenate([jnp.where(left, q, zero), jnp.where(left, zero, q)], axis=0)
            s = lax.dot_general(qq, kc, nt, preferred_element_type=F32) + bias[b, first]
            mx = jnp.max(s, axis=-1, keepdims=True)
            e = jnp.exp2(s - mx)
            den = jnp.sum(e, axis=-1, keepdims=True)
            res = jnp.dot(e.astype(BF16), vc, preferred_element_type=F32)
            nacc[b, rows(qs, QBLK), :] = jnp.where(left, res[:QBLK], res[QBLK:])
            dacc[b, rows(qs, QBLK), :] = jnp.where(left, den[:QBLK], den[QBLK:])
            macc[b, rows(qs, QBLK), :] = jnp.where(left, mx[:QBLK], mx[QBLK:])
            return carry

        lax.fori_loop(0, max(DILATIONS), block, 0, unroll=True)

    def merge(c, carry):
        sl = pl.ds(pl.multiple_of(c * QBLK, QBLK), QBLK)
        ms = [macc[b, sl, :] for b in range(len(DILATIONS))]
        top = jnp.maximum(jnp.maximum(ms[0], ms[1]), ms[2])
        ws = [jnp.exp2(mb - top) for mb in ms]
        num = sum(w * nacc[b, sl, :] for b, w in enumerate(ws))
        den = sum(w * dacc[b, sl, :] for b, w in enumerate(ws))
        o_ref[sl, :] = num / den
        return carry

    lax.fori_loop(0, TQ // QBLK, merge, 0)


def _attn_prompt(qkv, rel_bias):
    s = qkv.shape[0]
    assert s % TQ == 0
    nb = len(DILATIONS)
    hp_n = ATT_W // LANES
    tile = lambda f: pl.BlockSpec((TQ, LANES), f)
    return pl.pallas_call(
        _attn_prompt_kernel,
        grid=(hp_n, s // TQ),
        in_specs=[pl.BlockSpec(memory_space=pltpu.SMEM),
                  tile(lambda h, t: (t, h)),
                  tile(lambda h, t: (t, hp_n + h)),
                  tile(lambda h, t: (t, 2 * hp_n + h)),
                  _const_spec((nb, QBLK, 2 * QBLK))],
        out_specs=tile(lambda h, t: (t, h)),
        out_shape=jax.ShapeDtypeStruct((s, ATT_W), F32),
        scratch_shapes=[pltpu.VMEM((nb, TQ, LANES), BF16), pltpu.VMEM((nb, 2, TQ, LANES), BF16),
                        pltpu.VMEM((nb, 2, TQ, LANES), BF16), pltpu.VMEM((STAGE, TQ // STAGE, LANES), F32),
                        pltpu.VMEM((nb, 2, 2 * QBLK, 2 * QBLK), F32),
                        pltpu.VMEM((nb, TQ, LANES), F32), pltpu.VMEM((nb, TQ, LANES), F32),
                        pltpu.VMEM((nb, TQ, LANES), F32)],
        compiler_params=pltpu.CompilerParams(dimension_semantics=("parallel", "arbitrary"),
                                             vmem_limit_bytes=VMEM_LIMIT),
        name="attn_prompt",
    )(rel_bias, qkv, qkv, qkv, _band_buckets())


HPG = SSD_HEADS // N_GROUPS
GROUP_W = SSD_W // N_GROUPS
GROUP_WIN = 3 * LANES
GROUP_LO = tuple((g * GROUP_W) // LANES * LANES for g in range(N_GROUPS))
CARRY_ROW = 8


def _split(x, terms):
    parts = []
    for _ in range(terms):
        p = x.astype(BF16)
        parts.append(p)
        x = x - p.astype(F32)
    return parts


def _dot_exact_rhs(xs_terms, m):
    pieces = [p for x, terms in xs_terms for p in _split(x, terms)]
    out = jnp.dot(jnp.concatenate(pieces, axis=0) if len(pieces) > 1 else pieces[0], m,
                  preferred_element_type=F32)
    res, r = [], 0
    for x, terms in xs_terms:
        n = x.shape[0]
        res.append(sum(out[r + i * n:r + (i + 1) * n] for i in range(terms)))
        r += terms * n
    return res


def _dot_exact_lhs(m, x, terms=3):
    n = x.shape[1]
    out = jnp.dot(m, jnp.concatenate(_split(x, terms), axis=1), preferred_element_type=F32)
    return sum(out[:, i * n:(i + 1) * n] for i in range(terms))


def _softplus(x):
    return jnp.maximum(x, 0.0) + jnp.log1p(jnp.exp(-jnp.abs(x)))


def _silu(x):
    h = 0.5 * x
    return h + h * jnp.tanh(h)


def _ssd_consts():
    c = jnp.arange(SSD_W)
    head_of = (c // SSD_HEAD_DIM)[None, :] == jnp.arange(LANES)[:, None]
    group_of = (c // GROUP_W)[None, :] == jnp.arange(LANES)[:, None]
    return head_of.astype(BF16), group_of.astype(BF16), group_of.T.astype(BF16)


def _conv_silu(cbuf, xbc_ref, w_ref, b_ref, rows):
    out = []
    for c in range(CONV_DIM // LANES):
        cols = slice(c * LANES, (c + 1) * LANES)
        cbuf[c, CARRY_ROW:CARRY_ROW + rows, :] = xbc_ref[:, cols]
        acc = b_ref[:, cols]
        for j in range(CONV_W):
            acc = acc + cbuf[c, pl.ds(CARRY_ROW - (CONV_W - 1) + j, rows), :] * w_ref[j:j + 1, cols]
        out.append(_silu(acc))
    return out


def _drain(steps):
    try:
        while True:
            next(steps)
    except StopIteration as done:
        return done.value


def _interleave(*bodies):
    live = list(bodies)
    while live:
        for body in list(live):
            try:
                next(body)
            except StopIteration:
                live.remove(body)


def _ssd_ydiag(*args):
    return _drain(_ssd_ydiag_steps(*args))


def _ssd_ydiag_steps(xs, bm, cm, acs, dt, mask):
    cbs = [lax.dot_general(cm[g], bm[g], (((1,), (1,)), ((), ())), preferred_element_type=F32)
           for g in range(N_GROUPS)]
    acs_t = acs.T
    dt_t = dt.T
    left = lax.broadcasted_iota(jnp.int32, (1, LANES), 1) < SSD_HEAD_DIM
    pairs = []
    for pair in range(SSD_HEADS // 2):
        xpair = xs[:, pair * LANES:(pair + 1) * LANES]
        acc = None
        for hh in range(2):
            h = 2 * pair + hh
            seg = acs[:, h:h + 1] - acs_t[h:h + 1, :]
            dec = jnp.exp2(jnp.where(mask, seg, NEG_INF))
            w = (cbs[h // HPG] * dec * dt_t[h:h + 1, :]).astype(BF16)
            xh = jnp.where(left if hh == 0 else jnp.logical_not(left), xpair, 0.0).astype(BF16)
            part = jnp.dot(w, xh, preferred_element_type=F32)
            acc = part if acc is None else acc + part
        pairs.append(acc)
        if pair % 2 == 1:
            yield
    return jnp.concatenate(pairs, axis=1)


def _ssd_gate_norm(y, gate, gred, gexp, gout):
    y = y * gate
    ms = _dot_exact_rhs([(y * y, 2)], gred)[0] * (1.0 / GROUP_W)
    return y * _dot_exact_rhs([(lax.rsqrt(ms + EPS), 2)], gexp)[0] * gout


def _ssd_prompt_init(cbuf, state):
    @pl.when(pl.program_id(0) == 0)
    def _init():
        state[...] = jnp.zeros_like(state)
        cbuf[:, 0:CARRY_ROW, :] = jnp.zeros((CONV_DIM // LANES, CARRY_ROW, LANES), F32)


def _ssd_prompt_emit(st_ref, state):
    @pl.when(pl.program_id(0) == pl.num_programs(0) - 1)
    def _emit_state():
        for i in range(SSD_W // LANES):
            st_ref[i * LANES:(i + 1) * LANES, :] = state[:, i * LANES:(i + 1) * LANES].T


def _ssd_prompt_chunk(z_ref, xbc_ref, dt_ref, cw_ref, cb_ref, dtb_ref, alog_ref, dsk_ref, gout_ref,
                      hexp_ref, gexp_ref, gred_ref, y_ref, cbuf, state, yoff):
    cl = SSD_CHUNK
    xc = _conv_silu(cbuf, xbc_ref, cw_ref, cb_ref, cl)
    cbuf[:, CARRY_ROW - (CONV_W - 1):CARRY_ROW, :] = cbuf[:, CARRY_ROW + cl - (CONV_W - 1):CARRY_ROW + cl, :]
    nx = SSD_W // LANES
    xs = jnp.concatenate(xc[:nx], axis=1)
    bm = [b.astype(BF16) for b in xc[nx:nx + N_GROUPS]]
    cm = [b.astype(BF16) for b in xc[nx + N_GROUPS:]]
    yield

    li = lax.broadcasted_iota(jnp.int32, (cl, cl), 0)
    si = lax.broadcasted_iota(jnp.int32, (cl, cl), 1)
    causal = li >= si

    dt = _softplus(dt_ref[...] + dtb_ref[...])
    acs = _dot_exact_lhs(causal.astype(BF16), dt * (-jnp.exp(alog_ref[...]) * LOG2E))
    hexp = hexp_ref[...]
    e_in, e_end = _dot_exact_rhs([(jnp.exp2(acs), 3), (jnp.exp2(acs[cl - 1:cl, :] - acs) * dt, 1)], hexp)
    xw = (xs * e_end).astype(BF16)
    chunk_decay = e_in[cl - 1:cl]
    yield

    gmask = lax.broadcasted_iota(jnp.int32, (1, SSD_W), 1) // GROUP_W
    for g in range(N_GROUPS):
        lo = GROUP_LO[g]
        win = slice(lo, lo + GROUP_WIN)
        inside = gmask[:, win] == g
        yo = jnp.dot(cm[g], state[:, win].astype(BF16), preferred_element_type=F32)
        yoff[:, win] = jnp.where(inside, yo, 0.0 if g == 0 else yoff[:, win])
    yield
    for g in range(N_GROUPS):
        lo = GROUP_LO[g]
        win = slice(lo, lo + GROUP_WIN)
        inside = gmask[:, win] == g
        upd = lax.dot_general(bm[g], xw[:, win], (((0,), (0,)), ((), ())), preferred_element_type=F32)
        old = state[:, win]
        state[:, win] = jnp.where(inside, old * chunk_decay[:, win] + upd, old)
    yield

    ydiag = yield from _ssd_ydiag_steps(xs, bm, cm, acs, dt, causal)
    y_ref[...] = _ssd_gate_norm(ydiag + yoff[...] * e_in + xs * dsk_ref[...], _silu(z_ref[...]),
                                gred_ref[...], gexp_ref[...], gout_ref[...])


N_SSD_IN = 12


def _ssd_prompt_kernel(*refs):
    ins, (y_ref, st_ref, cbuf, state, yoff) = refs[:N_SSD_IN], refs[N_SSD_IN:]
    _ssd_prompt_init(cbuf, state)
    _drain(_ssd_prompt_chunk(*ins, y_ref, cbuf, state, yoff))
    _ssd_prompt_emit(st_ref, state)


def _ssd_prompt_call(z, xbc, dt, conv_w, conv_b, dt_bias, a_log, dskip_x, g_out):
    s = z.shape[0]
    assert s % SSD_CHUNK == 0
    hexp, gexp, gred = _ssd_consts()
    row = lambda n: pl.BlockSpec((SSD_CHUNK, n), lambda i: (i, 0))
    consts = [conv_w, conv_b, dt_bias, a_log, dskip_x, g_out, hexp, gexp, gred]
    return dict(
        steps=s // SSD_CHUNK,
        operands=[z, xbc, dt] + consts,
        in_specs=[row(SSD_W), row(CONV_DIM), row(LANES)] + [_const_spec(a.shape) for a in consts],
        out_specs=[row(SSD_W), pl.BlockSpec((SSD_W, D_STATE), lambda i: (0, 0))],
        out_shape=[jax.ShapeDtypeStruct((s, SSD_W), F32), jax.ShapeDtypeStruct((SSD_W, D_STATE), F32)],
        scratch=[pltpu.VMEM((CONV_DIM // LANES, CARRY_ROW + SSD_CHUNK, LANES), F32),
                 pltpu.VMEM((D_STATE, SSD_W), F32), pltpu.VMEM((SSD_CHUNK, SSD_W), F32)])


def _run(kernel_fn, name, *calls, vmem_limit=VMEM_LIMIT):
    steps = calls[0]["steps"]
    assert all(c["steps"] == steps for c in calls)
    cat = lambda key: [x for c in calls for x in c[key]]
    return pl.pallas_call(
        kernel_fn, grid=(steps,), in_specs=cat("in_specs"), out_specs=cat("out_specs"),
        out_shape=cat("out_shape"), scratch_shapes=cat("scratch"),
        compiler_params=pltpu.CompilerParams(dimension_semantics=("arbitrary",), vmem_limit_bytes=vmem_limit),
        name=name)(*cat("operands"))


def _ssd_prompt(*args):
    return _run(_ssd_prompt_kernel, "ssd_prompt", _ssd_prompt_call(*args))


DEC_SEQ = 4
NEW_ROWS = 8


def _branch_count(dist):
    return sum(((dist % d == 0) & (dist <= QBLK * d)).astype(np.int32) for d in DILATIONS)


def _sample_bias_tables(wb):
    t = np.arange(NEW_ROWS)[:, None]
    dist = wb + t - np.arange(wb)[None, :]
    cnt = np.where(t < DEC_SEQ, _branch_count(dist), 0)
    bkt = np.where(cnt > 0, _rel_bucket(dist), -1).astype(np.int32)
    ladd = np.log(np.maximum(cnt, 1).astype(np.float32))
    u = np.arange(LANES)[None, :]
    dnew = t - u
    cnew = np.where((t < DEC_SEQ) & (dnew > 0), _branch_count(dnew),
                    np.where((t < DEC_SEQ) & (dnew == 0), len(DILATIONS), 0))
    bkt_new = np.where(cnew > 0, _rel_bucket(dnew), -1).astype(np.int32)
    ladd_new = np.log(np.maximum(cnew, 1).astype(np.float32))
    return tuple(jnp.asarray(a) for a in (bkt, ladd, bkt_new, ladd_new))


def _attn_sample_init(tbl_ref, bkt_ref, ladd_ref, bktn_ref, laddn_ref, bias, bias_new):
    @pl.when(pl.program_id(0) == 0)
    def _build_bias():
        for h in range(ATT_HEADS):
            for src, add, dst in ((bkt_ref, ladd_ref, bias), (bktn_ref, laddn_ref, bias_new)):
                bk = src[...]
                acc = jnp.full(bk.shape, NEG_INF, F32)
                for u in range(N_BUCKETS):
                    acc = jnp.where(bk == u, tbl_ref[u, h], acc)
                dst[h // 2, (h % 2) * NEW_ROWS:(h % 2 + 1) * NEW_ROWS, :] = (acc + add[...]) * LOG2E


N_ATTN_S_IN = 8


def _attn_sample_kernel(tbl_ref, q_ref, kt_ref, vt_ref, bkt_ref, ladd_ref, bktn_ref, laddn_ref,
                        o_ref, bias, bias_new):
    _attn_sample_init(tbl_ref, bkt_ref, ladd_ref, bktn_ref, laddn_ref, bias, bias_new)
    whole = lambda hp: (kt_ref.at[0], vt_ref.at[0], hp * LANES)
    _drain(_attn_sample_seq(q_ref, whole, o_ref, bias, bias_new))


RING = 3
RING_VMEM_LIMIT = 58 * 1024 * 1024


def _ring_kv(k_hbm, v_hbm, kring, vring, sems):
    i = pl.program_id(0)
    n = pl.num_programs(0)

    def copies(seq, slot):
        return (pltpu.make_async_copy(k_hbm.at[seq], kring.at[slot], sems.at[0, slot]),
                pltpu.make_async_copy(v_hbm.at[seq], vring.at[slot], sems.at[1, slot]))

    def start(seq, slot):
        for c in copies(jnp.minimum(seq, n - 1), slot):
            c.start()

    @pl.when(i == 0)
    def _prime():
        for s in range(RING - 1):
            start(s, s)

    start(i + RING - 1, (i + RING - 1) % RING)
    slot = i % RING
    for c in copies(i, slot):
        c.wait()

    def finish():
        @pl.when(i == n - 1)
        def _drain_tail():
            for s in range(RING - 1):
                for c in copies(0, (n + s) % RING):
                    c.wait()

    return (lambda hp: (kring.at[slot], vring.at[slot], hp * LANES)), finish


def _attn_sample_seq(q_ref, kv_for_pair, o_ref, bias, bias_new):
    left = lax.broadcasted_iota(jnp.int32, (1, LANES), 1) < HEAD_DIM
    nt = (((1,), (1,)), ((), ()))
    scale = HEAD_DIM ** -0.5 * LOG2E
    second = pl.program_id(0) % 2 == 1
    mine = lambda x: jnp.where(second, pltpu.roll(x, DEC_SEQ, axis=0), x)
    outs = []
    for hp in range(ATT_W // LANES):
        cols = slice(hp * LANES, (hp + 1) * LANES)
        k_ref, v_ref, row0 = kv_for_pair(hp)
        kv_rows = slice(row0, row0 + LANES)
        q = mine(q_ref[:, cols]) * scale
        fill = jnp.zeros((LANES - NEW_ROWS, LANES), F32)
        knew = jnp.concatenate([mine(q_ref[:, ATT_W + hp * LANES:ATT_W + (hp + 1) * LANES]), fill], axis=0)
        vnew = jnp.concatenate([mine(q_ref[:, 2 * ATT_W + hp * LANES:2 * ATT_W + (hp + 1) * LANES]), fill], axis=0)
        qq = jnp.concatenate([jnp.where(left, q, 0.0), jnp.where(left, 0.0, q)], axis=0).astype(BF16)
        s = jnp.dot(qq, k_ref[kv_rows, :].astype(BF16), preferred_element_type=F32) + bias[hp]
        sn = lax.dot_general(qq, knew.astype(BF16), nt, preferred_element_type=F32) + bias_new[hp]
        m = jnp.maximum(jnp.max(s, axis=-1, keepdims=True), jnp.max(sn, axis=-1, keepdims=True))
        m = jnp.where(m == NEG_INF, 0.0, m)
        p = jnp.exp2(s - m)
        pn = jnp.exp2(sn - m)
        den = jnp.sum(p, axis=-1, keepdims=True) + jnp.sum(pn, axis=-1, keepdims=True)
        r = lax.dot_general(p.astype(BF16), v_ref[kv_rows, :].astype(BF16), nt, preferred_element_type=F32)
        r = r + jnp.dot(pn.astype(BF16), vnew.astype(BF16), preferred_element_type=F32)
        o = r / jnp.where(den == 0.0, 1.0, den)
        outs.append(jnp.where(left, o[:NEW_ROWS], o[NEW_ROWS:]))
        yield
    out = jnp.concatenate(outs, axis=1)

    @pl.when(jnp.logical_not(second))
    def _first_of_pair():
        o_ref[...] = out

    @pl.when(second)
    def _second_of_pair():
        upper = lax.broadcasted_iota(jnp.int32, (NEW_ROWS, 1), 0) < DEC_SEQ
        o_ref[...] = jnp.where(upper, o_ref[...], pltpu.roll(out, DEC_SEQ, axis=0))


def _attn_sample_call(qkv, cache_kt, cache_vt, rel_bias, ring=False):
    db, _, wb = cache_kt.shape
    assert wb >= WIN_MAX and qkv.shape[0] == db * DEC_SEQ and db % 2 == 0 and NEW_ROWS == 2 * DEC_SEQ
    consts = list(_sample_bias_tables(wb))
    hp_n = ATT_W // LANES
    pair_rows = lambda m: pl.BlockSpec((NEW_ROWS, m), lambda b: (b // 2, 0))
    kv_spec = pl.BlockSpec(memory_space=pl.ANY) if ring else pl.BlockSpec((1, ATT_W, wb), lambda b: (b, 0, 0))
    scratch = [pltpu.VMEM((hp_n, 2 * NEW_ROWS, wb), F32), pltpu.VMEM((hp_n, 2 * NEW_ROWS, LANES), F32)]
    if ring:
        scratch += [pltpu.VMEM((RING, ATT_W, wb), F32), pltpu.VMEM((RING, ATT_W, wb), F32),
                    pltpu.SemaphoreType.DMA((2, RING))]
    return dict(
        steps=db,
        operands=[rel_bias, qkv, cache_kt, cache_vt] + consts,
        in_specs=[pl.BlockSpec(memory_space=pltpu.SMEM), pair_rows(3 * ATT_W), kv_spec, kv_spec]
                 + [_const_spec(c.shape) for c in consts],
        out_specs=[pair_rows(ATT_W)],
        out_shape=[jax.ShapeDtypeStruct((db * DEC_SEQ, ATT_W), F32)],
        scratch=scratch)


def _attn_sample(qkv, cache_kt, cache_vt, rel_bias):
    out, = _run(_attn_sample_kernel, "attn_sample", _attn_sample_call(qkv, cache_kt, cache_vt, rel_bias))
    return out


def _ssd_prompt_attn_sample_kernel(*refs):
    a = N_SSD_IN + N_ATTN_S_IN
    ssd_in = refs[:N_SSD_IN]
    tbl_ref, q_ref, kt_ref, vt_ref, bkt_ref, ladd_ref, bktn_ref, laddn_ref = refs[N_SSD_IN:a]
    y_ref, st_ref, o_ref, cbuf, state, yoff, bias, bias_new, kring, vring, sems = refs[a:]
    _ssd_prompt_init(cbuf, state)
    _attn_sample_init(tbl_ref, bkt_ref, ladd_ref, bktn_ref, laddn_ref, bias, bias_new)
    kv_for_pair, finish_ring = _ring_kv(kt_ref, vt_ref, kring, vring, sems)
    _interleave(_ssd_prompt_chunk(*ssd_in, y_ref, cbuf, state, yoff),
                _attn_sample_seq(q_ref, kv_for_pair, o_ref, bias, bias_new))
    finish_ring()
    _ssd_prompt_emit(st_ref, state)


def _ssd_prompt_attn_sample(ssd_args, attn_args):
    y, st, att = _run(_ssd_prompt_attn_sample_kernel, "ssd_prompt_attn_sample",
                      _ssd_prompt_call(*ssd_args), _attn_sample_call(*attn_args, ring=True),
                      vmem_limit=RING_VMEM_LIMIT)
    return y, st, att


SEQ_ROWS = 8
SEQ_PER_STEP = 8


def _ssd_sample_kernel(z_ref, xp_ref, dt_ref, st_ref, cw_ref, cb_ref, dtb_ref, alog_ref, dsk_ref, gout_ref,
                       hexp_ref, gexp_ref, gred_ref, y_ref, sto_ref, cbuf, yoff):
    rows = SEQ_PER_STEP * SEQ_ROWS
    cbuf[:, 0:CARRY_ROW, :] = jnp.zeros((CONV_DIM // LANES, CARRY_ROW, LANES), F32)
    xc = _conv_silu(cbuf, xp_ref, cw_ref, cb_ref, rows)
    nx = SSD_W // LANES
    xs = jnp.concatenate(xc[:nx], axis=1)
    bmf = xc[nx:nx + N_GROUPS]
    bm = [b.astype(BF16) for b in bmf]
    cm = [b.astype(BF16) for b in xc[nx + N_GROUPS:]]

    li = lax.broadcasted_iota(jnp.int32, (rows, rows), 0)
    si = lax.broadcasted_iota(jnp.int32, (rows, rows), 1)
    same = (li // SEQ_ROWS) == (si // SEQ_ROWS)
    mask = same & (li >= si)
    rowi = lax.broadcasted_iota(jnp.int32, (rows, 1), 0)
    is_token = (rowi % SEQ_ROWS) >= SEQ_ROWS - DEC_SEQ

    dt = jnp.where(is_token, _softplus(dt_ref[...] + dtb_ref[...]), 0.0)
    da = dt * (-jnp.exp(alog_ref[...]) * LOG2E)
    acs = _dot_exact_lhs(mask.astype(BF16), da)
    acs_end = _dot_exact_lhs(same.astype(BF16), da)
    hexp = hexp_ref[...]
    e_in, e_end, e_all = _dot_exact_rhs(
        [(jnp.exp2(acs), 2), (jnp.exp2(acs_end - acs) * dt, 1), (jnp.exp2(acs_end), 3)], hexp)
    xw = xs * e_end

    sub = rowi % SEQ_ROWS
    pieces = [p.astype(F32) for p in _split(e_all, 3)]
    dstack = jnp.where(sub == 0, pieces[0], jnp.where(sub == 1, pieces[1], jnp.where(sub == 2, pieces[2], 0.0)))
    ones = jnp.ones((SEQ_ROWS, D_STATE), BF16)
    gmask = lax.broadcasted_iota(jnp.int32, (1, SSD_W), 1) // GROUP_W
    tn = (((0,), (0,)), ((), ()))
    nt = (((1,), (1,)), ((), ()))
    for i in range(SEQ_PER_STEP):
        rs = slice(i * SEQ_ROWS, (i + 1) * SEQ_ROWS)
        for g in range(N_GROUPS):
            lo = GROUP_LO[g]
            win = slice(lo, lo + GROUP_WIN)
            inside = gmask[:, win] == g
            yo = lax.dot_general(cm[g][rs], st_ref[i, win, :].astype(BF16), nt, preferred_element_type=F32)
            yoff[rs, win] = jnp.where(inside, yo, 0.0 if g == 0 else yoff[rs, win])
        xstack = jnp.concatenate([jnp.where(gmask == g, xw[rs, :], 0.0) for g in range(N_GROUPS)], axis=0)
        bstack = jnp.concatenate([bmf[g][rs] for g in range(N_GROUPS)], axis=0)
        upd = lax.dot_general(xstack.astype(BF16), bstack.astype(BF16), tn, preferred_element_type=F32)
        dcol = lax.dot_general(dstack[rs, :].astype(BF16), ones, tn, preferred_element_type=F32)
        sto_ref[i] = st_ref[i] * dcol + upd

    ydiag = _ssd_ydiag(xs, bm, cm, acs, dt, mask)
    y_ref[...] = _ssd_gate_norm(ydiag + yoff[...] * e_in + xs * dsk_ref[...], _silu(z_ref[...]),
                                gred_ref[...], gexp_ref[...], gout_ref[...])


def _ssd_sample(z, xbc, dt, conv_state, ssm_state, conv_w, conv_b, dt_bias, a_log, dskip_x, g_out):
    db = conv_state.shape[0]
    assert db % SEQ_PER_STEP == 0 and z.shape[0] == db * DEC_SEQ
    lead = SEQ_ROWS - DEC_SEQ

    def padded(a):
        a = a.reshape(db, DEC_SEQ, a.shape[-1])
        return jnp.pad(a, ((0, 0), (lead, 0), (0, 0))).reshape(db * SEQ_ROWS, a.shape[-1])

    xp = jnp.concatenate([jnp.zeros((db, lead - (CONV_W - 1), CONV_DIM), F32), conv_state,
                          xbc.reshape(db, DEC_SEQ, CONV_DIM)], axis=1).reshape(db * SEQ_ROWS, CONV_DIM)
    hexp, gexp, gred = _ssd_consts()
    rows = SEQ_PER_STEP * SEQ_ROWS
    row = lambda n: pl.BlockSpec((rows, n), lambda i: (i, 0))
    st_spec = pl.BlockSpec((SEQ_PER_STEP, SSD_W, D_STATE), lambda i: (i, 0, 0))
    consts = [conv_w, conv_b, dt_bias, a_log, dskip_x, g_out, hexp, gexp, gred]
    y, st = pl.pallas_call(
        _ssd_sample_kernel,
        grid=(db // SEQ_PER_STEP,),
        in_specs=[row(SSD_W), row(CONV_DIM), row(LANES), st_spec] + [_const_spec(a.shape) for a in consts],
        out_specs=[row(SSD_W), st_spec],
        out_shape=[jax.ShapeDtypeStruct((db * SEQ_ROWS, SSD_W), F32),
                   jax.ShapeDtypeStruct((db, SSD_W, D_STATE), F32)],
        scratch_shapes=[pltpu.VMEM((CONV_DIM // LANES, CARRY_ROW + rows, LANES), F32),
                        pltpu.VMEM((rows, SSD_W), F32)],
        compiler_params=pltpu.CompilerParams(dimension_semantics=("parallel",),
                                             vmem_limit_bytes=VMEM_LIMIT),
        name="ssd_sample",
    )(padded(z), xp, padded(dt), ssm_state, *consts)
    y = y.reshape(db, SEQ_ROWS, SSD_W)[:, lead:].reshape(db * DEC_SEQ, SSD_W)
    return y, st


def kernel(x_prompt, x_sample, cache_k, cache_v, state_conv, state_ssm, rel_bias, g_mix, w_in, conv_w, conv_b,
           dt_bias, a_log, d_skip, g_attn_out, g_ssd_out, w_out, g_ffn, w_gate, w_up, w_down, g_final):
    depth = w_in.shape[0]
    bp, sp, _ = x_prompt.shape
    db, ds, _ = x_sample.shape
    assert bp == 1 and ds == DEC_SEQ and sp >= WIN_MAX
    xp = x_prompt.reshape(bp * sp, D_MODEL)
    xs = x_sample.reshape(db * ds, D_MODEL)
    pad_heads = lambda v: jnp.pad(v, (0, LANES - SSD_HEADS))[None]
    outs = [[] for _ in range(8)]
    for l in range(depth):
        w_t = w_in[l].T.astype(BF16)
        wdt = jnp.pad(w_t[IN_SPLITS[-1]:], ((0, LANES - SSD_HEADS), (0, 0)))
        ssd_prm = (conv_w[l], conv_b[l][None], pad_heads(dt_bias[l]), pad_heads(a_log[l]),
                   jnp.repeat(d_skip[l], SSD_HEAD_DIM)[None], g_ssd_out[l][None])
        tail_prm = (g_attn_out[l][None], w_out[l][:ATT_W].astype(BF16), w_out[l][ATT_W:].astype(BF16),
                    g_ffn[l][None], w_gate[l].astype(BF16), w_up[l].astype(BF16), w_down[l].astype(BF16),
                    g_final[None])
        last = l == depth - 1

        qkv_p, z_p, xbc_p, dt_p = _inproj(xp, g_mix[l][None], w_t, wdt)
        qkv_s, z_s, xbc_s, dt_s = _inproj(xs, g_mix[l][None], w_t, wdt)
        att_p = _attn_prompt(qkv_p, rel_bias)
        feature_major = lambda c: jnp.transpose(c, (0, 2, 3, 1)).reshape(db, ATT_W, c.shape[1])
        ssd_args = (z_p, xbc_p, dt_p, *ssd_prm)
        attn_args = (qkv_s, feature_major(cache_k[l]), feature_major(cache_v[l]), rel_bias)
        if sp // SSD_CHUNK == db:
            ssd_p, st_p, att_s = _ssd_prompt_attn_sample(ssd_args, attn_args)
        else:
            ssd_p, st_p = _ssd_prompt(*ssd_args)
            att_s = _attn_sample(*attn_args)
        xp = _tail(xp, att_p, ssd_p, *tail_prm, final_norm=last)
        ssd_s, st_s = _ssd_sample(z_s, xbc_s, dt_s, state_conv[l], state_ssm[l].reshape(db, SSD_W, D_STATE),
                                  *ssd_prm)
        xs = _tail(xs, att_s, ssd_s, *tail_prm, final_norm=last)

        heads = lambda a: a.reshape(a.shape[0], a.shape[1], ATT_HEADS, HEAD_DIM)
        kv_p = qkv_p.reshape(bp, sp, 3 * ATT_W)[:, sp - WIN_MAX:]
        kv_s = qkv_s.reshape(db, ds, 3 * ATT_W)
        outs[0].append(heads(kv_p[..., ATT_W:2 * ATT_W]))
        outs[1].append(heads(kv_p[..., 2 * ATT_W:]))
        outs[2].append(xbc_p.reshape(bp, sp, CONV_DIM)[:, sp - (CONV_W - 1):])
        outs[3].append(st_p.reshape(bp, SSD_HEADS, SSD_HEAD_DIM, D_STATE))
        outs[4].append(heads(kv_s[..., ATT_W:2 * ATT_W]))
        outs[5].append(heads(kv_s[..., 2 * ATT_W:]))
        outs[6].append(xbc_s.reshape(db, ds, CONV_DIM)[:, ds - (CONV_W - 1):])
        outs[7].append(st_s.reshape(db, SSD_HEADS, SSD_HEAD_DIM, D_STATE))
    return (xp.reshape(bp, sp, D_MODEL), xs.reshape(db, ds, D_MODEL)) + tuple(jnp.stack(o) for o in outs)
```

```python
import functools
import math

import jax
import jax.numpy as jnp
import numpy as np
from jax import lax
from jax.experimental import pallas as pl
from jax.experimental.pallas import tpu as pltpu

F32 = jnp.float32
BF16 = jnp.bfloat16

D_MODEL = 1024
ATT_HEADS = 12
HEAD_DIM = 64
ATT_W = ATT_HEADS * HEAD_DIM
DILATIONS = (1, 4, 16)
QBLK = 128
WIN_MAX = 2048
N_BUCKETS = 32
MAX_EXACT = N_BUCKETS // 2
SSD_HEADS = 20
SSD_HEAD_DIM = 64
SSD_W = SSD_HEADS * SSD_HEAD_DIM
N_GROUPS = 4
D_STATE = 128
CONV_W = 4
CONV_DIM = SSD_W + 2 * N_GROUPS * D_STATE
SSD_CHUNK = 128
EPS = 1e-6

LANES = 128
VMEM_LIMIT = 48 * 1024 * 1024
TAIL_VMEM_LIMIT = 58 * 1024 * 1024


def _rms(x, g):
    return x * lax.rsqrt(jnp.mean(x * x, axis=-1, keepdims=True) + EPS) * g


def _const_spec(shape):
    nd = len(shape)
    return pl.BlockSpec(shape, lambda *_: (0,) * nd, pipeline_mode=pl.Buffered(1))


IN_SPLITS = (0, 3 * ATT_W, 3 * ATT_W + SSD_W, 3 * ATT_W + SSD_W + CONV_DIM)


def _inproj_kernel(x_ref, g_ref, w_ref, wdt_ref, qkv_ref, z_ref, xbc_ref, dt_ref):
    h = _rms(x_ref[...], g_ref[...]).astype(BF16)
    nt = (((1,), (1,)), ((), ()))
    for out, lo, hi in zip((qkv_ref, z_ref, xbc_ref), IN_SPLITS[:-1], IN_SPLITS[1:]):
        out[...] = lax.dot_general(h, w_ref[lo:hi, :], nt, preferred_element_type=F32)
    dt_ref[...] = lax.dot_general(h, wdt_ref[...], nt, preferred_element_type=F32)


def _inproj(x, g_mix, w_t, wdt):
    t = x.shape[0]
    tm = 512 if t % 512 == 0 and t > 512 else 256
    assert t % tm == 0
    row = lambda n: pl.BlockSpec((tm, n), lambda i: (i, 0))
    return pl.pallas_call(
        _inproj_kernel,
        grid=(t // tm,),
        in_specs=[row(D_MODEL), _const_spec((1, D_MODEL)), _const_spec(w_t.shape), _const_spec(wdt.shape)],
        out_specs=[row(3 * ATT_W), row(SSD_W), row(CONV_DIM), row(LANES)],
        out_shape=[jax.ShapeDtypeStruct((t, 3 * ATT_W), F32), jax.ShapeDtypeStruct((t, SSD_W), F32),
                   jax.ShapeDtypeStruct((t, CONV_DIM), F32), jax.ShapeDtypeStruct((t, LANES), F32)],
        compiler_params=pltpu.CompilerParams(dimension_semantics=("parallel",),
                                             vmem_limit_bytes=VMEM_LIMIT),
        name="inproj",
    )(x, g_mix, w_t, wdt)


def _tail_kernel(x_ref, att_ref, ssd_ref, gatt_ref, woa_ref, wos_ref, gffn_ref,
                 wg_ref, wu_ref, wd_ref, gfin_ref, y_ref, *, final_norm):
    an = _rms(att_ref[...], gatt_ref[...]).astype(BF16)
    mix = jnp.dot(an, woa_ref[...], preferred_element_type=F32)
    mix = mix + jnp.dot(ssd_ref[...].astype(BF16), wos_ref[...], preferred_element_type=F32)
    x1 = x_ref[...] + mix
    h2 = _rms(x1, gffn_ref[...]).astype(BF16)
    gate = jnp.dot(h2, wg_ref[...], preferred_element_type=F32)
    up = jnp.dot(h2, wu_ref[...], preferred_element_type=F32)
    act = (gate * jax.nn.sigmoid(gate) * up).astype(BF16)
    x2 = x1 + jnp.dot(act, wd_ref[...], preferred_element_type=F32)
    y_ref[...] = _rms(x2, gfin_ref[...]) if final_norm else x2


def _tail(x, att, ssd, g_att, woa, wos, g_ffn, wg, wu, wd, g_fin, final_norm):
    t = x.shape[0]
    tm = 512 if t % 512 == 0 and t > 512 else 256
    assert t % tm == 0
    row = lambda n: pl.BlockSpec((tm, n), lambda i: (i, 0))
    consts = [g_att, woa, wos, g_ffn, wg, wu, wd, g_fin]
    return pl.pallas_call(
        functools.partial(_tail_kernel, final_norm=final_norm),
        grid=(t // tm,),
        in_specs=[row(D_MODEL), row(ATT_W), row(SSD_W)] + [_const_spec(c.shape) for c in consts],
        out_specs=row(D_MODEL),
        out_shape=jax.ShapeDtypeStruct((t, D_MODEL), F32),
        compiler_params=pltpu.CompilerParams(dimension_semantics=("parallel",),
                                             vmem_limit_bytes=TAIL_VMEM_LIMIT),
        name="tail",
    )(x, att, ssd, *consts)


TQ = QBLK * max(DILATIONS)
NEG_INF = float("-inf")
LOG2E = math.log2(math.e)


def _rel_bucket(dist):
    n = np.maximum(dist, 0)
    nf = np.maximum(n, 1).astype(np.float32)
    large = MAX_EXACT + (np.log(nf / MAX_EXACT) / math.log(WIN_MAX / MAX_EXACT)
                         * (N_BUCKETS - MAX_EXACT)).astype(np.int32)
    return np.where(n < MAX_EXACT, n, np.minimum(large, N_BUCKETS - 1))


def _band_buckets():
    qi = np.arange(QBLK)[:, None] + QBLK
    kk = np.arange(2 * QBLK)[None, :]
    dist = qi - kk
    valid = (dist >= 0) & (dist <= QBLK)
    return jnp.asarray(np.stack([np.where(valid, _rel_bucket(dist * d), -1) for d in DILATIONS]).astype(np.int32))


STAGE = 4
assert DILATIONS == (1, STAGE, STAGE * STAGE)


def _regroup(src_ref, tmp, dsts, scale=None):
    cast = (lambda x: x.astype(BF16)) if scale is None else (lambda x: (x * scale).astype(BF16))
    sub = TQ // STAGE
    for c in range(STAGE):
        dsts[0][c * sub:(c + 1) * sub, :] = cast(src_ref[c * sub:(c + 1) * sub, :])
    for r in range(STAGE):
        x = src_ref[pl.ds(r, sub, stride=STAGE), :]
        tmp[r] = x
        dsts[1][r * sub:(r + 1) * sub, :] = cast(x)
    for r in range(STAGE):
        for r2 in range(STAGE):
            res = r + STAGE * r2
            dsts[2][res * QBLK:(res + 1) * QBLK, :] = cast(tmp[r, pl.ds(r2, QBLK, stride=STAGE), :])


def _attn_prompt_kernel(tbl_ref, q_ref, k_ref, v_ref, bkt_ref, o_ref,
                        qd, kd, vd, tmp, bias, macc, nacc, dacc):
    hp = pl.program_id(0)
    t = pl.program_id(1)
    nb = len(DILATIONS)
    slot = t % 2

    @pl.when(t == 0)
    def _no_previous_tile():
        kd[:, 1] = jnp.zeros((nb, TQ, LANES), BF16)
        vd[:, 1] = jnp.zeros((nb, TQ, LANES), BF16)

    @pl.when(t == 0)
    def _build_bias():
        prev_half = lax.broadcasted_iota(jnp.int32, (1, 2 * QBLK), 1) < QBLK
        for b in range(len(DILATIONS)):
            bk = bkt_ref[b]
            for hh in range(2):
                acc = jnp.full((QBLK, 2 * QBLK), NEG_INF, F32)
                for u in range(N_BUCKETS):
                    acc = jnp.where(bk == u, tbl_ref[u, 2 * hp + hh], acc)
                acc = acc * LOG2E
                bias[b, 0, hh * QBLK:(hh + 1) * QBLK] = acc
                bias[b, 1, hh * QBLK:(hh + 1) * QBLK] = jnp.where(prev_half, NEG_INF, acc)

    _regroup(q_ref, tmp, [qd.at[b] for b in range(nb)], HEAD_DIM ** -0.5 * LOG2E)
    _regroup(k_ref, tmp, [kd.at[b, slot] for b in range(nb)])
    _regroup(v_ref, tmp, [vd.at[b, slot] for b in range(nb)])

    left = lax.broadcasted_iota(jnp.int32, (1, LANES), 1) < HEAD_DIM
    nt = (((1,), (1,)), ((), ()))

    for b, d in enumerate(DILATIONS):
        nbr = max(DILATIONS) // d

        def rows(start, size, d=d):
            return pl.ds(start, size) if d == 1 else pl.ds(start, size, stride=d)

        def block(j, carry, b=b, d=d, nbr=nbr, rows=rows):
            r = j // nbr
            m = j % nbr
            qs = r + d * QBLK * m
            gs = pl.multiple_of((r * nbr + m) * QBLK, QBLK)
            wrap = m == 0
            ps = pl.multiple_of((r * nbr + jnp.where(wrap, nbr - 1, m - 1)) * QBLK, QBLK)
            pslot = jnp.where(wrap, 1 - slot, slot)
            q = qd[b, pl.ds(gs, QBLK), :]
            kc = jnp.concatenate([kd[b, pslot, pl.ds(ps, QBLK), :], kd[b, slot, pl.ds(gs, QBLK), :]], axis=0)
            vc = jnp.concatenate([vd[b, pslot, pl.ds(ps, QBLK), :], vd[b, slot, pl.ds(gs, QBLK), :]], axis=0)
            first = jnp.logical_and(wrap, t == 0).astype(jnp.int32)
            zero = jnp.zeros_like(q)
            qq = jnp.concatenate([jnp.where(left, q, zero), jnp.where(left, zero, q)], axis=0)
            s = lax.dot_general(qq, kc, nt, preferred_element_type=F32) + bias[b, first]
            mx = jnp.max(s, axis=-1, keepdims=True)
            e = jnp.exp2(s - mx)
            den = jnp.sum(e, axis=-1, keepdims=True)
            res = jnp.dot(e.astype(BF16), vc, preferred_element_type=F32)
            nacc[b, rows(qs, QBLK), :] = jnp.where(left, res[:QBLK], res[QBLK:])
            dacc[b, rows(qs, QBLK), :] = jnp.where(left, den[:QBLK], den[QBLK:])
            macc[b, rows(qs, QBLK), :] = jnp.where(left, mx[:QBLK], mx[QBLK:])
            return carry

        lax.fori_loop(0, max(DILATIONS), block, 0, unroll=True)

    def merge(c, carry):
        sl = pl.ds(pl.multiple_of(c * QBLK, QBLK), QBLK)
        ms = [macc[b, sl, :] for b in range(len(DILATIONS))]
        top = jnp.maximum(jnp.maximum(ms[0], ms[1]), ms[2])
        ws = [jnp.exp2(mb - top) for mb in ms]
        num = sum(w * nacc[b, sl, :] for b, w in enumerate(ws))
        den = sum(w * dacc[b, sl, :] for b, w in enumerate(ws))
        o_ref[sl, :] = num / den
        return carry

    lax.fori_loop(0, TQ // QBLK, merge, 0)


def _attn_prompt(qkv, rel_bias):
    s = qkv.shape[0]
    assert s % TQ == 0
    nb = len(DILATIONS)
    hp_n = ATT_W // LANES
    tile = lambda f: pl.BlockSpec((TQ, LANES), f)
    return pl.pallas_call(
        _attn_prompt_kernel,
        grid=(hp_n, s // TQ),
        in_specs=[pl.BlockSpec(memory_space=pltpu.SMEM),
                  tile(lambda h, t: (t, h)),
                  tile(lambda h, t: (t, hp_n + h)),
                  tile(lambda h, t: (t, 2 * hp_n + h)),
                  _const_spec((nb, QBLK, 2 * QBLK))],
        out_specs=tile(lambda h, t: (t, h)),
        out_shape=jax.ShapeDtypeStruct((s, ATT_W), F32),
        scratch_shapes=[pltpu.VMEM((nb, TQ, LANES), BF16), pltpu.VMEM((nb, 2, TQ, LANES), BF16),
                        pltpu.VMEM((nb, 2, TQ, LANES), BF16), pltpu.VMEM((STAGE, TQ // STAGE, LANES), F32),
                        pltpu.VMEM((nb, 2, 2 * QBLK, 2 * QBLK), F32),
                        pltpu.VMEM((nb, TQ, LANES), F32), pltpu.VMEM((nb, TQ, LANES), F32),
                        pltpu.VMEM((nb, TQ, LANES), F32)],
        compiler_params=pltpu.CompilerParams(dimension_semantics=("parallel", "arbitrary"),
                                             vmem_limit_bytes=VMEM_LIMIT),
        name="attn_prompt",
    )(rel_bias, qkv, qkv, qkv, _band_buckets())


HPG = SSD_HEADS // N_GROUPS
GROUP_W = SSD_W // N_GROUPS
GROUP_WIN = 3 * LANES
GROUP_LO = tuple((g * GROUP_W) // LANES * LANES for g in range(N_GROUPS))
CARRY_ROW = 8


def _split(x, terms):
    parts = []
    for _ in range(terms):
        p = x.astype(BF16)
        parts.append(p)
        x = x - p.astype(F32)
    return parts


def _dot_exact_rhs(xs_terms, m):
    pieces = [p for x, terms in xs_terms for p in _split(x, terms)]
    out = jnp.dot(jnp.concatenate(pieces, axis=0) if len(pieces) > 1 else pieces[0], m,
                  preferred_element_type=F32)
    res, r = [], 0
    for x, terms in xs_terms:
        n = x.shape[0]
        res.append(sum(out[r + i * n:r + (i + 1) * n] for i in range(terms)))
        r += terms * n
    return res


def _dot_exact_lhs(m, x, terms=3):
    n = x.shape[1]
    out = jnp.dot(m, jnp.concatenate(_split(x, terms), axis=1), preferred_element_type=F32)
    return sum(out[:, i * n:(i + 1) * n] for i in range(terms))


def _softplus(x):
    return jnp.maximum(x, 0.0) + jnp.log1p(jnp.exp(-jnp.abs(x)))


def _silu(x):
    h = 0.5 * x
    return h + h * jnp.tanh(h)


def _ssd_consts():
    c = jnp.arange(SSD_W)
    head_of = (c // SSD_HEAD_DIM)[None, :] == jnp.arange(LANES)[:, None]
    group_of = (c // GROUP_W)[None, :] == jnp.arange(LANES)[:, None]
    return head_of.astype(BF16), group_of.astype(BF16), group_of.T.astype(BF16)


def _conv_silu(cbuf, xbc_ref, w_ref, b_ref, rows):
    out = []
    for c in range(CONV_DIM // LANES):
        cols = slice(c * LANES, (c + 1) * LANES)
        cbuf[c, CARRY_ROW:CARRY_ROW + rows, :] = xbc_ref[:, cols]
        acc = b_ref[:, cols]
        for j in range(CONV_W):
            acc = acc + cbuf[c, pl.ds(CARRY_ROW - (CONV_W - 1) + j, rows), :] * w_ref[j:j + 1, cols]
        out.append(_silu(acc))
    return out


def _drain(steps):
    try:
        while True:
            next(steps)
    except StopIteration as done:
        return done.value


def _interleave(*bodies):
    live = list(bodies)
    while live:
        for body in list(live):
            try:
                next(body)
            except StopIteration:
                live.remove(body)


def _ssd_ydiag(*args):
    return _drain(_ssd_ydiag_steps(*args))


def _ssd_ydiag_steps(xs, bm, cm, acs, dt, mask):
    cbs = [lax.dot_general(cm[g], bm[g], (((1,), (1,)), ((), ())), preferred_element_type=F32)
           for g in range(N_GROUPS)]
    acs_t = acs.T
    dt_t = dt.T
    left = lax.broadcasted_iota(jnp.int32, (1, LANES), 1) < SSD_HEAD_DIM
    pairs = []
    for pair in range(SSD_HEADS // 2):
        xpair = xs[:, pair * LANES:(pair + 1) * LANES]
        acc = None
        for hh in range(2):
            h = 2 * pair + hh
            seg = acs[:, h:h + 1] - acs_t[h:h + 1, :]
            dec = jnp.exp2(jnp.where(mask, seg, NEG_INF))
            w = (cbs[h // HPG] * dec * dt_t[h:h + 1, :]).astype(BF16)
            xh = jnp.where(left if hh == 0 else jnp.logical_not(left), xpair, 0.0).astype(BF16)
            part = jnp.dot(w, xh, preferred_element_type=F32)
            acc = part if acc is None else acc + part
        pairs.append(acc)
        if pair % 2 == 1:
            yield
    return jnp.concatenate(pairs, axis=1)


def _ssd_gate_norm(y, gate, gred, gexp, gout):
    y = y * gate
    ms = _dot_exact_rhs([(y * y, 2)], gred)[0] * (1.0 / GROUP_W)
    return y * _dot_exact_rhs([(lax.rsqrt(ms + EPS), 2)], gexp)[0] * gout


def _ssd_prompt_init(cbuf, state):
    @pl.when(pl.program_id(0) == 0)
    def _init():
        state[...] = jnp.zeros_like(state)
        cbuf[:, 0:CARRY_ROW, :] = jnp.zeros((CONV_DIM // LANES, CARRY_ROW, LANES), F32)


def _ssd_prompt_emit(st_ref, state):
    @pl.when(pl.program_id(0) == pl.num_programs(0) - 1)
    def _emit_state():
        for i in range(SSD_W // LANES):
            st_ref[i * LANES:(i + 1) * LANES, :] = state[:, i * LANES:(i + 1) * LANES].T


def _ssd_prompt_chunk(z_ref, xbc_ref, dt_ref, cw_ref, cb_ref, dtb_ref, alog_ref, dsk_ref, gout_ref,
                      hexp_ref, gexp_ref, gred_ref, y_ref, cbuf, state, yoff):
    cl = SSD_CHUNK
    xc = _conv_silu(cbuf, xbc_ref, cw_ref, cb_ref, cl)
    cbuf[:, CARRY_ROW - (CONV_W - 1):CARRY_ROW, :] = cbuf[:, CARRY_ROW + cl - (CONV_W - 1):CARRY_ROW + cl, :]
    nx = SSD_W // LANES
    xs = jnp.concatenate(xc[:nx], axis=1)
    bm = [b.astype(BF16) for b in xc[nx:nx + N_GROUPS]]
    cm = [b.astype(BF16) for b in xc[nx + N_GROUPS:]]
    yield

    li = lax.broadcasted_iota(jnp.int32, (cl, cl), 0)
    si = lax.broadcasted_iota(jnp.int32, (cl, cl), 1)
    causal = li >= si

    dt = _softplus(dt_ref[...] + dtb_ref[...])
    acs = _dot_exact_lhs(causal.astype(BF16), dt * (-jnp.exp(alog_ref[...]) * LOG2E))
    hexp = hexp_ref[...]
    e_in, e_end = _dot_exact_rhs([(jnp.exp2(acs), 3), (jnp.exp2(acs[cl - 1:cl, :] - acs) * dt, 1)], hexp)
    xw = (xs * e_end).astype(BF16)
    chunk_decay = e_in[cl - 1:cl]
    yield

    gmask = lax.broadcasted_iota(jnp.int32, (1, SSD_W), 1) // GROUP_W
    for g in range(N_GROUPS):
        lo = GROUP_LO[g]
        win = slice(lo, lo + GROUP_WIN)
        inside = gmask[:, win] == g
        yo = jnp.dot(cm[g], state[:, win].astype(BF16), preferred_element_type=F32)
        yoff[:, win] = jnp.where(inside, yo, 0.0 if g == 0 else yoff[:, win])
    yield
    for g in range(N_GROUPS):
        lo = GROUP_LO[g]
        win = slice(lo, lo + GROUP_WIN)
        inside = gmask[:, win] == g
        upd = lax.dot_general(bm[g], xw[:, win], (((0,), (0,)), ((), ())), preferred_element_type=F32)
        old = state[:, win]
        state[:, win] = jnp.where(inside, old * chunk_decay[:, win] + upd, old)
    yield

    ydiag = yield from _ssd_ydiag_steps(xs, bm, cm, acs, dt, causal)
    y_ref[...] = _ssd_gate_norm(ydiag + yoff[...] * e_in + xs * dsk_ref[...], _silu(z_ref[...]),
                                gred_ref[...], gexp_ref[...], gout_ref[...])


N_SSD_IN = 12


def _ssd_prompt_kernel(*refs):
    ins, (y_ref, st_ref, cbuf, state, yoff) = refs[:N_SSD_IN], refs[N_SSD_IN:]
    _ssd_prompt_init(cbuf, state)
    _drain(_ssd_prompt_chunk(*ins, y_ref, cbuf, state, yoff))
    _ssd_prompt_emit(st_ref, state)


def _ssd_prompt_call(z, xbc, dt, conv_w, conv_b, dt_bias, a_log, dskip_x, g_out):
    s = z.shape[0]
    assert s % SSD_CHUNK == 0
    hexp, gexp, gred = _ssd_consts()
    row = lambda n: pl.BlockSpec((SSD_CHUNK, n), lambda i: (i, 0))
    consts = [conv_w, conv_b, dt_bias, a_log, dskip_x, g_out, hexp, gexp, gred]
    return dict(
        steps=s // SSD_CHUNK,
        operands=[z, xbc, dt] + consts,
        in_specs=[row(SSD_W), row(CONV_DIM), row(LANES)] + [_const_spec(a.shape) for a in consts],
        out_specs=[row(SSD_W), pl.BlockSpec((SSD_W, D_STATE), lambda i: (0, 0))],
        out_shape=[jax.ShapeDtypeStruct((s, SSD_W), F32), jax.ShapeDtypeStruct((SSD_W, D_STATE), F32)],
        scratch=[pltpu.VMEM((CONV_DIM // LANES, CARRY_ROW + SSD_CHUNK, LANES), F32),
                 pltpu.VMEM((D_STATE, SSD_W), F32), pltpu.VMEM((SSD_CHUNK, SSD_W), F32)])


def _run(kernel_fn, name, *calls, vmem_limit=VMEM_LIMIT):
    steps = calls[0]["steps"]
    assert all(c["steps"] == steps for c in calls)
    cat = lambda key: [x for c in calls for x in c[key]]
    return pl.pallas_call(
        kernel_fn, grid=(steps,), in_specs=cat("in_specs"), out_specs=cat("out_specs"),
        out_shape=cat("out_shape"), scratch_shapes=cat("scratch"),
        compiler_params=pltpu.CompilerParams(dimension_semantics=("arbitrary",), vmem_limit_bytes=vmem_limit),
        name=name)(*cat("operands"))


def _ssd_prompt(*args):
    return _run(_ssd_prompt_kernel, "ssd_prompt", _ssd_prompt_call(*args))


DEC_SEQ = 4
NEW_ROWS = 8


def _branch_count(dist):
    return sum(((dist % d == 0) & (dist <= QBLK * d)).astype(np.int32) for d in DILATIONS)


def _sample_bias_tables(wb):
    t = np.arange(NEW_ROWS)[:, None]
    dist = wb + t - np.arange(wb)[None, :]
    cnt = np.where(t < DEC_SEQ, _branch_count(dist), 0)
    bkt = np.where(cnt > 0, _rel_bucket(dist), -1).astype(np.int32)
    ladd = np.log(np.maximum(cnt, 1).astype(np.float32))
    u = np.arange(LANES)[None, :]
    dnew = t - u
    cnew = np.where((t < DEC_SEQ) & (dnew > 0), _branch_count(dnew),
                    np.where((t < DEC_SEQ) & (dnew == 0), len(DILATIONS), 0))
    bkt_new = np.where(cnew > 0, _rel_bucket(dnew), -1).astype(np.int32)
    ladd_new = np.log(np.maximum(cnew, 1).astype(np.float32))
    return tuple(jnp.asarray(a) for a in (bkt, ladd, bkt_new, ladd_new))


def _attn_sample_init(tbl_ref, bkt_ref, ladd_ref, bktn_ref, laddn_ref, bias, bias_new):
    @pl.when(pl.program_id(0) == 0)
    def _build_bias():
        for h in range(ATT_HEADS):
            for src, add, dst in ((bkt_ref, ladd_ref, bias), (bktn_ref, laddn_ref, bias_new)):
                bk = src[...]
                acc = jnp.full(bk.shape, NEG_INF, F32)
                for u in range(N_BUCKETS):
                    acc = jnp.where(bk == u, tbl_ref[u, h], acc)
                dst[h // 2, (h % 2) * NEW_ROWS:(h % 2 + 1) * NEW_ROWS, :] = (acc + add[...]) * LOG2E


N_ATTN_S_IN = 8


def _attn_sample_kernel(tbl_ref, q_ref, kt_ref, vt_ref, bkt_ref, ladd_ref, bktn_ref, laddn_ref,
                        o_ref, bias, bias_new):
    _attn_sample_init(tbl_ref, bkt_ref, ladd_ref, bktn_ref, laddn_ref, bias, bias_new)
    whole = lambda hp: (kt_ref.at[0], vt_ref.at[0], hp * LANES)
    _drain(_attn_sample_seq(q_ref, whole, o_ref, bias, bias_new))


RING = 3
RING_VMEM_LIMIT = 58 * 1024 * 1024


def _ring_kv(k_hbm, v_hbm, kring, vring, sems):
    i = pl.program_id(0)
    n = pl.num_programs(0)

    def copies(seq, slot):
        return (pltpu.make_async_copy(k_hbm.at[seq], kring.at[slot], sems.at[0, slot]),
                pltpu.make_async_copy(v_hbm.at[seq], vring.at[slot], sems.at[1, slot]))

    def start(seq, slot):
        for c in copies(jnp.minimum(seq, n - 1), slot):
            c.start()

    @pl.when(i == 0)
    def _prime():
        for s in range(RING - 1):
            start(s, s)

    start(i + RING - 1, (i + RING - 1) % RING)
    slot = i % RING
    for c in copies(i, slot):
        c.wait()

    def finish():
        @pl.when(i == n - 1)
        def _drain_tail():
            for s in range(RING - 1):
                for c in copies(0, (n + s) % RING):
                    c.wait()

    return (lambda hp: (kring.at[slot], vring.at[slot], hp * LANES)), finish


def _attn_sample_seq(q_ref, kv_for_pair, o_ref, bias, bias_new):
    left = lax.broadcasted_iota(jnp.int32, (1, LANES), 1) < HEAD_DIM
    nt = (((1,), (1,)), ((), ()))
    scale = HEAD_DIM ** -0.5 * LOG2E
    outs = []
    for hp in range(ATT_W // LANES):
        cols = slice(hp * LANES, (hp + 1) * LANES)
        k_ref, v_ref, row0 = kv_for_pair(hp)
        kv_rows = slice(row0, row0 + LANES)
        q = q_ref[0, :, cols] * scale
        fill = jnp.zeros((LANES - NEW_ROWS, LANES), F32)
        knew = jnp.concatenate([q_ref[0, :, ATT_W + hp * LANES:ATT_W + (hp + 1) * LANES], fill], axis=0)
        vnew = jnp.concatenate([q_ref[0, :, 2 * ATT_W + hp * LANES:2 * ATT_W + (hp + 1) * LANES], fill], axis=0)
        qq = jnp.concatenate([jnp.where(left, q, 0.0), jnp.where(left, 0.0, q)], axis=0).astype(BF16)
        s = jnp.dot(qq, k_ref[kv_rows, :].astype(BF16), preferred_element_type=F32) + bias[hp]
        sn = lax.dot_general(qq, knew.astype(BF16), nt, preferred_element_type=F32) + bias_new[hp]
        m = jnp.maximum(jnp.max(s, axis=-1, keepdims=True), jnp.max(sn, axis=-1, keepdims=True))
        m = jnp.where(m == NEG_INF, 0.0, m)
        p = jnp.exp2(s - m)
        pn = jnp.exp2(sn - m)
        den = jnp.sum(p, axis=-1, keepdims=True) + jnp.sum(pn, axis=-1, keepdims=True)
        r = lax.dot_general(p.astype(BF16), v_ref[kv_rows, :].astype(BF16), nt, preferred_element_type=F32)
        r = r + jnp.dot(pn.astype(BF16), vnew.astype(BF16), preferred_element_type=F32)
        o = r / jnp.where(den == 0.0, 1.0, den)
        outs.append(jnp.where(left, o[:NEW_ROWS], o[NEW_ROWS:]))
        yield
    o_ref[0] = jnp.concatenate(outs, axis=1)[:DEC_SEQ]


def _attn_sample_call(qkv, cache_kt, cache_vt, rel_bias, ring=False):
    db, _, wb = cache_kt.shape
    assert wb >= WIN_MAX and qkv.shape[0] == db * DEC_SEQ
    q8 = jnp.pad(qkv.reshape(db, DEC_SEQ, 3 * ATT_W), ((0, 0), (0, NEW_ROWS - DEC_SEQ), (0, 0)))
    consts = list(_sample_bias_tables(wb))
    hp_n = ATT_W // LANES
    seq = lambda n, m: pl.BlockSpec((1, n, m), lambda b: (b, 0, 0))
    kv_spec = pl.BlockSpec(memory_space=pl.ANY) if ring else seq(ATT_W, wb)
    scratch = [pltpu.VMEM((hp_n, 2 * NEW_ROWS, wb), F32), pltpu.VMEM((hp_n, 2 * NEW_ROWS, LANES), F32)]
    if ring:
        scratch += [pltpu.VMEM((RING, ATT_W, wb), F32), pltpu.VMEM((RING, ATT_W, wb), F32),
                    pltpu.SemaphoreType.DMA((2, RING))]
    return dict(
        steps=db,
        operands=[rel_bias, q8, cache_kt, cache_vt] + consts,
        in_specs=[pl.BlockSpec(memory_space=pltpu.SMEM), seq(NEW_ROWS, 3 * ATT_W), kv_spec, kv_spec]
                 + [_const_spec(c.shape) for c in consts],
        out_specs=[seq(DEC_SEQ, ATT_W)],
        out_shape=[jax.ShapeDtypeStruct((db, DEC_SEQ, ATT_W), F32)],
        scratch=scratch)


def _attn_sample(qkv, cache_kt, cache_vt, rel_bias):
    out, = _run(_attn_sample_kernel, "attn_sample", _attn_sample_call(qkv, cache_kt, cache_vt, rel_bias))
    return out.reshape(-1, ATT_W)


def _ssd_prompt_attn_sample_kernel(*refs):
    a = N_SSD_IN + N_ATTN_S_IN
    ssd_in = refs[:N_SSD_IN]
    tbl_ref, q_ref, kt_ref, vt_ref, bkt_ref, ladd_ref, bktn_ref, laddn_ref = refs[N_SSD_IN:a]
    y_ref, st_ref, o_ref, cbuf, state, yoff, bias, bias_new, kring, vring, sems = refs[a:]
    _ssd_prompt_init(cbuf, state)
    _attn_sample_init(tbl_ref, bkt_ref, ladd_ref, bktn_ref, laddn_ref, bias, bias_new)
    kv_for_pair, finish_ring = _ring_kv(kt_ref, vt_ref, kring, vring, sems)
    _interleave(_ssd_prompt_chunk(*ssd_in, y_ref, cbuf, state, yoff),
                _attn_sample_seq(q_ref, kv_for_pair, o_ref, bias, bias_new))
    finish_ring()
    _ssd_prompt_emit(st_ref, state)


def _ssd_prompt_attn_sample(ssd_args, attn_args):
    y, st, att = _run(_ssd_prompt_attn_sample_kernel, "ssd_prompt_attn_sample",
                      _ssd_prompt_call(*ssd_args), _attn_sample_call(*attn_args, ring=True),
                      vmem_limit=RING_VMEM_LIMIT)
    return y, st, att.reshape(-1, ATT_W)


SEQ_ROWS = 8
SEQ_PER_STEP = 8


def _ssd_sample_kernel(z_ref, xp_ref, dt_ref, st_ref, cw_ref, cb_ref, dtb_ref, alog_ref, dsk_ref, gout_ref,
                       hexp_ref, gexp_ref, gred_ref, y_ref, sto_ref, cbuf, yoff):
    rows = SEQ_PER_STEP * SEQ_ROWS
    cbuf[:, 0:CARRY_ROW, :] = jnp.zeros((CONV_DIM // LANES, CARRY_ROW, LANES), F32)
    xc = _conv_silu(cbuf, xp_ref, cw_ref, cb_ref, rows)
    nx = SSD_W // LANES
    xs = jnp.concatenate(xc[:nx], axis=1)
    bmf = xc[nx:nx + N_GROUPS]
    bm = [b.astype(BF16) for b in bmf]
    cm = [b.astype(BF16) for b in xc[nx + N_GROUPS:]]

    li = lax.broadcasted_iota(jnp.int32, (rows, rows), 0)
    si = lax.broadcasted_iota(jnp.int32, (rows, rows), 1)
    same = (li // SEQ_ROWS) == (si // SEQ_ROWS)
    mask = same & (li >= si)
    rowi = lax.broadcasted_iota(jnp.int32, (rows, 1), 0)
    is_token = (rowi % SEQ_ROWS) >= SEQ_ROWS - DEC_SEQ

    dt = jnp.where(is_token, _softplus(dt_ref[...] + dtb_ref[...]), 0.0)
    da = dt * (-jnp.exp(alog_ref[...]) * LOG2E)
    acs = _dot_exact_lhs(mask.astype(BF16), da)
    acs_end = _dot_exact_lhs(same.astype(BF16), da)
    hexp = hexp_ref[...]
    e_in, e_end, e_all = _dot_exact_rhs(
        [(jnp.exp2(acs), 2), (jnp.exp2(acs_end - acs) * dt, 1), (jnp.exp2(acs_end), 3)], hexp)
    xw = xs * e_end

    sub = rowi % SEQ_ROWS
    pieces = [p.astype(F32) for p in _split(e_all, 3)]
    dstack = jnp.where(sub == 0, pieces[0], jnp.where(sub == 1, pieces[1], jnp.where(sub == 2, pieces[2], 0.0)))
    ones = jnp.ones((SEQ_ROWS, D_STATE), BF16)
    gmask = lax.broadcasted_iota(jnp.int32, (1, SSD_W), 1) // GROUP_W
    tn = (((0,), (0,)), ((), ()))
    nt = (((1,), (1,)), ((), ()))
    for i in range(SEQ_PER_STEP):
        rs = slice(i * SEQ_ROWS, (i + 1) * SEQ_ROWS)
        for g in range(N_GROUPS):
            lo = GROUP_LO[g]
            win = slice(lo, lo + GROUP_WIN)
            inside = gmask[:, win] == g
            yo = lax.dot_general(cm[g][rs], st_ref[i, win, :].astype(BF16), nt, preferred_element_type=F32)
            yoff[rs, win] = jnp.where(inside, yo, 0.0 if g == 0 else yoff[rs, win])
        xstack = jnp.concatenate([jnp.where(gmask == g, xw[rs, :], 0.0) for g in range(N_GROUPS)], axis=0)
        bstack = jnp.concatenate([bmf[g][rs] for g in range(N_GROUPS)], axis=0)
        upd = lax.dot_general(xstack.astype(BF16), bstack.astype(BF16), tn, preferred_element_type=F32)
        dcol = lax.dot_general(dstack[rs, :].astype(BF16), ones, tn, preferred_element_type=F32)
        sto_ref[i] = st_ref[i] * dcol + upd

    ydiag = _ssd_ydiag(xs, bm, cm, acs, dt, mask)
    y_ref[...] = _ssd_gate_norm(ydiag + yoff[...] * e_in + xs * dsk_ref[...], _silu(z_ref[...]),
                                gred_ref[...], gexp_ref[...], gout_ref[...])


def _ssd_sample(z, xbc, dt, conv_state, ssm_state, conv_w, conv_b, dt_bias, a_log, dskip_x, g_out):
    db = conv_state.shape[0]
    assert db % SEQ_PER_STEP == 0 and z.shape[0] == db * DEC_SEQ
    lead = SEQ_ROWS - DEC_SEQ

    def padded(a):
        a = a.reshape(db, DEC_SEQ, a.shape[-1])
        return jnp.pad(a, ((0, 0), (lead, 0), (0, 0))).reshape(db * SEQ_ROWS, a.shape[-1])

    xp = jnp.concatenate([jnp.zeros((db, lead - (CONV_W - 1), CONV_DIM), F32), conv_state,
                          xbc.reshape(db, DEC_SEQ, CONV_DIM)], axis=1).reshape(db * SEQ_ROWS, CONV_DIM)
    hexp, gexp, gred = _ssd_consts()
    rows = SEQ_PER_STEP * SEQ_ROWS
    row = lambda n: pl.BlockSpec((rows, n), lambda i: (i, 0))
    st_spec = pl.BlockSpec((SEQ_PER_STEP, SSD_W, D_STATE), lambda i: (i, 0, 0))
    consts = [conv_w, conv_b, dt_bias, a_log, dskip_x, g_out, hexp, gexp, gred]
    y, st = pl.pallas_call(
        _ssd_sample_kernel,
        grid=(db // SEQ_PER_STEP,),
        in_specs=[row(SSD_W), row(CONV_DIM), row(LANES), st_spec] + [_const_spec(a.shape) for a in consts],
        out_specs=[row(SSD_W), st_spec],
        out_shape=[jax.ShapeDtypeStruct((db * SEQ_ROWS, SSD_W), F32),
                   jax.ShapeDtypeStruct((db, SSD_W, D_STATE), F32)],
        scratch_shapes=[pltpu.VMEM((CONV_DIM // LANES, CARRY_ROW + rows, LANES), F32),
                        pltpu.VMEM((rows, SSD_W), F32)],
        compiler_params=pltpu.CompilerParams(dimension_semantics=("parallel",),
                                             vmem_limit_bytes=VMEM_LIMIT),
        name="ssd_sample",
    )(padded(z), xp, padded(dt), ssm_state, *consts)
    y = y.reshape(db, SEQ_ROWS, SSD_W)[:, lead:].reshape(db * DEC_SEQ, SSD_W)
    return y, st


def kernel(x_prompt, x_sample, cache_k, cache_v, state_conv, state_ssm, rel_bias, g_mix, w_in, conv_w, conv_b,
           dt_bias, a_log, d_skip, g_attn_out, g_ssd_out, w_out, g_ffn, w_gate, w_up, w_down, g_final):
    depth = w_in.shape[0]
    bp, sp, _ = x_prompt.shape
    db, ds, _ = x_sample.shape
    assert bp == 1 and ds == DEC_SEQ and sp >= WIN_MAX
    xp = x_prompt.reshape(bp * sp, D_MODEL)
    xs = x_sample.reshape(db * ds, D_MODEL)
    pad_heads = lambda v: jnp.pad(v, (0, LANES - SSD_HEADS))[None]
    outs = [[] for _ in range(8)]
    for l in range(depth):
        w_t = w_in[l].T.astype(BF16)
        wdt = jnp.pad(w_t[IN_SPLITS[-1]:], ((0, LANES - SSD_HEADS), (0, 0)))
        ssd_prm = (conv_w[l], conv_b[l][None], pad_heads(dt_bias[l]), pad_heads(a_log[l]),
                   jnp.repeat(d_skip[l], SSD_HEAD_DIM)[None], g_ssd_out[l][None])
        tail_prm = (g_attn_out[l][None], w_out[l][:ATT_W].astype(BF16), w_out[l][ATT_W:].astype(BF16),
                    g_ffn[l][None], w_gate[l].astype(BF16), w_up[l].astype(BF16), w_down[l].astype(BF16),
                    g_final[None])
        last = l == depth - 1

        qkv_p, z_p, xbc_p, dt_p = _inproj(xp, g_mix[l][None], w_t, wdt)
        qkv_s, z_s, xbc_s, dt_s = _inproj(xs, g_mix[l][None], w_t, wdt)
        att_p = _attn_prompt(qkv_p, rel_bias)
        feature_major = lambda c: jnp.transpose(c, (0, 2, 3, 1)).reshape(db, ATT_W, c.shape[1])
        ssd_args = (z_p, xbc_p, dt_p, *ssd_prm)
        attn_args = (qkv_s, feature_major(cache_k[l]), feature_major(cache_v[l]), rel_bias)
        if sp // SSD_CHUNK == db:
            ssd_p, st_p, att_s = _ssd_prompt_attn_sample(ssd_args, attn_args)
        else:
            ssd_p, st_p = _ssd_prompt(*ssd_args)
            att_s = _attn_sample(*attn_args)
        xp = _tail(xp, att_p, ssd_p, *tail_prm, final_norm=last)
        ssd_s, st_s = _ssd_sample(z_s, xbc_s, dt_s, state_conv[l], state_ssm[l].reshape(db, SSD_W, D_STATE),
                                  *ssd_prm)
        xs = _tail(xs, att_s, ssd_s, *tail_prm, final_norm=last)

        heads = lambda a: a.reshape(a.shape[0], a.shape[1], ATT_HEADS, HEAD_DIM)
        kv_p = qkv_p.reshape(bp, sp, 3 * ATT_W)[:, sp - WIN_MAX:]
        kv_s = qkv_s.reshape(db, ds, 3 * ATT_W)
        outs[0].append(heads(kv_p[..., ATT_W:2 * ATT_W]))
        outs[1].append(heads(kv_p[..., 2 * ATT_W:]))
        outs[2].append(xbc_p.reshape(bp, sp, CONV_DIM)[:, sp - (CONV_W - 1):])
        outs[3].append(st_p.reshape(bp, SSD_HEADS, SSD_HEAD_DIM, D_STATE))
        outs[4].append(heads(kv_s[..., ATT_W:2 * ATT_W]))
        outs[5].append(heads(kv_s[..., 2 * ATT_W:]))
        outs[6].append(xbc_s.reshape(db, ds, CONV_DIM)[:, ds - (CONV_W - 1):])
        outs[7].append(st_s.reshape(db, SSD_HEADS, SSD_HEAD_DIM, D_STATE))
    return (xp.reshape(bp, sp, D_MODEL), xs.reshape(db, ds, D_MODEL)) + tuple(jnp.stack(o) for o in outs)
```

```python
import functools
import math

import jax
import jax.numpy as jnp
import numpy as np
from jax import lax
from jax.experimental import pallas as pl
from jax.experimental.pallas import tpu as pltpu

F32 = jnp.float32
BF16 = jnp.bfloat16

D_MODEL = 1024
ATT_HEADS = 12
HEAD_DIM = 64
ATT_W = ATT_HEADS * HEAD_DIM
DILATIONS = (1, 4, 16)
QBLK = 128
WIN_MAX = 2048
N_BUCKETS = 32
MAX_EXACT = N_BUCKETS // 2
SSD_HEADS = 20
SSD_HEAD_DIM = 64
SSD_W = SSD_HEADS * SSD_HEAD_DIM
N_GROUPS = 4
D_STATE = 128
CONV_W = 4
CONV_DIM = SSD_W + 2 * N_GROUPS * D_STATE
SSD_CHUNK = 128
EPS = 1e-6

LANES = 128
VMEM_LIMIT = 48 * 1024 * 1024
TAIL_VMEM_LIMIT = 58 * 1024 * 1024


def _rms(x, g):
    return x * lax.rsqrt(jnp.mean(x * x, axis=-1, keepdims=True) + EPS) * g


def _const_spec(shape):
    nd = len(shape)
    return pl.BlockSpec(shape, lambda *_: (0,) * nd, pipeline_mode=pl.Buffered(1))


IN_SPLITS = (0, 3 * ATT_W, 3 * ATT_W + SSD_W, 3 * ATT_W + SSD_W + CONV_DIM)


def _inproj_kernel(x_ref, g_ref, w_ref, wdt_ref, qkv_ref, z_ref, xbc_ref, dt_ref):
    h = _rms(x_ref[...], g_ref[...]).astype(BF16)
    nt = (((1,), (1,)), ((), ()))
    for out, lo, hi in zip((qkv_ref, z_ref, xbc_ref), IN_SPLITS[:-1], IN_SPLITS[1:]):
        out[...] = lax.dot_general(h, w_ref[lo:hi, :], nt, preferred_element_type=F32)
    dt_ref[...] = lax.dot_general(h, wdt_ref[...], nt, preferred_element_type=F32)


def _inproj(x, g_mix, w_t, wdt):
    t = x.shape[0]
    tm = 512 if t % 512 == 0 and t > 512 else 256
    assert t % tm == 0
    row = lambda n: pl.BlockSpec((tm, n), lambda i: (i, 0))
    return pl.pallas_call(
        _inproj_kernel,
        grid=(t // tm,),
        in_specs=[row(D_MODEL), _const_spec((1, D_MODEL)), _const_spec(w_t.shape), _const_spec(wdt.shape)],
        out_specs=[row(3 * ATT_W), row(SSD_W), row(CONV_DIM), row(LANES)],
        out_shape=[jax.ShapeDtypeStruct((t, 3 * ATT_W), F32), jax.ShapeDtypeStruct((t, SSD_W), F32),
                   jax.ShapeDtypeStruct((t, CONV_DIM), F32), jax.ShapeDtypeStruct((t, LANES), F32)],
        compiler_params=pltpu.CompilerParams(dimension_semantics=("parallel",),
                                             vmem_limit_bytes=VMEM_LIMIT),
        name="inproj",
    )(x, g_mix, w_t, wdt)


def _tail_kernel(x_ref, att_ref, ssd_ref, gatt_ref, woa_ref, wos_ref, gffn_ref,
                 wg_ref, wu_ref, wd_ref, gfin_ref, y_ref, *, final_norm):
    an = _rms(att_ref[...], gatt_ref[...]).astype(BF16)
    mix = jnp.dot(an, woa_ref[...], preferred_element_type=F32)
    mix = mix + jnp.dot(ssd_ref[...].astype(BF16), wos_ref[...], preferred_element_type=F32)
    x1 = x_ref[...] + mix
    h2 = _rms(x1, gffn_ref[...]).astype(BF16)
    gate = jnp.dot(h2, wg_ref[...], preferred_element_type=F32)
    up = jnp.dot(h2, wu_ref[...], preferred_element_type=F32)
    act = (gate * jax.nn.sigmoid(gate) * up).astype(BF16)
    x2 = x1 + jnp.dot(act, wd_ref[...], preferred_element_type=F32)
    y_ref[...] = _rms(x2, gfin_ref[...]) if final_norm else x2


def _tail(x, att, ssd, g_att, woa, wos, g_ffn, wg, wu, wd, g_fin, final_norm):
    t = x.shape[0]
    tm = 512 if t % 512 == 0 and t > 512 else 256
    assert t % tm == 0
    row = lambda n: pl.BlockSpec((tm, n), lambda i: (i, 0))
    consts = [g_att, woa, wos, g_ffn, wg, wu, wd, g_fin]
    return pl.pallas_call(
        functools.partial(_tail_kernel, final_norm=final_norm),
        grid=(t // tm,),
        in_specs=[row(D_MODEL), row(ATT_W), row(SSD_W)] + [_const_spec(c.shape) for c in consts],
        out_specs=row(D_MODEL),
        out_shape=jax.ShapeDtypeStruct((t, D_MODEL), F32),
        compiler_params=pltpu.CompilerParams(dimension_semantics=("parallel",),
                                             vmem_limit_bytes=TAIL_VMEM_LIMIT),
        name="tail",
    )(x, att, ssd, *consts)


TQ = QBLK * max(DILATIONS)
NEG_INF = float("-inf")
LOG2E = math.log2(math.e)


def _rel_bucket(dist):
    n = np.maximum(dist, 0)
    nf = np.maximum(n, 1).astype(np.float32)
    large = MAX_EXACT + (np.log(nf / MAX_EXACT) / math.log(WIN_MAX / MAX_EXACT)
                         * (N_BUCKETS - MAX_EXACT)).astype(np.int32)
    return np.where(n < MAX_EXACT, n, np.minimum(large, N_BUCKETS - 1))


BAND_ROWS = 8


def _band_buckets():
    dist = (QBLK - np.arange(2 * QBLK)) % (2 * QBLK)
    valid = dist <= QBLK
    row0 = np.stack([np.where(valid, _rel_bucket(dist * d), -1) for d in DILATIONS]).astype(np.int32)
    return jnp.asarray(np.broadcast_to(row0[:, None, :], (len(DILATIONS), BAND_ROWS, 2 * QBLK)).copy())


STAGE = 4
assert DILATIONS == (1, STAGE, STAGE * STAGE)


def _regroup(src_ref, tmp, dsts, scale=None):
    cast = (lambda x: x.astype(BF16)) if scale is None else (lambda x: (x * scale).astype(BF16))
    sub = TQ // STAGE
    for c in range(STAGE):
        dsts[0][c * sub:(c + 1) * sub, :] = cast(src_ref[c * sub:(c + 1) * sub, :])
    for r in range(STAGE):
        x = src_ref[pl.ds(r, sub, stride=STAGE), :]
        tmp[r] = x
        dsts[1][r * sub:(r + 1) * sub, :] = cast(x)
    for r in range(STAGE):
        for r2 in range(STAGE):
            res = r + STAGE * r2
            dsts[2][res * QBLK:(res + 1) * QBLK, :] = cast(tmp[r, pl.ds(r2, QBLK, stride=STAGE), :])


def _attn_prompt_kernel(tbl_ref, q_ref, k_ref, v_ref, bkt_ref, o_ref,
                        qd, kd, vd, tmp, bias, macc, nacc, dacc):
    hp = pl.program_id(0)
    t = pl.program_id(1)
    nb = len(DILATIONS)
    slot = t % 2

    @pl.when(t == 0)
    def _no_previous_tile():
        kd[:, 1] = jnp.zeros((nb, TQ, LANES), BF16)
        vd[:, 1] = jnp.zeros((nb, TQ, LANES), BF16)

    @pl.when(t == 0)
    def _build_bias():
        prev_half = lax.broadcasted_iota(jnp.int32, (1, 2 * QBLK), 1) < QBLK
        for b in range(len(DILATIONS)):
            bk = bkt_ref[b]
            for hh in range(2):
                row = jnp.full((BAND_ROWS, 2 * QBLK), NEG_INF, F32)
                for u in range(N_BUCKETS):
                    row = jnp.where(bk == u, tbl_ref[u, 2 * hp + hh], row)
                row = row * LOG2E
                acc = pltpu.roll(jnp.concatenate([row] * (QBLK // BAND_ROWS), axis=0), 0, axis=1,
                                 stride=1, stride_axis=0)
                bias[b, 0, hh * QBLK:(hh + 1) * QBLK] = acc
                bias[b, 1, hh * QBLK:(hh + 1) * QBLK] = jnp.where(prev_half, NEG_INF, acc)

    _regroup(q_ref, tmp, [qd.at[b] for b in range(nb)], HEAD_DIM ** -0.5 * LOG2E)
    _regroup(k_ref, tmp, [kd.at[b, slot] for b in range(nb)])
    _regroup(v_ref, tmp, [vd.at[b, slot] for b in range(nb)])

    left = lax.broadcasted_iota(jnp.int32, (1, LANES), 1) < HEAD_DIM
    nt = (((1,), (1,)), ((), ()))

    for b, d in enumerate(DILATIONS):
        nbr = max(DILATIONS) // d

        def rows(start, size, d=d):
            return pl.ds(start, size) if d == 1 else pl.ds(start, size, stride=d)

        def block(j, carry, b=b, d=d, nbr=nbr, rows=rows):
            r = j // nbr
            m = j % nbr
            qs = r + d * QBLK * m
            gs = pl.multiple_of((r * nbr + m) * QBLK, QBLK)
            wrap = m == 0
            ps = pl.multiple_of((r * nbr + jnp.where(wrap, nbr - 1, m - 1)) * QBLK, QBLK)
            pslot = jnp.where(wrap, 1 - slot, slot)
            q = qd[b, pl.ds(gs, QBLK), :]
            kc = jnp.concatenate([kd[b, pslot, pl.ds(ps, QBLK), :], kd[b, slot, pl.ds(gs, QBLK), :]], axis=0)
            vc = jnp.concatenate([vd[b, pslot, pl.ds(ps, QBLK), :], vd[b, slot, pl.ds(gs, QBLK), :]], axis=0)
            first = jnp.logical_and(wrap, t == 0).astype(jnp.int32)
            zero = jnp.zeros_like(q)
            qq = jnp.concatenate([jnp.where(left, q, zero), jnp.where(left, zero, q)], axis=0)
            s = lax.dot_general(qq, kc, nt, preferred_element_type=F32) + bias[b, first]
            mx = jnp.max(s, axis=-1, keepdims=True)
            e = jnp.exp2(s - mx)
            den = jnp.sum(e, axis=-1, keepdims=True)
            res = jnp.dot(e.astype(BF16), vc, preferred_element_type=F32)
            nacc[b, rows(qs, QBLK), :] = jnp.where(left, res[:QBLK], res[QBLK:])
            dacc[b, rows(qs, QBLK), :] = jnp.where(left, den[:QBLK], den[QBLK:])
            macc[b, rows(qs, QBLK), :] = jnp.where(left, mx[:QBLK], mx[QBLK:])
            return carry

        lax.fori_loop(0, max(DILATIONS), block, 0, unroll=True)

    def merge(c, carry):
        sl = pl.ds(pl.multiple_of(c * QBLK, QBLK), QBLK)
        ms = [macc[b, sl, :] for b in range(len(DILATIONS))]
        top = jnp.maximum(jnp.maximum(ms[0], ms[1]), ms[2])
        ws = [jnp.exp2(mb - top) for mb in ms]
        num = sum(w * nacc[b, sl, :] for b, w in enumerate(ws))
        den = sum(w * dacc[b, sl, :] for b, w in enumerate(ws))
        o_ref[sl, :] = num / den
        return carry

    lax.fori_loop(0, TQ // QBLK, merge, 0)


def _attn_prompt(qkv, rel_bias):
    s = qkv.shape[0]
    assert s % TQ == 0
    nb = len(DILATIONS)
    hp_n = ATT_W // LANES
    tile = lambda f: pl.BlockSpec((TQ, LANES), f)
    return pl.pallas_call(
        _attn_prompt_kernel,
        grid=(hp_n, s // TQ),
        in_specs=[pl.BlockSpec(memory_space=pltpu.SMEM),
                  tile(lambda h, t: (t, h)),
                  tile(lambda h, t: (t, hp_n + h)),
                  tile(lambda h, t: (t, 2 * hp_n + h)),
                  _const_spec((nb, BAND_ROWS, 2 * QBLK))],
        out_specs=tile(lambda h, t: (t, h)),
        out_shape=jax.ShapeDtypeStruct((s, ATT_W), F32),
        scratch_shapes=[pltpu.VMEM((nb, TQ, LANES), BF16), pltpu.VMEM((nb, 2, TQ, LANES), BF16),
                        pltpu.VMEM((nb, 2, TQ, LANES), BF16), pltpu.VMEM((STAGE, TQ // STAGE, LANES), F32),
                        pltpu.VMEM((nb, 2, 2 * QBLK, 2 * QBLK), F32),
                        pltpu.VMEM((nb, TQ, LANES), F32), pltpu.VMEM((nb, TQ, LANES), F32),
                        pltpu.VMEM((nb, TQ, LANES), F32)],
        compiler_params=pltpu.CompilerParams(dimension_semantics=("parallel", "arbitrary"),
                                             vmem_limit_bytes=VMEM_LIMIT),
        name="attn_prompt",
    )(rel_bias, qkv, qkv, qkv, _band_buckets())


HPG = SSD_HEADS // N_GROUPS
GROUP_W = SSD_W // N_GROUPS
GROUP_WIN = 3 * LANES
GROUP_LO = tuple((g * GROUP_W) // LANES * LANES for g in range(N_GROUPS))
CARRY_ROW = 8


def _split(x, terms):
    parts = []
    for _ in range(terms):
        p = x.astype(BF16)
        parts.append(p)
        x = x - p.astype(F32)
    return parts


def _dot_exact_rhs(xs_terms, m):
    pieces = [p for x, terms in xs_terms for p in _split(x, terms)]
    out = jnp.dot(jnp.concatenate(pieces, axis=0) if len(pieces) > 1 else pieces[0], m,
                  preferred_element_type=F32)
    res, r = [], 0
    for x, terms in xs_terms:
        n = x.shape[0]
        res.append(sum(out[r + i * n:r + (i + 1) * n] for i in range(terms)))
        r += terms * n
    return res


def _dot_exact_lhs(m, x, terms=3):
    n = x.shape[1]
    out = jnp.dot(m, jnp.concatenate(_split(x, terms), axis=1), preferred_element_type=F32)
    return sum(out[:, i * n:(i + 1) * n] for i in range(terms))


def _softplus(x):
    return jnp.maximum(x, 0.0) + jnp.log1p(jnp.exp(-jnp.abs(x)))


def _silu(x):
    h = 0.5 * x
    return h + h * jnp.tanh(h)


def _ssd_consts():
    c = jnp.arange(SSD_W)
    head_of = (c // SSD_HEAD_DIM)[None, :] == jnp.arange(LANES)[:, None]
    group_of = (c // GROUP_W)[None, :] == jnp.arange(LANES)[:, None]
    return head_of.astype(BF16), group_of.astype(BF16), group_of.T.astype(BF16)


def _conv_silu(cbuf, xbc_ref, w_ref, b_ref, rows):
    out = []
    for c in range(CONV_DIM // LANES):
        cols = slice(c * LANES, (c + 1) * LANES)
        cbuf[c, CARRY_ROW:CARRY_ROW + rows, :] = xbc_ref[:, cols]
        acc = b_ref[:, cols]
        for j in range(CONV_W):
            acc = acc + cbuf[c, pl.ds(CARRY_ROW - (CONV_W - 1) + j, rows), :] * w_ref[j:j + 1, cols]
        out.append(_silu(acc))
    return out


def _drain(steps):
    try:
        while True:
            next(steps)
    except StopIteration as done:
        return done.value


def _interleave(*bodies):
    live = list(bodies)
    while live:
        for body in list(live):
            try:
                next(body)
            except StopIteration:
                live.remove(body)


def _ssd_ydiag(*args):
    return _drain(_ssd_ydiag_steps(*args))


def _ssd_ydiag_steps(xs, bm, cm, acs, dt, mask):
    cbs = [lax.dot_general(cm[g], bm[g], (((1,), (1,)), ((), ())), preferred_element_type=F32)
           for g in range(N_GROUPS)]
    acs_t = acs.T
    dt_t = dt.T
    left = lax.broadcasted_iota(jnp.int32, (1, LANES), 1) < SSD_HEAD_DIM
    pairs = []
    for pair in range(SSD_HEADS // 2):
        xpair = xs[:, pair * LANES:(pair + 1) * LANES]
        acc = None
        for hh in range(2):
            h = 2 * pair + hh
            seg = acs[:, h:h + 1] - acs_t[h:h + 1, :]
            dec = jnp.exp2(jnp.where(mask, seg, NEG_INF))
            w = (cbs[h // HPG] * dec * dt_t[h:h + 1, :]).astype(BF16)
            xh = jnp.where(left if hh == 0 else jnp.logical_not(left), xpair, 0.0).astype(BF16)
            part = jnp.dot(w, xh, preferred_element_type=F32)
            acc = part if acc is None else acc + part
        pairs.append(acc)
        if pair % 2 == 1:
            yield
    return jnp.concatenate(pairs, axis=1)


def _ssd_gate_norm(y, gate, gred, gexp, gout):
    y = y * gate
    ms = _dot_exact_rhs([(y * y, 2)], gred)[0] * (1.0 / GROUP_W)
    return y * _dot_exact_rhs([(lax.rsqrt(ms + EPS), 2)], gexp)[0] * gout


def _ssd_prompt_init(cbuf, state):
    @pl.when(pl.program_id(0) == 0)
    def _init():
        state[...] = jnp.zeros_like(state)
        cbuf[:, 0:CARRY_ROW, :] = jnp.zeros((CONV_DIM // LANES, CARRY_ROW, LANES), F32)


def _ssd_prompt_emit(st_ref, state):
    @pl.when(pl.program_id(0) == pl.num_programs(0) - 1)
    def _emit_state():
        for i in range(SSD_W // LANES):
            st_ref[i * LANES:(i + 1) * LANES, :] = state[:, i * LANES:(i + 1) * LANES].T


def _ssd_prompt_chunk(z_ref, xbc_ref, dt_ref, cw_ref, cb_ref, dtb_ref, alog_ref, dsk_ref, gout_ref,
                      hexp_ref, gexp_ref, gred_ref, y_ref, cbuf, state, yoff):
    cl = SSD_CHUNK
    xc = _conv_silu(cbuf, xbc_ref, cw_ref, cb_ref, cl)
    cbuf[:, CARRY_ROW - (CONV_W - 1):CARRY_ROW, :] = cbuf[:, CARRY_ROW + cl - (CONV_W - 1):CARRY_ROW + cl, :]
    nx = SSD_W // LANES
    xs = jnp.concatenate(xc[:nx], axis=1)
    bm = [b.astype(BF16) for b in xc[nx:nx + N_GROUPS]]
    cm = [b.astype(BF16) for b in xc[nx + N_GROUPS:]]
    yield

    li = lax.broadcasted_iota(jnp.int32, (cl, cl), 0)
    si = lax.broadcasted_iota(jnp.int32, (cl, cl), 1)
    causal = li >= si

    dt = _softplus(dt_ref[...] + dtb_ref[...])
    acs = _dot_exact_lhs(causal.astype(BF16), dt * (-jnp.exp(alog_ref[...]) * LOG2E))
    hexp = hexp_ref[...]
    e_in, e_end = _dot_exact_rhs([(jnp.exp2(acs), 3), (jnp.exp2(acs[cl - 1:cl, :] - acs) * dt, 1)], hexp)
    xw = (xs * e_end).astype(BF16)
    chunk_decay = e_in[cl - 1:cl]
    yield

    gmask = lax.broadcasted_iota(jnp.int32, (1, SSD_W), 1) // GROUP_W
    for g in range(N_GROUPS):
        lo = GROUP_LO[g]
        win = slice(lo, lo + GROUP_WIN)
        inside = gmask[:, win] == g
        yo = jnp.dot(cm[g], state[:, win].astype(BF16), preferred_element_type=F32)
        yoff[:, win] = jnp.where(inside, yo, 0.0 if g == 0 else yoff[:, win])
    yield
    for g in range(N_GROUPS):
        lo = GROUP_LO[g]
        win = slice(lo, lo + GROUP_WIN)
        inside = gmask[:, win] == g
        upd = lax.dot_general(bm[g], xw[:, win], (((0,), (0,)), ((), ())), preferred_element_type=F32)
        old = state[:, win]
        state[:, win] = jnp.where(inside, old * chunk_decay[:, win] + upd, old)
    yield

    ydiag = yield from _ssd_ydiag_steps(xs, bm, cm, acs, dt, causal)
    y_ref[...] = _ssd_gate_norm(ydiag + yoff[...] * e_in + xs * dsk_ref[...], _silu(z_ref[...]),
                                gred_ref[...], gexp_ref[...], gout_ref[...])


N_SSD_IN = 12


def _ssd_prompt_kernel(*refs):
    ins, (y_ref, st_ref, cbuf, state, yoff) = refs[:N_SSD_IN], refs[N_SSD_IN:]
    _ssd_prompt_init(cbuf, state)
    _drain(_ssd_prompt_chunk(*ins, y_ref, cbuf, state, yoff))
    _ssd_prompt_emit(st_ref, state)


def _ssd_prompt_call(z, xbc, dt, conv_w, conv_b, dt_bias, a_log, dskip_x, g_out):
    s = z.shape[0]
    assert s % SSD_CHUNK == 0
    hexp, gexp, gred = _ssd_consts()
    row = lambda n: pl.BlockSpec((SSD_CHUNK, n), lambda i: (i, 0))
    consts = [conv_w, conv_b, dt_bias, a_log, dskip_x, g_out, hexp, gexp, gred]
    return dict(
        steps=s // SSD_CHUNK,
        operands=[z, xbc, dt] + consts,
        in_specs=[row(SSD_W), row(CONV_DIM), row(LANES)] + [_const_spec(a.shape) for a in consts],
        out_specs=[row(SSD_W), pl.BlockSpec((SSD_W, D_STATE), lambda i: (0, 0))],
        out_shape=[jax.ShapeDtypeStruct((s, SSD_W), F32), jax.ShapeDtypeStruct((SSD_W, D_STATE), F32)],
        scratch=[pltpu.VMEM((CONV_DIM // LANES, CARRY_ROW + SSD_CHUNK, LANES), F32),
                 pltpu.VMEM((D_STATE, SSD_W), F32), pltpu.VMEM((SSD_CHUNK, SSD_W), F32)])


def _run(kernel_fn, name, *calls, vmem_limit=VMEM_LIMIT):
    steps = calls[0]["steps"]
    assert all(c["steps"] == steps for c in calls)
    cat = lambda key: [x for c in calls for x in c[key]]
    return pl.pallas_call(
        kernel_fn, grid=(steps,), in_specs=cat("in_specs"), out_specs=cat("out_specs"),
        out_shape=cat("out_shape"), scratch_shapes=cat("scratch"),
        compiler_params=pltpu.CompilerParams(dimension_semantics=("arbitrary",), vmem_limit_bytes=vmem_limit),
        name=name)(*cat("operands"))


def _ssd_prompt(*args):
    return _run(_ssd_prompt_kernel, "ssd_prompt", _ssd_prompt_call(*args))


DEC_SEQ = 4
NEW_ROWS = 8


def _branch_count(dist):
    return sum(((dist % d == 0) & (dist <= QBLK * d)).astype(np.int32) for d in DILATIONS)


def _sample_bias_tables(wb):
    t = np.arange(NEW_ROWS)[:, None]
    dist = wb + t - np.arange(wb)[None, :]
    cnt = np.where(t < DEC_SEQ, _branch_count(dist), 0)
    bkt = np.where(cnt > 0, _rel_bucket(dist), -1).astype(np.int32)
    ladd = np.log(np.maximum(cnt, 1).astype(np.float32))
    u = np.arange(LANES)[None, :]
    dnew = t - u
    cnew = np.where((t < DEC_SEQ) & (dnew > 0), _branch_count(dnew),
                    np.where((t < DEC_SEQ) & (dnew == 0), len(DILATIONS), 0))
    bkt_new = np.where(cnew > 0, _rel_bucket(dnew), -1).astype(np.int32)
    ladd_new = np.log(np.maximum(cnew, 1).astype(np.float32))
    return tuple(jnp.asarray(a) for a in (bkt, ladd, bkt_new, ladd_new))


def _attn_sample_init(tbl_ref, bkt_ref, ladd_ref, bktn_ref, laddn_ref, bias, bias_new):
    @pl.when(pl.program_id(0) == 0)
    def _build_bias():
        for h in range(ATT_HEADS):
            for src, add, dst in ((bkt_ref, ladd_ref, bias), (bktn_ref, laddn_ref, bias_new)):
                bk = src[...]
                acc = jnp.full(bk.shape, NEG_INF, F32)
                for u in range(N_BUCKETS):
                    acc = jnp.where(bk == u, tbl_ref[u, h], acc)
                dst[h // 2, (h % 2) * NEW_ROWS:(h % 2 + 1) * NEW_ROWS, :] = (acc + add[...]) * LOG2E


N_ATTN_S_IN = 8


def _attn_sample_kernel(tbl_ref, q_ref, kt_ref, vt_ref, bkt_ref, ladd_ref, bktn_ref, laddn_ref,
                        o_ref, bias, bias_new):
    _attn_sample_init(tbl_ref, bkt_ref, ladd_ref, bktn_ref, laddn_ref, bias, bias_new)
    whole = lambda hp: (kt_ref.at[0], vt_ref.at[0], hp * LANES)
    _drain(_attn_sample_seq(q_ref, whole, o_ref, bias, bias_new))


RING = 3
RING_VMEM_LIMIT = 58 * 1024 * 1024


def _ring_kv(k_hbm, v_hbm, kring, vring, sems):
    i = pl.program_id(0)
    n = pl.num_programs(0)

    def copies(seq, slot):
        return (pltpu.make_async_copy(k_hbm.at[seq], kring.at[slot], sems.at[0, slot]),
                pltpu.make_async_copy(v_hbm.at[seq], vring.at[slot], sems.at[1, slot]))

    def start(seq, slot):
        for c in copies(jnp.minimum(seq, n - 1), slot):
            c.start()

    @pl.when(i == 0)
    def _prime():
        for s in range(RING - 1):
            start(s, s)

    start(i + RING - 1, (i + RING - 1) % RING)
    slot = i % RING
    for c in copies(i, slot):
        c.wait()

    def finish():
        @pl.when(i == n - 1)
        def _drain_tail():
            for s in range(RING - 1):
                for c in copies(0, (n + s) % RING):
                    c.wait()

    return (lambda hp: (kring.at[slot], vring.at[slot], hp * LANES)), finish


def _attn_sample_seq(q_ref, kv_for_pair, o_ref, bias, bias_new):
    left = lax.broadcasted_iota(jnp.int32, (1, LANES), 1) < HEAD_DIM
    nt = (((1,), (1,)), ((), ()))
    scale = HEAD_DIM ** -0.5 * LOG2E
    outs = []
    for hp in range(ATT_W // LANES):
        cols = slice(hp * LANES, (hp + 1) * LANES)
        k_ref, v_ref, row0 = kv_for_pair(hp)
        kv_rows = slice(row0, row0 + LANES)
        q = q_ref[0, :, cols] * scale
        fill = jnp.zeros((LANES - NEW_ROWS, LANES), F32)
        knew = jnp.concatenate([q_ref[0, :, ATT_W + hp * LANES:ATT_W + (hp + 1) * LANES], fill], axis=0)
        vnew = jnp.concatenate([q_ref[0, :, 2 * ATT_W + hp * LANES:2 * ATT_W + (hp + 1) * LANES], fill], axis=0)
        qq = jnp.concatenate([jnp.where(left, q, 0.0), jnp.where(left, 0.0, q)], axis=0).astype(BF16)
        s = jnp.dot(qq, k_ref[kv_rows, :].astype(BF16), preferred_element_type=F32) + bias[hp]
        sn = lax.dot_general(qq, knew.astype(BF16), nt, preferred_element_type=F32) + bias_new[hp]
        m = jnp.maximum(jnp.max(s, axis=-1, keepdims=True), jnp.max(sn, axis=-1, keepdims=True))
        m = jnp.where(m == NEG_INF, 0.0, m)
        p = jnp.exp2(s - m)
        pn = jnp.exp2(sn - m)
        den = jnp.sum(p, axis=-1, keepdims=True) + jnp.sum(pn, axis=-1, keepdims=True)
        r = lax.dot_general(p.astype(BF16), v_ref[kv_rows, :].astype(BF16), nt, preferred_element_type=F32)
        r = r + jnp.dot(pn.astype(BF16), vnew.astype(BF16), preferred_element_type=F32)
        o = r / jnp.where(den == 0.0, 1.0, den)
        outs.append(jnp.where(left, o[:NEW_ROWS], o[NEW_ROWS:]))
        yield
    o_ref[0] = jnp.concatenate(outs, axis=1)[:DEC_SEQ]


def _attn_sample_call(qkv, cache_kt, cache_vt, rel_bias, ring=False):
    db, _, wb = cache_kt.shape
    assert wb >= WIN_MAX and qkv.shape[0] == db * DEC_SEQ
    q8 = jnp.pad(qkv.reshape(db, DEC_SEQ, 3 * ATT_W), ((0, 0), (0, NEW_ROWS - DEC_SEQ), (0, 0)))
    consts = list(_sample_bias_tables(wb))
    hp_n = ATT_W // LANES
    seq = lambda n, m: pl.BlockSpec((1, n, m), lambda b: (b, 0, 0))
    kv_spec = pl.BlockSpec(memory_space=pl.ANY) if ring else seq(ATT_W, wb)
    scratch = [pltpu.VMEM((hp_n, 2 * NEW_ROWS, wb), F32), pltpu.VMEM((hp_n, 2 * NEW_ROWS, LANES), F32)]
    if ring:
        scratch += [pltpu.VMEM((RING, ATT_W, wb), F32), pltpu.VMEM((RING, ATT_W, wb), F32),
                    pltpu.SemaphoreType.DMA((2, RING))]
    return dict(
        steps=db,
        operands=[rel_bias, q8, cache_kt, cache_vt] + consts,
        in_specs=[pl.BlockSpec(memory_space=pltpu.SMEM), seq(NEW_ROWS, 3 * ATT_W), kv_spec, kv_spec]
                 + [_const_spec(c.shape) for c in consts],
        out_specs=[seq(DEC_SEQ, ATT_W)],
        out_shape=[jax.ShapeDtypeStruct((db, DEC_SEQ, ATT_W), F32)],
        scratch=scratch)


def _attn_sample(qkv, cache_kt, cache_vt, rel_bias):
    out, = _run(_attn_sample_kernel, "attn_sample", _attn_sample_call(qkv, cache_kt, cache_vt, rel_bias))
    return out.reshape(-1, ATT_W)


def _ssd_prompt_attn_sample_kernel(*refs):
    a = N_SSD_IN + N_ATTN_S_IN
    ssd_in = refs[:N_SSD_IN]
    tbl_ref, q_ref, kt_ref, vt_ref, bkt_ref, ladd_ref, bktn_ref, laddn_ref = refs[N_SSD_IN:a]
    y_ref, st_ref, o_ref, cbuf, state, yoff, bias, bias_new, kring, vring, sems = refs[a:]
    _ssd_prompt_init(cbuf, state)
    _attn_sample_init(tbl_ref, bkt_ref, ladd_ref, bktn_ref, laddn_ref, bias, bias_new)
    kv_for_pair, finish_ring = _ring_kv(kt_ref, vt_ref, kring, vring, sems)
    _interleave(_ssd_prompt_chunk(*ssd_in, y_ref, cbuf, state, yoff),
                _attn_sample_seq(q_ref, kv_for_pair, o_ref, bias, bias_new))
    finish_ring()
    _ssd_prompt_emit(st_ref, state)


def _ssd_prompt_attn_sample(ssd_args, attn_args):
    y, st, att = _run(_ssd_prompt_attn_sample_kernel, "ssd_prompt_attn_sample",
                      _ssd_prompt_call(*ssd_args), _attn_sample_call(*attn_args, ring=True),
                      vmem_limit=RING_VMEM_LIMIT)
    return y, st, att.reshape(-1, ATT_W)


SEQ_ROWS = 8
SEQ_PER_STEP = 8


def _ssd_sample_kernel(z_ref, xp_ref, dt_ref, st_hbm, cw_ref, cb_ref, dtb_ref, alog_ref, dsk_ref, gout_ref,
                       hexp_ref, gexp_ref, gred_ref, y_ref, sto_ref, cbuf, yoff, st_ring, sems):
    step = pl.program_id(0)
    n_steps = pl.num_programs(0)

    def state_copy(s, slot):
        first = pl.multiple_of(jnp.minimum(s, n_steps - 1) * SEQ_PER_STEP, SEQ_PER_STEP)
        return pltpu.make_async_copy(st_hbm.at[pl.ds(first, SEQ_PER_STEP)], st_ring.at[slot], sems.at[slot])

    @pl.when(step == 0)
    def _prime():
        for s in range(RING - 1):
            state_copy(s, s).start()

    state_copy(step + RING - 1, (step + RING - 1) % RING).start()
    state_copy(step, step % RING).wait()
    st_ref = st_ring.at[step % RING]

    rows = SEQ_PER_STEP * SEQ_ROWS
    cbuf[:, 0:CARRY_ROW, :] = jnp.zeros((CONV_DIM // LANES, CARRY_ROW, LANES), F32)
    xc = _conv_silu(cbuf, xp_ref, cw_ref, cb_ref, rows)
    nx = SSD_W // LANES
    xs = jnp.concatenate(xc[:nx], axis=1)
    bmf = xc[nx:nx + N_GROUPS]
    bm = [b.astype(BF16) for b in bmf]
    cm = [b.astype(BF16) for b in xc[nx + N_GROUPS:]]

    li = lax.broadcasted_iota(jnp.int32, (rows, rows), 0)
    si = lax.broadcasted_iota(jnp.int32, (rows, rows), 1)
    same = (li // SEQ_ROWS) == (si // SEQ_ROWS)
    mask = same & (li >= si)
    rowi = lax.broadcasted_iota(jnp.int32, (rows, 1), 0)
    is_token = (rowi % SEQ_ROWS) >= SEQ_ROWS - DEC_SEQ

    dt = jnp.where(is_token, _softplus(dt_ref[...] + dtb_ref[...]), 0.0)
    da = dt * (-jnp.exp(alog_ref[...]) * LOG2E)
    acs = _dot_exact_lhs(mask.astype(BF16), da)
    acs_end = _dot_exact_lhs(same.astype(BF16), da)
    hexp = hexp_ref[...]
    e_in, e_end, e_all = _dot_exact_rhs(
        [(jnp.exp2(acs), 2), (jnp.exp2(acs_end - acs) * dt, 1), (jnp.exp2(acs_end), 3)], hexp)
    xw = xs * e_end

    sub = rowi % SEQ_ROWS
    pieces = [p.astype(F32) for p in _split(e_all, 3)]
    dstack = jnp.where(sub == 0, pieces[0], jnp.where(sub == 1, pieces[1], jnp.where(sub == 2, pieces[2], 0.0)))
    ones = jnp.ones((SEQ_ROWS, D_STATE), BF16)
    gmask = lax.broadcasted_iota(jnp.int32, (1, SSD_W), 1) // GROUP_W
    tn = (((0,), (0,)), ((), ()))
    nt = (((1,), (1,)), ((), ()))
    for i in range(SEQ_PER_STEP):
        rs = slice(i * SEQ_ROWS, (i + 1) * SEQ_ROWS)
        for g in range(N_GROUPS):
            lo = GROUP_LO[g]
            win = slice(lo, lo + GROUP_WIN)
            inside = gmask[:, win] == g
            yo = lax.dot_general(cm[g][rs], st_ref[i, win, :].astype(BF16), nt, preferred_element_type=F32)
            yoff[rs, win] = jnp.where(inside, yo, 0.0 if g == 0 else yoff[rs, win])
        xstack = jnp.concatenate([jnp.where(gmask == g, xw[rs, :], 0.0) for g in range(N_GROUPS)], axis=0)
        bstack = jnp.concatenate([bmf[g][rs] for g in range(N_GROUPS)], axis=0)
        upd = lax.dot_general(xstack.astype(BF16), bstack.astype(BF16), tn, preferred_element_type=F32)
        dcol = lax.dot_general(dstack[rs, :].astype(BF16), ones, tn, preferred_element_type=F32)
        sto_ref[i] = st_ref[i] * dcol + upd

    ydiag = _ssd_ydiag(xs, bm, cm, acs, dt, mask)
    y_ref[...] = _ssd_gate_norm(ydiag + yoff[...] * e_in + xs * dsk_ref[...], _silu(z_ref[...]),
                                gred_ref[...], gexp_ref[...], gout_ref[...])

    @pl.when(step == n_steps - 1)
    def _drain_tail():
        for s in range(RING - 1):
            state_copy(0, (n_steps + s) % RING).wait()


def _ssd_sample(z, xbc, dt, conv_state, ssm_state, conv_w, conv_b, dt_bias, a_log, dskip_x, g_out):
    db = conv_state.shape[0]
    assert db % SEQ_PER_STEP == 0 and z.shape[0] == db * DEC_SEQ
    lead = SEQ_ROWS - DEC_SEQ

    def padded(a):
        a = a.reshape(db, DEC_SEQ, a.shape[-1])
        return jnp.pad(a, ((0, 0), (lead, 0), (0, 0))).reshape(db * SEQ_ROWS, a.shape[-1])

    xp = jnp.concatenate([jnp.zeros((db, lead - (CONV_W - 1), CONV_DIM), F32), conv_state,
                          xbc.reshape(db, DEC_SEQ, CONV_DIM)], axis=1).reshape(db * SEQ_ROWS, CONV_DIM)
    hexp, gexp, gred = _ssd_consts()
    rows = SEQ_PER_STEP * SEQ_ROWS
    row = lambda n: pl.BlockSpec((rows, n), lambda i: (i, 0))
    st_spec = pl.BlockSpec((SEQ_PER_STEP, SSD_W, D_STATE), lambda i: (i, 0, 0))
    consts = [conv_w, conv_b, dt_bias, a_log, dskip_x, g_out, hexp, gexp, gred]
    y, st = pl.pallas_call(
        _ssd_sample_kernel,
        grid=(db // SEQ_PER_STEP,),
        in_specs=[row(SSD_W), row(CONV_DIM), row(LANES), pl.BlockSpec(memory_space=pl.ANY)]
                 + [_const_spec(a.shape) for a in consts],
        out_specs=[row(SSD_W), st_spec],
        out_shape=[jax.ShapeDtypeStruct((db * SEQ_ROWS, SSD_W), F32),
                   jax.ShapeDtypeStruct((db, SSD_W, D_STATE), F32)],
        scratch_shapes=[pltpu.VMEM((CONV_DIM // LANES, CARRY_ROW + rows, LANES), F32),
                        pltpu.VMEM((rows, SSD_W), F32),
                        pltpu.VMEM((RING, SEQ_PER_STEP, SSD_W, D_STATE), F32), pltpu.SemaphoreType.DMA((RING,))],
        compiler_params=pltpu.CompilerParams(dimension_semantics=("arbitrary",),
                                             vmem_limit_bytes=VMEM_LIMIT),
        name="ssd_sample",
    )(padded(z), xp, padded(dt), ssm_state, *consts)
    y = y.reshape(db, SEQ_ROWS, SSD_W)[:, lead:].reshape(db * DEC_SEQ, SSD_W)
    return y, st


def kernel(x_prompt, x_sample, cache_k, cache_v, state_conv, state_ssm, rel_bias, g_mix, w_in, conv_w, conv_b,
           dt_bias, a_log, d_skip, g_attn_out, g_ssd_out, w_out, g_ffn, w_gate, w_up, w_down, g_final):
    depth = w_in.shape[0]
    bp, sp, _ = x_prompt.shape
    db, ds, _ = x_sample.shape
    assert bp == 1 and ds == DEC_SEQ and sp >= WIN_MAX
    xp = x_prompt.reshape(bp * sp, D_MODEL)
    xs = x_sample.reshape(db * ds, D_MODEL)
    pad_heads = lambda v: jnp.pad(v, (0, LANES - SSD_HEADS))[None]
    outs = [[] for _ in range(8)]
    for l in range(depth):
        w_t = w_in[l].T.astype(BF16)
        wdt = jnp.pad(w_t[IN_SPLITS[-1]:], ((0, LANES - SSD_HEADS), (0, 0)))
        ssd_prm = (conv_w[l], conv_b[l][None], pad_heads(dt_bias[l]), pad_heads(a_log[l]),
                   jnp.repeat(d_skip[l], SSD_HEAD_DIM)[None], g_ssd_out[l][None])
        tail_prm = (g_attn_out[l][None], w_out[l][:ATT_W].astype(BF16), w_out[l][ATT_W:].astype(BF16),
                    g_ffn[l][None], w_gate[l].astype(BF16), w_up[l].astype(BF16), w_down[l].astype(BF16),
                    g_final[None])
        last = l == depth - 1

        qkv_p, z_p, xbc_p, dt_p = _inproj(xp, g_mix[l][None], w_t, wdt)
        qkv_s, z_s, xbc_s, dt_s = _inproj(xs, g_mix[l][None], w_t, wdt)
        att_p = _attn_prompt(qkv_p, rel_bias)
        feature_major = lambda c: jnp.transpose(c, (0, 2, 3, 1)).reshape(db, ATT_W, c.shape[1])
        ssd_args = (z_p, xbc_p, dt_p, *ssd_prm)
        attn_args = (qkv_s, feature_major(cache_k[l]), feature_major(cache_v[l]), rel_bias)
        if sp // SSD_CHUNK == db:
            ssd_p, st_p, att_s = _ssd_prompt_attn_sample(ssd_args, attn_args)
        else:
            ssd_p, st_p = _ssd_prompt(*ssd_args)
            att_s = _attn_sample(*attn_args)
        xp = _tail(xp, att_p, ssd_p, *tail_prm, final_norm=last)
        ssd_s, st_s = _ssd_sample(z_s, xbc_s, dt_s, state_conv[l], state_ssm[l].reshape(db, SSD_W, D_STATE),
                                  *ssd_prm)
        xs = _tail(xs, att_s, ssd_s, *tail_prm, final_norm=last)

        heads = lambda a: a.reshape(a.shape[0], a.shape[1], ATT_HEADS, HEAD_DIM)
        kv_p = qkv_p.reshape(bp, sp, 3 * ATT_W)[:, sp - WIN_MAX:]
        kv_s = qkv_s.reshape(db, ds, 3 * ATT_W)
        outs[0].append(heads(kv_p[..., ATT_W:2 * ATT_W]))
        outs[1].append(heads(kv_p[..., 2 * ATT_W:]))
        outs[2].append(xbc_p.reshape(bp, sp, CONV_DIM)[:, sp - (CONV_W - 1):])
        outs[3].append(st_p.reshape(bp, SSD_HEADS, SSD_HEAD_DIM, D_STATE))
        outs[4].append(heads(kv_s[..., ATT_W:2 * ATT_W]))
        outs[5].append(heads(kv_s[..., 2 * ATT_W:]))
        outs[6].append(xbc_s.reshape(db, ds, CONV_DIM)[:, ds - (CONV_W - 1):])
        outs[7].append(st_s.reshape(db, SSD_HEADS, SSD_HEAD_DIM, D_STATE))
    return (xp.reshape(bp, sp, D_MODEL), xs.reshape(db, ds, D_MODEL)) + tuple(jnp.stack(o) for o in outs)
```

```python
import functools
import math

import jax
import jax.numpy as jnp
import numpy as np
from jax import lax
from jax.experimental import pallas as pl
from jax.experimental.pallas import tpu as pltpu

F32 = jnp.float32
BF16 = jnp.bfloat16

D_MODEL = 1024
ATT_HEADS = 12
HEAD_DIM = 64
ATT_W = ATT_HEADS * HEAD_DIM
DILATIONS = (1, 4, 16)
QBLK = 128
WIN_MAX = 2048
N_BUCKETS = 32
MAX_EXACT = N_BUCKETS // 2
SSD_HEADS = 20
SSD_HEAD_DIM = 64
SSD_W = SSD_HEADS * SSD_HEAD_DIM
N_GROUPS = 4
D_STATE = 128
CONV_W = 4
CONV_DIM = SSD_W + 2 * N_GROUPS * D_STATE
SSD_CHUNK = 128
EPS = 1e-6

LANES = 128
VMEM_LIMIT = 48 * 1024 * 1024
TAIL_VMEM_LIMIT = 58 * 1024 * 1024


def _rms(x, g):
    return x * lax.rsqrt(jnp.mean(x * x, axis=-1, keepdims=True) + EPS) * g


def _const_spec(shape):
    nd = len(shape)
    return pl.BlockSpec(shape, lambda *_: (0,) * nd, pipeline_mode=pl.Buffered(1))


IN_SPLITS = (0, 3 * ATT_W, 3 * ATT_W + SSD_W, 3 * ATT_W + SSD_W + CONV_DIM)


def _inproj_kernel(x_ref, g_ref, w_ref, wdt_ref, qkv_ref, z_ref, xbc_ref, dt_ref):
    h = _rms(x_ref[...], g_ref[...]).astype(BF16)
    nt = (((1,), (1,)), ((), ()))
    for out, lo, hi in zip((qkv_ref, z_ref, xbc_ref), IN_SPLITS[:-1], IN_SPLITS[1:]):
        out[...] = lax.dot_general(h, w_ref[lo:hi, :], nt, preferred_element_type=F32)
    dt_ref[...] = lax.dot_general(h, wdt_ref[...], nt, preferred_element_type=F32)


def _inproj(x, g_mix, w_t, wdt):
    t = x.shape[0]
    tm = 512 if t % 512 == 0 and t > 512 else 256
    assert t % tm == 0
    row = lambda n: pl.BlockSpec((tm, n), lambda i: (i, 0))
    return pl.pallas_call(
        _inproj_kernel,
        grid=(t // tm,),
        in_specs=[row(D_MODEL), _const_spec((1, D_MODEL)), _const_spec(w_t.shape), _const_spec(wdt.shape)],
        out_specs=[row(3 * ATT_W), row(SSD_W), row(CONV_DIM), row(LANES)],
        out_shape=[jax.ShapeDtypeStruct((t, 3 * ATT_W), F32), jax.ShapeDtypeStruct((t, SSD_W), F32),
                   jax.ShapeDtypeStruct((t, CONV_DIM), F32), jax.ShapeDtypeStruct((t, LANES), F32)],
        compiler_params=pltpu.CompilerParams(dimension_semantics=("parallel",),
                                             vmem_limit_bytes=VMEM_LIMIT),
        name="inproj",
    )(x, g_mix, w_t, wdt)


def _tail_kernel(x_ref, att_ref, ssd_ref, gatt_ref, woa_ref, wos_ref, gffn_ref,
                 wg_ref, wu_ref, wd_ref, gfin_ref, y_ref, *, final_norm):
    an = _rms(att_ref[...], gatt_ref[...]).astype(BF16)
    mix = jnp.dot(an, woa_ref[...], preferred_element_type=F32)
    mix = mix + jnp.dot(ssd_ref[...].astype(BF16), wos_ref[...], preferred_element_type=F32)
    x1 = x_ref[...] + mix
    h2 = _rms(x1, gffn_ref[...]).astype(BF16)
    gate = jnp.dot(h2, wg_ref[...], preferred_element_type=F32)
    up = jnp.dot(h2, wu_ref[...], preferred_element_type=F32)
    act = (gate * jax.nn.sigmoid(gate) * up).astype(BF16)
    x2 = x1 + jnp.dot(act, wd_ref[...], preferred_element_type=F32)
    y_ref[...] = _rms(x2, gfin_ref[...]) if final_norm else x2


def _tail(x, att, ssd, g_att, woa, wos, g_ffn, wg, wu, wd, g_fin, final_norm):
    t = x.shape[0]
    tm = 512 if t % 512 == 0 and t > 512 else 256
    assert t % tm == 0
    row = lambda n: pl.BlockSpec((tm, n), lambda i: (i, 0))
    consts = [g_att, woa, wos, g_ffn, wg, wu, wd, g_fin]
    return pl.pallas_call(
        functools.partial(_tail_kernel, final_norm=final_norm),
        grid=(t // tm,),
        in_specs=[row(D_MODEL), row(ATT_W), row(SSD_W)] + [_const_spec(c.shape) for c in consts],
        out_specs=row(D_MODEL),
        out_shape=jax.ShapeDtypeStruct((t, D_MODEL), F32),
        compiler_params=pltpu.CompilerParams(dimension_semantics=("parallel",),
                                             vmem_limit_bytes=TAIL_VMEM_LIMIT),
        name="tail",
    )(x, att, ssd, *consts)


TQ = QBLK * max(DILATIONS)
NEG_INF = float("-inf")
LOG2E = math.log2(math.e)


def _rel_bucket(dist):
    n = np.maximum(dist, 0)
    nf = np.maximum(n, 1).astype(np.float32)
    large = MAX_EXACT + (np.log(nf / MAX_EXACT) / math.log(WIN_MAX / MAX_EXACT)
                         * (N_BUCKETS - MAX_EXACT)).astype(np.int32)
    return np.where(n < MAX_EXACT, n, np.minimum(large, N_BUCKETS - 1))


BAND_ROWS = 8


def _band_buckets():
    dist = (QBLK - np.arange(2 * QBLK)) % (2 * QBLK)
    valid = dist <= QBLK
    row0 = np.stack([np.where(valid, _rel_bucket(dist * d), -1) for d in DILATIONS]).astype(np.int32)
    return jnp.asarray(np.broadcast_to(row0[:, None, :], (len(DILATIONS), BAND_ROWS, 2 * QBLK)).copy())


STAGE = 4
assert DILATIONS == (1, STAGE, STAGE * STAGE)


def _regroup(src_ref, tmp, dsts, scale=None):
    cast = (lambda x: x.astype(BF16)) if scale is None else (lambda x: (x * scale).astype(BF16))
    sub = TQ // STAGE
    for c in range(STAGE):
        dsts[0][c * sub:(c + 1) * sub, :] = cast(src_ref[c * sub:(c + 1) * sub, :])
    for r in range(STAGE):
        x = src_ref[pl.ds(r, sub, stride=STAGE), :]
        tmp[r] = x
        dsts[1][r * sub:(r + 1) * sub, :] = cast(x)
    for r in range(STAGE):
        for r2 in range(STAGE):
            res = r + STAGE * r2
            dsts[2][res * QBLK:(res + 1) * QBLK, :] = cast(tmp[r, pl.ds(r2, QBLK, stride=STAGE), :])


def _attn_prompt_kernel(tbl_ref, q_ref, k_ref, v_ref, bkt_ref, o_ref,
                        qd, kd, vd, tmp, bias, macc, nacc, dacc):
    hp = pl.program_id(0)
    t = pl.program_id(1)
    nb = len(DILATIONS)
    slot = t % 2

    @pl.when(t == 0)
    def _no_previous_tile():
        kd[:, 1] = jnp.zeros((nb, TQ, LANES), BF16)
        vd[:, 1] = jnp.zeros((nb, TQ, LANES), BF16)

    @pl.when(t == 0)
    def _build_bias():
        prev_half = lax.broadcasted_iota(jnp.int32, (1, 2 * QBLK), 1) < QBLK
        for b in range(len(DILATIONS)):
            bk = bkt_ref[b]
            for hh in range(2):
                row = jnp.full((BAND_ROWS, 2 * QBLK), NEG_INF, F32)
                for u in range(N_BUCKETS):
                    row = jnp.where(bk == u, tbl_ref[u, 2 * hp + hh], row)
                row = row * LOG2E
                acc = pltpu.roll(jnp.concatenate([row] * (QBLK // BAND_ROWS), axis=0), 0, axis=1,
                                 stride=1, stride_axis=0)
                bias[b, 0, hh * QBLK:(hh + 1) * QBLK] = acc
                bias[b, 1, hh * QBLK:(hh + 1) * QBLK] = jnp.where(prev_half, NEG_INF, acc)

    _regroup(q_ref, tmp, [qd.at[b] for b in range(nb)], HEAD_DIM ** -0.5 * LOG2E)
    _regroup(k_ref, tmp, [kd.at[b, slot] for b in range(nb)])
    _regroup(v_ref, tmp, [vd.at[b, slot] for b in range(nb)])

    left = lax.broadcasted_iota(jnp.int32, (1, LANES), 1) < HEAD_DIM
    nt = (((1,), (1,)), ((), ()))

    for b, d in enumerate(DILATIONS):
        nbr = max(DILATIONS) // d

        def rows(start, size, d=d):
            return pl.ds(start, size) if d == 1 else pl.ds(start, size, stride=d)

        def block(j, carry, b=b, d=d, nbr=nbr, rows=rows):
            r = j // nbr
            m = j % nbr
            qs = r + d * QBLK * m
            gs = pl.multiple_of((r * nbr + m) * QBLK, QBLK)
            wrap = m == 0
            ps = pl.multiple_of((r * nbr + jnp.where(wrap, nbr - 1, m - 1)) * QBLK, QBLK)
            pslot = jnp.where(wrap, 1 - slot, slot)
            q = qd[b, pl.ds(gs, QBLK), :]
            kc = jnp.concatenate([kd[b, pslot, pl.ds(ps, QBLK), :], kd[b, slot, pl.ds(gs, QBLK), :]], axis=0)
            vc = jnp.concatenate([vd[b, pslot, pl.ds(ps, QBLK), :], vd[b, slot, pl.ds(gs, QBLK), :]], axis=0)
            first = jnp.logical_and(wrap, t == 0).astype(jnp.int32)
            zero = jnp.zeros_like(q)
            qq = jnp.concatenate([jnp.where(left, q, zero), jnp.where(left, zero, q)], axis=0)
            s = lax.dot_general(qq, kc, nt, preferred_element_type=F32) + bias[b, first]
            mx = jnp.max(s, axis=-1, keepdims=True)
            e = jnp.exp2(s - mx)
            den = jnp.sum(e, axis=-1, keepdims=True)
            res = jnp.dot(e.astype(BF16), vc, preferred_element_type=F32)
            nacc[b, rows(qs, QBLK), :] = jnp.where(left, res[:QBLK], res[QBLK:])
            dacc[b, rows(qs, QBLK), :] = jnp.where(left, den[:QBLK], den[QBLK:])
            macc[b, rows(qs, QBLK), :] = jnp.where(left, mx[:QBLK], mx[QBLK:])
            return carry

        lax.fori_loop(0, max(DILATIONS), block, 0, unroll=True)

    def merge(c, carry):
        sl = pl.ds(pl.multiple_of(c * QBLK, QBLK), QBLK)
        ms = [macc[b, sl, :] for b in range(len(DILATIONS))]
        top = jnp.maximum(jnp.maximum(ms[0], ms[1]), ms[2])
        ws = [jnp.exp2(mb - top) for mb in ms]
        num = sum(w * nacc[b, sl, :] for b, w in enumerate(ws))
        den = sum(w * dacc[b, sl, :] for b, w in enumerate(ws))
        o_ref[sl, :] = num / den
        return carry

    lax.fori_loop(0, TQ // QBLK, merge, 0, unroll=4)


def _attn_prompt(qkv, rel_bias):
    s = qkv.shape[0]
    assert s % TQ == 0
    nb = len(DILATIONS)
    hp_n = ATT_W // LANES
    tile = lambda f: pl.BlockSpec((TQ, LANES), f)
    return pl.pallas_call(
        _attn_prompt_kernel,
        grid=(hp_n, s // TQ),
        in_specs=[pl.BlockSpec(memory_space=pltpu.SMEM),
                  tile(lambda h, t: (t, h)),
                  tile(lambda h, t: (t, hp_n + h)),
                  tile(lambda h, t: (t, 2 * hp_n + h)),
                  _const_spec((nb, BAND_ROWS, 2 * QBLK))],
        out_specs=tile(lambda h, t: (t, h)),
        out_shape=jax.ShapeDtypeStruct((s, ATT_W), F32),
        scratch_shapes=[pltpu.VMEM((nb, TQ, LANES), BF16), pltpu.VMEM((nb, 2, TQ, LANES), BF16),
                        pltpu.VMEM((nb, 2, TQ, LANES), BF16), pltpu.VMEM((STAGE, TQ // STAGE, LANES), F32),
                        pltpu.VMEM((nb, 2, 2 * QBLK, 2 * QBLK), F32),
                        pltpu.VMEM((nb, TQ, LANES), F32), pltpu.VMEM((nb, TQ, LANES), F32),
                        pltpu.VMEM((nb, TQ, LANES), F32)],
        compiler_params=pltpu.CompilerParams(dimension_semantics=("parallel", "arbitrary"),
                                             vmem_limit_bytes=VMEM_LIMIT),
        name="attn_prompt",
    )(rel_bias, qkv, qkv, qkv, _band_buckets())


HPG = SSD_HEADS // N_GROUPS
GROUP_W = SSD_W // N_GROUPS
GROUP_WIN = 3 * LANES
GROUP_LO = tuple((g * GROUP_W) // LANES * LANES for g in range(N_GROUPS))
CARRY_ROW = 8


def _split(x, terms):
    parts = []
    for _ in range(terms):
        p = x.astype(BF16)
        parts.append(p)
        x = x - p.astype(F32)
    return parts


def _dot_exact_rhs(xs_terms, m):
    pieces = [p for x, terms in xs_terms for p in _split(x, terms)]
    out = jnp.dot(jnp.concatenate(pieces, axis=0) if len(pieces) > 1 else pieces[0], m,
                  preferred_element_type=F32)
    res, r = [], 0
    for x, terms in xs_terms:
        n = x.shape[0]
        res.append(sum(out[r + i * n:r + (i + 1) * n] for i in range(terms)))
        r += terms * n
    return res


def _dot_exact_lhs(m, x, terms=3):
    n = x.shape[1]
    out = jnp.dot(m, jnp.concatenate(_split(x, terms), axis=1), preferred_element_type=F32)
    return sum(out[:, i * n:(i + 1) * n] for i in range(terms))


def _softplus(x):
    return jnp.maximum(x, 0.0) + jnp.log1p(jnp.exp(-jnp.abs(x)))


def _silu(x):
    h = 0.5 * x
    return h + h * jnp.tanh(h)


def _ssd_consts():
    c = jnp.arange(SSD_W)
    head_of = (c // SSD_HEAD_DIM)[None, :] == jnp.arange(LANES)[:, None]
    group_of = (c // GROUP_W)[None, :] == jnp.arange(LANES)[:, None]
    return head_of.astype(BF16), group_of.astype(BF16), group_of.T.astype(BF16)


def _conv_silu(cbuf, xbc_ref, w_ref, b_ref, rows):
    out = []
    for c in range(CONV_DIM // LANES):
        cols = slice(c * LANES, (c + 1) * LANES)
        cbuf[c, CARRY_ROW:CARRY_ROW + rows, :] = xbc_ref[:, cols]
        acc = b_ref[:, cols]
        for j in range(CONV_W):
            acc = acc + cbuf[c, pl.ds(CARRY_ROW - (CONV_W - 1) + j, rows), :] * w_ref[j:j + 1, cols]
        out.append(_silu(acc))
    return out


def _drain(steps):
    try:
        while True:
            next(steps)
    except StopIteration as done:
        return done.value


def _interleave(*bodies):
    live = list(bodies)
    while live:
        for body in list(live):
            try:
                next(body)
            except StopIteration:
                live.remove(body)


def _ssd_ydiag(*args):
    return _drain(_ssd_ydiag_steps(*args))


def _ssd_ydiag_steps(xs, bm, cm, acs, dt, mask):
    cbs = [lax.dot_general(cm[g], bm[g], (((1,), (1,)), ((), ())), preferred_element_type=F32)
           for g in range(N_GROUPS)]
    acs_t = acs.T
    dt_t = dt.T
    left = lax.broadcasted_iota(jnp.int32, (1, LANES), 1) < SSD_HEAD_DIM
    pairs = []
    for pair in range(SSD_HEADS // 2):
        xpair = xs[:, pair * LANES:(pair + 1) * LANES]
        acc = None
        for hh in range(2):
            h = 2 * pair + hh
            seg = acs[:, h:h + 1] - acs_t[h:h + 1, :]
            dec = jnp.exp2(jnp.where(mask, seg, NEG_INF))
            w = (cbs[h // HPG] * dec * dt_t[h:h + 1, :]).astype(BF16)
            xh = jnp.where(left if hh == 0 else jnp.logical_not(left), xpair, 0.0).astype(BF16)
            part = jnp.dot(w, xh, preferred_element_type=F32)
            acc = part if acc is None else acc + part
        pairs.append(acc)
        if pair % 2 == 1:
            yield
    return jnp.concatenate(pairs, axis=1)


def _ssd_gate_norm(y, gate, gred, gexp, gout):
    y = y * gate
    ms = _dot_exact_rhs([(y * y, 2)], gred)[0] * (1.0 / GROUP_W)
    return y * _dot_exact_rhs([(lax.rsqrt(ms + EPS), 2)], gexp)[0] * gout


def _ssd_prompt_init(cbuf, state):
    @pl.when(pl.program_id(0) == 0)
    def _init():
        state[...] = jnp.zeros_like(state)
        cbuf[:, 0:CARRY_ROW, :] = jnp.zeros((CONV_DIM // LANES, CARRY_ROW, LANES), F32)


def _ssd_prompt_emit(st_ref, state):
    @pl.when(pl.program_id(0) == pl.num_programs(0) - 1)
    def _emit_state():
        for i in range(SSD_W // LANES):
            st_ref[i * LANES:(i + 1) * LANES, :] = state[:, i * LANES:(i + 1) * LANES].T


def _ssd_prompt_chunk(z_ref, xbc_ref, dt_ref, cw_ref, cb_ref, dtb_ref, alog_ref, dsk_ref, gout_ref,
                      hexp_ref, gexp_ref, gred_ref, y_ref, cbuf, state, yoff):
    cl = SSD_CHUNK
    xc = _conv_silu(cbuf, xbc_ref, cw_ref, cb_ref, cl)
    cbuf[:, CARRY_ROW - (CONV_W - 1):CARRY_ROW, :] = cbuf[:, CARRY_ROW + cl - (CONV_W - 1):CARRY_ROW + cl, :]
    nx = SSD_W // LANES
    xs = jnp.concatenate(xc[:nx], axis=1)
    bm = [b.astype(BF16) for b in xc[nx:nx + N_GROUPS]]
    cm = [b.astype(BF16) for b in xc[nx + N_GROUPS:]]
    yield

    li = lax.broadcasted_iota(jnp.int32, (cl, cl), 0)
    si = lax.broadcasted_iota(jnp.int32, (cl, cl), 1)
    causal = li >= si

    dt = _softplus(dt_ref[...] + dtb_ref[...])
    acs = _dot_exact_lhs(causal.astype(BF16), dt * (-jnp.exp(alog_ref[...]) * LOG2E))
    hexp = hexp_ref[...]
    e_in, e_end = _dot_exact_rhs([(jnp.exp2(acs), 3), (jnp.exp2(acs[cl - 1:cl, :] - acs) * dt, 1)], hexp)
    xw = (xs * e_end).astype(BF16)
    chunk_decay = e_in[cl - 1:cl]
    yield

    gmask = lax.broadcasted_iota(jnp.int32, (1, SSD_W), 1) // GROUP_W
    for g in range(N_GROUPS):
        lo = GROUP_LO[g]
        win = slice(lo, lo + GROUP_WIN)
        inside = gmask[:, win] == g
        yo = jnp.dot(cm[g], state[:, win].astype(BF16), preferred_element_type=F32)
        yoff[:, win] = jnp.where(inside, yo, 0.0 if g == 0 else yoff[:, win])
    yield
    for g in range(N_GROUPS):
        lo = GROUP_LO[g]
        win = slice(lo, lo + GROUP_WIN)
        inside = gmask[:, win] == g
        upd = lax.dot_general(bm[g], xw[:, win], (((0,), (0,)), ((), ())), preferred_element_type=F32)
        old = state[:, win]
        state[:, win] = jnp.where(inside, old * chunk_decay[:, win] + upd, old)
    yield

    ydiag = yield from _ssd_ydiag_steps(xs, bm, cm, acs, dt, causal)
    y_ref[...] = _ssd_gate_norm(ydiag + yoff[...] * e_in + xs * dsk_ref[...], _silu(z_ref[...]),
                                gred_ref[...], gexp_ref[...], gout_ref[...])


N_SSD_IN = 12


def _ssd_prompt_kernel(*refs):
    ins, (y_ref, st_ref, cbuf, state, yoff) = refs[:N_SSD_IN], refs[N_SSD_IN:]
    _ssd_prompt_init(cbuf, state)
    _drain(_ssd_prompt_chunk(*ins, y_ref, cbuf, state, yoff))
    _ssd_prompt_emit(st_ref, state)


def _ssd_prompt_call(z, xbc, dt, conv_w, conv_b, dt_bias, a_log, dskip_x, g_out):
    s = z.shape[0]
    assert s % SSD_CHUNK == 0
    hexp, gexp, gred = _ssd_consts()
    row = lambda n: pl.BlockSpec((SSD_CHUNK, n), lambda i: (i, 0))
    consts = [conv_w, conv_b, dt_bias, a_log, dskip_x, g_out, hexp, gexp, gred]
    return dict(
        steps=s // SSD_CHUNK,
        operands=[z, xbc, dt] + consts,
        in_specs=[row(SSD_W), row(CONV_DIM), row(LANES)] + [_const_spec(a.shape) for a in consts],
        out_specs=[row(SSD_W), pl.BlockSpec((SSD_W, D_STATE), lambda i: (0, 0))],
        out_shape=[jax.ShapeDtypeStruct((s, SSD_W), F32), jax.ShapeDtypeStruct((SSD_W, D_STATE), F32)],
        scratch=[pltpu.VMEM((CONV_DIM // LANES, CARRY_ROW + SSD_CHUNK, LANES), F32),
                 pltpu.VMEM((D_STATE, SSD_W), F32), pltpu.VMEM((SSD_CHUNK, SSD_W), F32)])


def _run(kernel_fn, name, *calls, vmem_limit=VMEM_LIMIT):
    steps = calls[0]["steps"]
    assert all(c["steps"] == steps for c in calls)
    cat = lambda key: [x for c in calls for x in c[key]]
    return pl.pallas_call(
        kernel_fn, grid=(steps,), in_specs=cat("in_specs"), out_specs=cat("out_specs"),
        out_shape=cat("out_shape"), scratch_shapes=cat("scratch"),
        compiler_params=pltpu.CompilerParams(dimension_semantics=("arbitrary",), vmem_limit_bytes=vmem_limit),
        name=name)(*cat("operands"))


def _ssd_prompt(*args):
    return _run(_ssd_prompt_kernel, "ssd_prompt", _ssd_prompt_call(*args))


DEC_SEQ = 4
NEW_ROWS = 8


def _branch_count(dist):
    return sum(((dist % d == 0) & (dist <= QBLK * d)).astype(np.int32) for d in DILATIONS)


def _sample_bias_tables(wb):
    t = np.arange(NEW_ROWS)[:, None]
    dist = wb + t - np.arange(wb)[None, :]
    cnt = np.where(t < DEC_SEQ, _branch_count(dist), 0)
    bkt = np.where(cnt > 0, _rel_bucket(dist), -1).astype(np.int32)
    ladd = np.log(np.maximum(cnt, 1).astype(np.float32))
    u = np.arange(LANES)[None, :]
    dnew = t - u
    cnew = np.where((t < DEC_SEQ) & (dnew > 0), _branch_count(dnew),
                    np.where((t < DEC_SEQ) & (dnew == 0), len(DILATIONS), 0))
    bkt_new = np.where(cnew > 0, _rel_bucket(dnew), -1).astype(np.int32)
    ladd_new = np.log(np.maximum(cnew, 1).astype(np.float32))
    return tuple(jnp.asarray(a) for a in (bkt, ladd, bkt_new, ladd_new))


def _attn_sample_init(tbl_ref, bkt_ref, ladd_ref, bktn_ref, laddn_ref, bias, bias_new):
    @pl.when(pl.program_id(0) == 0)
    def _build_bias():
        for h in range(ATT_HEADS):
            for src, add, dst in ((bkt_ref, ladd_ref, bias), (bktn_ref, laddn_ref, bias_new)):
                bk = src[...]
                acc = jnp.full(bk.shape, NEG_INF, F32)
                for u in range(N_BUCKETS):
                    acc = jnp.where(bk == u, tbl_ref[u, h], acc)
                dst[h // 2, (h % 2) * NEW_ROWS:(h % 2 + 1) * NEW_ROWS, :] = (acc + add[...]) * LOG2E


N_ATTN_S_IN = 8


def _attn_sample_kernel(tbl_ref, q_ref, kt_ref, vt_ref, bkt_ref, ladd_ref, bktn_ref, laddn_ref,
                        o_ref, bias, bias_new):
    _attn_sample_init(tbl_ref, bkt_ref, ladd_ref, bktn_ref, laddn_ref, bias, bias_new)
    whole = lambda hp: (kt_ref.at[0], vt_ref.at[0], hp * LANES)
    _drain(_attn_sample_seq(q_ref, whole, o_ref, bias, bias_new))


RING = 3
RING_VMEM_LIMIT = 58 * 1024 * 1024


def _ring_kv(k_hbm, v_hbm, kring, vring, sems):
    i = pl.program_id(0)
    n = pl.num_programs(0)

    def copies(seq, slot):
        return (pltpu.make_async_copy(k_hbm.at[seq], kring.at[slot], sems.at[0, slot]),
                pltpu.make_async_copy(v_hbm.at[seq], vring.at[slot], sems.at[1, slot]))

    def start(seq, slot):
        for c in copies(jnp.minimum(seq, n - 1), slot):
            c.start()

    @pl.when(i == 0)
    def _prime():
        for s in range(RING - 1):
            start(s, s)

    start(i + RING - 1, (i + RING - 1) % RING)
    slot = i % RING
    for c in copies(i, slot):
        c.wait()

    def finish():
        @pl.when(i == n - 1)
        def _drain_tail():
            for s in range(RING - 1):
                for c in copies(0, (n + s) % RING):
                    c.wait()

    return (lambda hp: (kring.at[slot], vring.at[slot], hp * LANES)), finish


def _attn_sample_seq(q_ref, kv_for_pair, o_ref, bias, bias_new):
    left = lax.broadcasted_iota(jnp.int32, (1, LANES), 1) < HEAD_DIM
    nt = (((1,), (1,)), ((), ()))
    scale = HEAD_DIM ** -0.5 * LOG2E
    outs = []
    for hp in range(ATT_W // LANES):
        cols = slice(hp * LANES, (hp + 1) * LANES)
        k_ref, v_ref, row0 = kv_for_pair(hp)
        kv_rows = slice(row0, row0 + LANES)
        q = q_ref[0, :, cols] * scale
        fill = jnp.zeros((LANES - NEW_ROWS, LANES), F32)
        knew = jnp.concatenate([q_ref[0, :, ATT_W + hp * LANES:ATT_W + (hp + 1) * LANES], fill], axis=0)
        vnew = jnp.concatenate([q_ref[0, :, 2 * ATT_W + hp * LANES:2 * ATT_W + (hp + 1) * LANES], fill], axis=0)
        qq = jnp.concatenate([jnp.where(left, q, 0.0), jnp.where(left, 0.0, q)], axis=0).astype(BF16)
        s = jnp.dot(qq, k_ref[kv_rows, :].astype(BF16), preferred_element_type=F32) + bias[hp]
        sn = lax.dot_general(qq, knew.astype(BF16), nt, preferred_element_type=F32) + bias_new[hp]
        m = jnp.maximum(jnp.max(s, axis=-1, keepdims=True), jnp.max(sn, axis=-1, keepdims=True))
        m = jnp.where(m == NEG_INF, 0.0, m)
        p = jnp.exp2(s - m)
        pn = jnp.exp2(sn - m)
        den = jnp.sum(p, axis=-1, keepdims=True) + jnp.sum(pn, axis=-1, keepdims=True)
        r = lax.dot_general(p.astype(BF16), v_ref[kv_rows, :].astype(BF16), nt, preferred_element_type=F32)
        r = r + jnp.dot(pn.astype(BF16), vnew.astype(BF16), preferred_element_type=F32)
        o = r / jnp.where(den == 0.0, 1.0, den)
        outs.append(jnp.where(left, o[:NEW_ROWS], o[NEW_ROWS:]))
        yield
    o_ref[0] = jnp.concatenate(outs, axis=1)[:DEC_SEQ]


def _attn_sample_call(qkv, cache_kt, cache_vt, rel_bias, ring=False):
    db, _, wb = cache_kt.shape
    assert wb >= WIN_MAX and qkv.shape[0] == db * DEC_SEQ
    q8 = jnp.pad(qkv.reshape(db, DEC_SEQ, 3 * ATT_W), ((0, 0), (0, NEW_ROWS - DEC_SEQ), (0, 0)))
    consts = list(_sample_bias_tables(wb))
    hp_n = ATT_W // LANES
    seq = lambda n, m: pl.BlockSpec((1, n, m), lambda b: (b, 0, 0))
    kv_spec = pl.BlockSpec(memory_space=pl.ANY) if ring else seq(ATT_W, wb)
    scratch = [pltpu.VMEM((hp_n, 2 * NEW_ROWS, wb), F32), pltpu.VMEM((hp_n, 2 * NEW_ROWS, LANES), F32)]
    if ring:
        scratch += [pltpu.VMEM((RING, ATT_W, wb), F32), pltpu.VMEM((RING, ATT_W, wb), F32),
                    pltpu.SemaphoreType.DMA((2, RING))]
    return dict(
        steps=db,
        operands=[rel_bias, q8, cache_kt, cache_vt] + consts,
        in_specs=[pl.BlockSpec(memory_space=pltpu.SMEM), seq(NEW_ROWS, 3 * ATT_W), kv_spec, kv_spec]
                 + [_const_spec(c.shape) for c in consts],
        out_specs=[seq(DEC_SEQ, ATT_W)],
        out_shape=[jax.ShapeDtypeStruct((db, DEC_SEQ, ATT_W), F32)],
        scratch=scratch)


def _attn_sample(qkv, cache_kt, cache_vt, rel_bias):
    out, = _run(_attn_sample_kernel, "attn_sample", _attn_sample_call(qkv, cache_kt, cache_vt, rel_bias))
    return out.reshape(-1, ATT_W)


def _ssd_prompt_attn_sample_kernel(*refs):
    a = N_SSD_IN + N_ATTN_S_IN
    ssd_in = refs[:N_SSD_IN]
    tbl_ref, q_ref, kt_ref, vt_ref, bkt_ref, ladd_ref, bktn_ref, laddn_ref = refs[N_SSD_IN:a]
    y_ref, st_ref, o_ref, cbuf, state, yoff, bias, bias_new, kring, vring, sems = refs[a:]
    _ssd_prompt_init(cbuf, state)
    _attn_sample_init(tbl_ref, bkt_ref, ladd_ref, bktn_ref, laddn_ref, bias, bias_new)
    kv_for_pair, finish_ring = _ring_kv(kt_ref, vt_ref, kring, vring, sems)
    _interleave(_ssd_prompt_chunk(*ssd_in, y_ref, cbuf, state, yoff),
                _attn_sample_seq(q_ref, kv_for_pair, o_ref, bias, bias_new))
    finish_ring()
    _ssd_prompt_emit(st_ref, state)


def _ssd_prompt_attn_sample(ssd_args, attn_args):
    y, st, att = _run(_ssd_prompt_attn_sample_kernel, "ssd_prompt_attn_sample",
                      _ssd_prompt_call(*ssd_args), _attn_sample_call(*attn_args, ring=True),
                      vmem_limit=RING_VMEM_LIMIT)
    return y, st, att.reshape(-1, ATT_W)


SEQ_ROWS = 8
SEQ_PER_STEP = 8


def _ssd_sample_kernel(z_ref, xp_ref, dt_ref, st_hbm, cw_ref, cb_ref, dtb_ref, alog_ref, dsk_ref, gout_ref,
                       hexp_ref, gexp_ref, gred_ref, y_ref, sto_ref, cbuf, yoff, st_ring, sems):
    step = pl.program_id(0)
    n_steps = pl.num_programs(0)

    def state_copy(s, slot):
        first = pl.multiple_of(jnp.minimum(s, n_steps - 1) * SEQ_PER_STEP, SEQ_PER_STEP)
        return pltpu.make_async_copy(st_hbm.at[pl.ds(first, SEQ_PER_STEP)], st_ring.at[slot], sems.at[slot])

    @pl.when(step == 0)
    def _prime():
        for s in range(RING - 1):
            state_copy(s, s).start()

    state_copy(step + RING - 1, (step + RING - 1) % RING).start()
    state_copy(step, step % RING).wait()
    st_ref = st_ring.at[step % RING]

    rows = SEQ_PER_STEP * SEQ_ROWS
    cbuf[:, 0:CARRY_ROW, :] = jnp.zeros((CONV_DIM // LANES, CARRY_ROW, LANES), F32)
    xc = _conv_silu(cbuf, xp_ref, cw_ref, cb_ref, rows)
    nx = SSD_W // LANES
    xs = jnp.concatenate(xc[:nx], axis=1)
    bmf = xc[nx:nx + N_GROUPS]
    bm = [b.astype(BF16) for b in bmf]
    cm = [b.astype(BF16) for b in xc[nx + N_GROUPS:]]

    li = lax.broadcasted_iota(jnp.int32, (rows, rows), 0)
    si = lax.broadcasted_iota(jnp.int32, (rows, rows), 1)
    same = (li // SEQ_ROWS) == (si // SEQ_ROWS)
    mask = same & (li >= si)
    rowi = lax.broadcasted_iota(jnp.int32, (rows, 1), 0)
    is_token = (rowi % SEQ_ROWS) >= SEQ_ROWS - DEC_SEQ

    dt = jnp.where(is_token, _softplus(dt_ref[...] + dtb_ref[...]), 0.0)
    da = dt * (-jnp.exp(alog_ref[...]) * LOG2E)
    acs = _dot_exact_lhs(mask.astype(BF16), da)
    acs_end = _dot_exact_lhs(same.astype(BF16), da)
    hexp = hexp_ref[...]
    e_in, e_end, e_all = _dot_exact_rhs(
        [(jnp.exp2(acs), 2), (jnp.exp2(acs_end - acs) * dt, 1), (jnp.exp2(acs_end), 3)], hexp)
    xw = xs * e_end

    sub = rowi % SEQ_ROWS
    pieces = [p.astype(F32) for p in _split(e_all, 3)]
    dstack = jnp.where(sub == 0, pieces[0], jnp.where(sub == 1, pieces[1], jnp.where(sub == 2, pieces[2], 0.0)))
    ones = jnp.ones((SEQ_ROWS, D_STATE), BF16)
    gmask = lax.broadcasted_iota(jnp.int32, (1, SSD_W), 1) // GROUP_W
    tn = (((0,), (0,)), ((), ()))
    nt = (((1,), (1,)), ((), ()))
    for i in range(SEQ_PER_STEP):
        rs = slice(i * SEQ_ROWS, (i + 1) * SEQ_ROWS)
        for g in range(N_GROUPS):
            lo = GROUP_LO[g]
            win = slice(lo, lo + GROUP_WIN)
            inside = gmask[:, win] == g
            yo = lax.dot_general(cm[g][rs], st_ref[i, win, :].astype(BF16), nt, preferred_element_type=F32)
            yoff[rs, win] = jnp.where(inside, yo, 0.0 if g == 0 else yoff[rs, win])
        xstack = jnp.concatenate([jnp.where(gmask == g, xw[rs, :], 0.0) for g in range(N_GROUPS)], axis=0)
        bstack = jnp.concatenate([bmf[g][rs] for g in range(N_GROUPS)], axis=0)
        upd = lax.dot_general(xstack.astype(BF16), bstack.astype(BF16), tn, preferred_element_type=F32)
        dcol = lax.dot_general(dstack[rs, :].astype(BF16), ones, tn, preferred_element_type=F32)
        sto_ref[i] = st_ref[i] * dcol + upd

    ydiag = _ssd_ydiag(xs, bm, cm, acs, dt, mask)
    y_ref[...] = _ssd_gate_norm(ydiag + yoff[...] * e_in + xs * dsk_ref[...], _silu(z_ref[...]),
                                gred_ref[...], gexp_ref[...], gout_ref[...])

    @pl.when(step == n_steps - 1)
    def _drain_tail():
        for s in range(RING - 1):
            state_copy(0, (n_steps + s) % RING).wait()


def _ssd_sample(z, xbc, dt, conv_state, ssm_state, conv_w, conv_b, dt_bias, a_log, dskip_x, g_out):
    db = conv_state.shape[0]
    assert db % SEQ_PER_STEP == 0 and z.shape[0] == db * DEC_SEQ
    lead = SEQ_ROWS - DEC_SEQ

    def padded(a):
        a = a.reshape(db, DEC_SEQ, a.shape[-1])
        return jnp.pad(a, ((0, 0), (lead, 0), (0, 0))).reshape(db * SEQ_ROWS, a.shape[-1])

    xp = jnp.concatenate([jnp.zeros((db, lead - (CONV_W - 1), CONV_DIM), F32), conv_state,
                          xbc.reshape(db, DEC_SEQ, CONV_DIM)], axis=1).reshape(db * SEQ_ROWS, CONV_DIM)
    hexp, gexp, gred = _ssd_consts()
    rows = SEQ_PER_STEP * SEQ_ROWS
    row = lambda n: pl.BlockSpec((rows, n), lambda i: (i, 0))
    st_spec = pl.BlockSpec((SEQ_PER_STEP, SSD_W, D_STATE), lambda i: (i, 0, 0))
    consts = [conv_w, conv_b, dt_bias, a_log, dskip_x, g_out, hexp, gexp, gred]
    y, st = pl.pallas_call(
        _ssd_sample_kernel,
        grid=(db // SEQ_PER_STEP,),
        in_specs=[row(SSD_W), row(CONV_DIM), row(LANES), pl.BlockSpec(memory_space=pl.ANY)]
                 + [_const_spec(a.shape) for a in consts],
        out_specs=[row(SSD_W), st_spec],
        out_shape=[jax.ShapeDtypeStruct((db * SEQ_ROWS, SSD_W), F32),
                   jax.ShapeDtypeStruct((db, SSD_W, D_STATE), F32)],
        scratch_shapes=[pltpu.VMEM((CONV_DIM // LANES, CARRY_ROW + rows, LANES), F32),
                        pltpu.VMEM((rows, SSD_W), F32),
                        pltpu.VMEM((RING, SEQ_PER_STEP, SSD_W, D_STATE), F32), pltpu.SemaphoreType.DMA((RING,))],
        compiler_params=pltpu.CompilerParams(dimension_semantics=("arbitrary",),
                                             vmem_limit_bytes=VMEM_LIMIT),
        name="ssd_sample",
    )(padded(z), xp, padded(dt), ssm_state, *consts)
    y = y.reshape(db, SEQ_ROWS, SSD_W)[:, lead:].reshape(db * DEC_SEQ, SSD_W)
    return y, st


def kernel(x_prompt, x_sample, cache_k, cache_v, state_conv, state_ssm, rel_bias, g_mix, w_in, conv_w, conv_b,
           dt_bias, a_log, d_skip, g_attn_out, g_ssd_out, w_out, g_ffn, w_gate, w_up, w_down, g_final):
    depth = w_in.shape[0]
    bp, sp, _ = x_prompt.shape
    db, ds, _ = x_sample.shape
    assert bp == 1 and ds == DEC_SEQ and sp >= WIN_MAX
    xp = x_prompt.reshape(bp * sp, D_MODEL)
    xs = x_sample.reshape(db * ds, D_MODEL)
    pad_heads = lambda v: jnp.pad(v, (0, LANES - SSD_HEADS))[None]
    outs = [[] for _ in range(8)]
    for l in range(depth):
        w_t = w_in[l].T.astype(BF16)
        wdt = jnp.pad(w_t[IN_SPLITS[-1]:], ((0, LANES - SSD_HEADS), (0, 0)))
        ssd_prm = (conv_w[l], conv_b[l][None], pad_heads(dt_bias[l]), pad_heads(a_log[l]),
                   jnp.repeat(d_skip[l], SSD_HEAD_DIM)[None], g_ssd_out[l][None])
        tail_prm = (g_attn_out[l][None], w_out[l][:ATT_W].astype(BF16), w_out[l][ATT_W:].astype(BF16),
                    g_ffn[l][None], w_gate[l].astype(BF16), w_up[l].astype(BF16), w_down[l].astype(BF16),
                    g_final[None])
        last = l == depth - 1

        qkv_p, z_p, xbc_p, dt_p = _inproj(xp, g_mix[l][None], w_t, wdt)
        qkv_s, z_s, xbc_s, dt_s = _inproj(xs, g_mix[l][None], w_t, wdt)
        att_p = _attn_prompt(qkv_p, rel_bias)
        feature_major = lambda c: jnp.transpose(c, (0, 2, 3, 1)).reshape(db, ATT_W, c.shape[1])
        ssd_args = (z_p, xbc_p, dt_p, *ssd_prm)
        attn_args = (qkv_s, feature_major(cache_k[l]), feature_major(cache_v[l]), rel_bias)
        if sp // SSD_CHUNK == db:
            ssd_p, st_p, att_s = _ssd_prompt_attn_sample(ssd_args, attn_args)
        else:
            ssd_p, st_p = _ssd_prompt(*ssd_args)
            att_s = _attn_sample(*attn_args)
        xp = _tail(xp, att_p, ssd_p, *tail_prm, final_norm=last)
        ssd_s, st_s = _ssd_sample(z_s, xbc_s, dt_s, state_conv[l], state_ssm[l].reshape(db, SSD_W, D_STATE),
                                  *ssd_prm)
        xs = _tail(xs, att_s, ssd_s, *tail_prm, final_norm=last)

        heads = lambda a: a.reshape(a.shape[0], a.shape[1], ATT_HEADS, HEAD_DIM)
        kv_p = qkv_p.reshape(bp, sp, 3 * ATT_W)[:, sp - WIN_MAX:]
        kv_s = qkv_s.reshape(db, ds, 3 * ATT_W)
        outs[0].append(heads(kv_p[..., ATT_W:2 * ATT_W]))
        outs[1].append(heads(kv_p[..., 2 * ATT_W:]))
        outs[2].append(xbc_p.reshape(bp, sp, CONV_DIM)[:, sp - (CONV_W - 1):])
        outs[3].append(st_p.reshape(bp, SSD_HEADS, SSD_HEAD_DIM, D_STATE))
        outs[4].append(heads(kv_s[..., ATT_W:2 * ATT_W]))
        outs[5].append(heads(kv_s[..., 2 * ATT_W:]))
        outs[6].append(xbc_s.reshape(db, ds, CONV_DIM)[:, ds - (CONV_W - 1):])
        outs[7].append(st_s.reshape(db, SSD_HEADS, SSD_HEAD_DIM, D_STATE))
    return (xp.reshape(bp, sp, D_MODEL), xs.reshape(db, ds, D_MODEL)) + tuple(jnp.stack(o) for o in outs)
```
